```python
import math
import jax
import jax.numpy as jnp
from jax import lax
import numpy as np

D_MODEL = 2048
BATCH = 32
SEQ = 256
DEPTH = 4
DEC_BATCH = 2
DEC_SEQ = 1024
PAST_LEN = 512

GRID_W = 64
ROPE_THETA = 10000.0
Q_BLOCK = 128
N_BRANCH = 4
BRANCH_W = D_MODEL // 4

SSD_HEAD_DIM = 64
SSD_HEADS = BRANCH_W // SSD_HEAD_DIM
SSD_GROUPS = 2
SSD_STATE = 128
SSD_CONV = 5
SSD_CHUNK = 128
SSD_XBC = BRANCH_W + 2 * SSD_GROUPS * SSD_STATE
DIFF_HEADS = 4
DIFF_D = BRANCH_W // DIFF_HEADS // 2
MLA_HEADS = 4
MLA_NOPE = 128
MLA_ROPE = 64
MLA_V = BRANCH_W // MLA_HEADS
MLA_Q_RANK = D_MODEL // 4
MLA_KV_RANK = D_MODEL // 8
GQA_HEAD_DIM = 128
GQA_HEADS = BRANCH_W // GQA_HEAD_DIM
GQA_KV_HEADS = 2
N_EXPERTS = 32
TOP_K = 4
EXPERT_FF = D_MODEL // 2
SWIGLU_LIMIT = 7.0
SWIGLU_ALPHA = 1.702
MOE_BLOCK = 128

DEEPNORM_ALPHA = (2.0 * DEPTH) ** 0.25
DEEPNORM_BETA = (8.0 * DEPTH) ** -0.25

IN_SPLIT_SIZES = (
    BRANCH_W,
    SSD_XBC,
    2 * SSD_HEADS,
    DIFF_HEADS * 2 * DIFF_D,
    DIFF_HEADS * 2 * DIFF_D,
    DIFF_HEADS * 2 * DIFF_D,
    MLA_Q_RANK,
    MLA_KV_RANK,
    MLA_ROPE,
    GQA_HEADS * GQA_HEAD_DIM,
    GQA_KV_HEADS * GQA_HEAD_DIM,
    GQA_KV_HEADS * GQA_HEAD_DIM,
    N_BRANCH * D_MODEL,
)
IN_WIDTH = sum(IN_SPLIT_SIZES)

kernel_name = 'hybrid_diffusion_prefix_trunk_step'

F32 = jnp.float32


def split_cols(a, sizes):
    return jnp.split(a, np.cumsum(sizes)[:-1].tolist(), axis=-1)


def rms_norm(x, g, eps=1e-6):
    xf = x.astype(F32)
    y = xf * lax.rsqrt(jnp.mean(xf * xf, axis=-1, keepdims=True) + eps)
    return (y * g.astype(F32)).astype(x.dtype)


def layer_norm(x, g=None, b=None, eps=1e-6):
    xf = x.astype(F32)
    mu = jnp.mean(xf, axis=-1, keepdims=True)
    var = jnp.mean(jnp.square(xf - mu), axis=-1, keepdims=True)
    y = (xf - mu) * lax.rsqrt(var + eps)
    if g is not None:
        y = y * g.astype(F32) + b.astype(F32)
    return y.astype(x.dtype)


def axial_rope_tables(n_tokens, rot_dim):
    rows = n_tokens // GRID_W
    row = jnp.repeat(jnp.arange(rows, dtype=F32), GRID_W)
    col = jnp.tile(jnp.arange(GRID_W, dtype=F32), rows)
    n_freq = rot_dim // 4
    inv = ROPE_THETA ** (-jnp.arange(n_freq, dtype=F32) / n_freq)
    ang = jnp.concatenate([row[:, None] * inv, col[:, None] * inv], axis=-1)
    return jnp.cos(ang), jnp.sin(ang)


def apply_rope(x, cos, sin):
    shape = (1, cos.shape[0]) + (1,) * (x.ndim - 3) + (cos.shape[1],)
    c, s = cos.reshape(shape), sin.reshape(shape)
    x1, x2 = jnp.split(x.astype(F32), 2, axis=-1)
    return jnp.concatenate([x1 * c - x2 * s, x1 * s + x2 * c], axis=-1).astype(x.dtype)


def sweep_query_blocks(fn, *qs):
    b, L = qs[0].shape[:2]
    n = L // Q_BLOCK
    blocks = tuple(jnp.swapaxes(q.reshape((b, n, Q_BLOCK) + q.shape[2:]), 0, 1) for q in qs)
    out = lax.map(lambda blk: fn(*blk), blocks)
    return jnp.swapaxes(out, 0, 1).reshape((b, L) + out.shape[3:])


def dwconv_centred(x, w, bias):
    k, ch = w.shape
    y = lax.conv_general_dilated(x, w.reshape(k, 1, ch).astype(x.dtype), window_strides=(1,),
                                 padding=((k // 2, k // 2),), dimension_numbers=('NWC', 'WIO', 'NWC'),
                                 feature_group_count=ch)
    return y + bias.astype(x.dtype)


def ssd_chunked(x, dt, a_head, bm, cm, h0):
    b, L, H, P = x.shape
    G, N = bm.shape[2], bm.shape[3]
    nc, Q = L // SSD_CHUNK, SSD_CHUNK
    rep = H // G
    xf = x.astype(F32).reshape(b, nc, Q, H, P)
    bh = jnp.repeat(bm.astype(F32), rep, axis=2).reshape(b, nc, Q, H, N)
    ch = jnp.repeat(cm.astype(F32), rep, axis=2).reshape(b, nc, Q, H, N)
    dtc = dt.reshape(b, nc, Q, H)
    a_cum = jnp.cumsum(dtc * a_head, axis=2)
    mask = jnp.tril(jnp.ones((Q, Q), dtype=bool))[None, None, :, :, None]
    seg = a_cum[:, :, :, None, :] - a_cum[:, :, None, :, :]
    decay = jnp.where(mask, jnp.exp(jnp.where(mask, seg, 0.0)), 0.0)
    cb = jnp.einsum('bcihn,bcjhn->bcijh', ch, bh)
    y_diag = jnp.einsum('bcijh,bcjhp->bcihp', cb * decay * dtc[:, :, None, :, :], xf)
    to_end = jnp.exp(a_cum[:, :, -1:, :] - a_cum)
    states = jnp.einsum('bcjhn,bcjhp->bchpn', bh * (to_end * dtc)[..., None], xf)
    chunk_decay = jnp.exp(a_cum[:, :, -1, :])

    def step(h, inp):
        s, d = inp
        return d[:, :, None, None] * h + s, h

    h_last, h_in = lax.scan(step, h0, (jnp.moveaxis(states, 1, 0), jnp.moveaxis(chunk_decay, 1, 0)))
    h_in = jnp.moveaxis(h_in, 0, 1)
    y_off = jnp.einsum('bcihn,bchpn->bcihp', ch * jnp.exp(a_cum)[..., None], h_in)
    return (y_diag + y_off).reshape(b, L, H, P), h_last


def diff_attention(q, k, v, lam):
    scale = DIFF_D ** -0.5

    def block(qb):
        s = jnp.einsum('bqhcd,bkhcd->bchqk', qb, k).astype(F32) * scale
        p = jax.nn.softmax(s, axis=-1)
        att = p[:, 0] - lam * p[:, 1]
        return jnp.einsum('bhqk,bkhe->bqhe', att.astype(v.dtype), v)

    return sweep_query_blocks(block, q)


def mla_attention(q_nope, q_rope, k_nope, k_rope, v):
    scale = (MLA_NOPE + MLA_ROPE) ** -0.5

    def block(qn, qr):
        s = (jnp.einsum('bqhd,bkhd->bhqk', qn, k_nope)
             + jnp.einsum('bqhr,bkr->bhqk', qr, k_rope)).astype(F32) * scale
        p = jax.nn.softmax(s, axis=-1)
        return jnp.einsum('bhqk,bkhd->bqhd', p.astype(v.dtype), v)

    return sweep_query_blocks(block, q_nope, q_rope)


def gqa_attention(q, k, v):
    scale = GQA_HEAD_DIM ** -0.5

    def block(qb):
        s = jnp.einsum('bqhgd,bkhd->bhgqk', qb, k).astype(F32) * scale
        p = jax.nn.softmax(s, axis=-1)
        return jnp.einsum('bhgqk,bkhd->bqhgd', p.astype(v.dtype), v)

    return sweep_query_blocks(block, q)


def mixer_sublayer(u, lp, layer_idx, ctx):
    b, L, _ = u.shape
    latent = ctx is not None

    def rope(t, dim):
        if not latent:
            return t
        cos, sin = axial_rope_tables(L, dim)
        return apply_rope(t, cos, sin)

    proj = u @ lp['w_in']
    (s_z, s_xbc, s_dt, d_q, d_k, d_v, m_cq, m_ckv, m_kr,
     g_q, g_k, g_v, gate_logits) = split_cols(proj, IN_SPLIT_SIZES)

    xbc = jax.nn.silu(dwconv_centred(s_xbc, lp['ssd_conv_w'], lp['ssd_conv_b']))
    xs, bm, cm = split_cols(xbc, (BRANCH_W, SSD_GROUPS * SSD_STATE, SSD_GROUPS * SSD_STATE))
    xs = xs.reshape(b, L, SSD_HEADS, SSD_HEAD_DIM)
    bm = bm.reshape(b, L, SSD_GROUPS, SSD_STATE)
    cm = cm.reshape(b, L, SSD_GROUPS, SSD_STATE)
    dt = jax.nn.softplus(s_dt.reshape(b, L, 2, SSD_HEADS).astype(F32) + lp['ssd_dt_bias'].astype(F32))
    a_dir = -jnp.exp(lp['ssd_a_log'].astype(F32))
    if latent:
        h0 = ctx[0].astype(F32)
    else:
        h0 = jnp.zeros((b, 2, SSD_HEADS, SSD_HEAD_DIM, SSD_STATE), F32)
    y_f, h_f = ssd_chunked(xs, dt[:, :, 0], a_dir[0], bm, cm, h0[:, 0])
    y_b, h_b = ssd_chunked(jnp.flip(xs, 1), jnp.flip(dt[:, :, 1], 1), a_dir[1],
                           jnp.flip(bm, 1), jnp.flip(cm, 1), h0[:, 1])
    y_ssd = y_f + jnp.flip(y_b, 1) + lp['ssd_d'].astype(F32)[:, None] * xs.astype(F32)
    o_ssd = rms_norm(y_ssd.reshape(b, L, BRANCH_W).astype(u.dtype) * jax.nn.silu(s_z), lp['ssd_norm'])

    dq = rope(d_q.reshape(b, L, DIFF_HEADS, 2, DIFF_D), DIFF_D)
    dk = rope(d_k.reshape(b, L, DIFF_HEADS, 2, DIFF_D), DIFF_D)
    dv = d_v.reshape(b, L, DIFF_HEADS, 2 * DIFF_D)
    if latent:
        dk_all = jnp.concatenate([ctx[1].reshape(b, -1, DIFF_HEADS, 2, DIFF_D), dk], axis=1)
        dv_all = jnp.concatenate([ctx[2], dv], axis=1)
    else:
        dk_all, dv_all = dk, dv
    lam_init = 0.8 - 0.6 * math.exp(-0.3 * layer_idx)
    lv = lp['diff_lambda'].astype(F32)
    lam = jnp.exp(jnp.sum(lv[0] * lv[1])) - jnp.exp(jnp.sum(lv[2] * lv[3])) + lam_init
    o_diff = diff_attention(dq, dk_all, dv_all, lam)
    o_diff = (rms_norm(o_diff, lp['diff_norm']) * (1.0 - lam_init)).reshape(b, L, BRANCH_W)

    cq = (rms_norm(m_cq, lp['mla_q_norm']) @ lp['mla_w_uq']).reshape(b, L, MLA_HEADS, MLA_NOPE + MLA_ROPE)
    q_nope = cq[..., :MLA_NOPE]
    q_rope = rope(cq[..., MLA_NOPE:], MLA_ROPE)
    ckv = rms_norm(m_ckv, lp['mla_kv_norm'])
    kr = rope(m_kr, MLA_ROPE)
    if latent:
        ckv_all = jnp.concatenate([ctx[3], ckv], axis=1)
        kr_all = jnp.concatenate([ctx[4], kr], axis=1)
    else:
        ckv_all, kr_all = ckv, kr
    lk = ckv_all.shape[1]
    k_nope = (ckv_all @ lp['mla_w_uk']).reshape(b, lk, MLA_HEADS, MLA_NOPE)
    v_mla = (ckv_all @ lp['mla_w_uv']).reshape(b, lk, MLA_HEADS, MLA_V)
    o_mla = mla_attention(q_nope, q_rope, k_nope, kr_all, v_mla).reshape(b, L, BRANCH_W)

    gq = rope(rms_norm(g_q.reshape(b, L, GQA_HEADS, GQA_HEAD_DIM), lp['gqa_q_norm']), GQA_HEAD_DIM)
    gk = rope(rms_norm(g_k.reshape(b, L, GQA_KV_HEADS, GQA_HEAD_DIM), lp['gqa_k_norm']), GQA_HEAD_DIM)
    gv = g_v.reshape(b, L, GQA_KV_HEADS, GQA_HEAD_DIM)
    if latent:
        gk_all = jnp.concatenate([ctx[5], gk], axis=1)
        gv_all = jnp.concatenate([ctx[6], gv], axis=1)
    else:
        gk_all, gv_all = gk, gv
    gq5 = gq.reshape(b, L, GQA_KV_HEADS, GQA_HEADS // GQA_KV_HEADS, GQA_HEAD_DIM)
    o_gqa = gqa_attention(gq5, gk_all, gv_all).reshape(b, L, BRANCH_W)

    branches = jnp.stack([o_ssd, o_diff, o_mla, o_gqa], axis=2)
    proj_b = jnp.einsum('blnw,nwd->blnd', branches, lp['w_branch'])
    gates = jax.nn.sigmoid(gate_logits.reshape(b, L, N_BRANCH, D_MODEL).astype(F32)).astype(u.dtype)
    out = jnp.sum(gates * proj_b, axis=2) @ lp['w_out']

    if latent:
        return out, None
    ctx_out = (jnp.stack([h_f, h_b], axis=1).astype(u.dtype),
               dk.reshape(b, L, DIFF_HEADS, 2 * DIFF_D), dv, ckv, kr, gk, gv)
    return out, ctx_out


def moe_ffn(x, router_w, router_b, w_gu, b_gu, w_down, b_down):
    b, L, D = x.shape
    xt = x.reshape(-1, D)
    T = xt.shape[0]
    logits = (xt @ router_w + router_b).astype(F32)
    top_v, top_i = lax.top_k(logits, TOP_K)
    gate = jax.nn.softmax(top_v, axis=-1)
    A = T * TOP_K
    flat_e = top_i.reshape(-1)
    flat_t = jnp.repeat(jnp.arange(T, dtype=jnp.int32), TOP_K)
    order = jnp.argsort(flat_e)
    se = flat_e[order]
    counts = jnp.bincount(flat_e, length=N_EXPERTS)
    padded = (counts + MOE_BLOCK - 1) // MOE_BLOCK * MOE_BLOCK
    start = jnp.cumsum(counts) - counts
    ends_p = jnp.cumsum(padded)
    pstart = ends_p - padded
    dest = pstart[se] + jnp.arange(A) - start[se]
    n_blk = -(-A // MOE_BLOCK) + N_EXPERTS
    P = n_blk * MOE_BLOCK
    tok = jnp.zeros((P,), jnp.int32).at[dest].set(flat_t[order])
    wgt = jnp.zeros((P,), F32).at[dest].set(gate.reshape(-1)[order])
    blk_e = jnp.minimum(jnp.searchsorted(ends_p, jnp.arange(n_blk) * MOE_BLOCK, side='right'), N_EXPERTS - 1)
    xb = xt[tok].reshape(n_blk, MOE_BLOCK, D)

    def expert_block(args):
        xblk, e = args
        h = xblk @ w_gu[e] + b_gu[e]
        g, lin = jnp.split(h, 2, axis=-1)
        g = jnp.minimum(g, SWIGLU_LIMIT)
        lin = jnp.clip(lin, -SWIGLU_LIMIT, SWIGLU_LIMIT)
        act = (lin + 1.0) * (g * jax.nn.sigmoid(SWIGLU_ALPHA * g))
        return act @ w_down[e] + b_down[e]

    yb = lax.map(expert_block, (xb, blk_e)).reshape(P, D)
    y = jax.ops.segment_sum(yb * wgt[:, None].astype(yb.dtype), tok, num_segments=T)
    return y.reshape(b, L, D)


def trunk_layer(x, mod, lp, layer_idx, ctx):
    sh_a, sc_a, g_a, sh_f, sc_f, g_f = jnp.split(mod.astype(x.dtype), 6, axis=-1)
    u = layer_norm(x) * (1.0 + sc_a) + sh_a
    mix, ctx_out = mixer_sublayer(u, lp, layer_idx, ctx)
    x = layer_norm(DEEPNORM_ALPHA * x + g_a * mix, lp['ln_mix_g'], lp['ln_mix_b'])
    u = layer_norm(x) * (1.0 + sc_f) + sh_f
    ffn = moe_ffn(u, lp['router_w'], lp['router_b'], lp['exp_w_gu'], lp['exp_b_gu'],
                  lp['exp_w_down'], lp['exp_b_down'])
    x = layer_norm(DEEPNORM_ALPHA * x + g_f * ffn, lp['ln_ffn_g'], lp['ln_ffn_b'])
    return x, ctx_out


def setup_inputs(seed: int = 0) -> dict:
    key = jax.random.key(seed)
    ks = iter(jax.random.split(key, 64))

    def nrm(shape, scale):
        return jax.random.normal(next(ks), shape, F32) * scale

    def gain(shape):
        return 1.0 + nrm(shape, 0.05)

    dt0 = jnp.exp(jax.random.uniform(next(ks), (DEPTH, 2, SSD_HEADS), F32,
                                     minval=math.log(1e-3), maxval=math.log(1e-1)))
    a_log = jnp.log(jax.random.uniform(next(ks), (DEPTH, 2, SSD_HEADS), F32, minval=1.0, maxval=16.0))
    return {
        'x_prompt': nrm((BATCH, SEQ, D_MODEL), 1.0),
        'x_sample': nrm((DEC_BATCH, DEC_SEQ, D_MODEL), 1.0),
        'state_ssd': nrm((DEC_BATCH, DEPTH, 2, SSD_HEADS, SSD_HEAD_DIM, SSD_STATE), 0.1),
        'cache_diff_k': nrm((DEC_BATCH, DEPTH, PAST_LEN, DIFF_HEADS, 2 * DIFF_D), 1.0),
        'cache_diff_v': nrm((DEC_BATCH, DEPTH, PAST_LEN, DIFF_HEADS, 2 * DIFF_D), 1.0),
        'cache_mla_ckv': nrm((DEC_BATCH, DEPTH, PAST_LEN, MLA_KV_RANK), 1.0),
        'cache_mla_kr': nrm((DEC_BATCH, DEPTH, PAST_LEN, MLA_ROPE), 1.0),
        'cache_gqa_k': nrm((DEC_BATCH, DEPTH, PAST_LEN, GQA_KV_HEADS, GQA_HEAD_DIM), 1.0),
        'cache_gqa_v': nrm((DEC_BATCH, DEPTH, PAST_LEN, GQA_KV_HEADS, GQA_HEAD_DIM), 1.0),
        'c': nrm((DEC_BATCH, D_MODEL), 1.0),
        'c_ctx': nrm((D_MODEL,), 1.0),
        'w_mod': nrm((DEPTH, D_MODEL, 6 * D_MODEL), 0.5 * D_MODEL ** -0.5),
        'b_mod': nrm((DEPTH, 6 * D_MODEL), 0.02),
        'w_in': nrm((DEPTH, D_MODEL, IN_WIDTH), D_MODEL ** -0.5),
        'ssd_conv_w': nrm((DEPTH, SSD_CONV, SSD_XBC), SSD_CONV ** -0.5),
        'ssd_conv_b': nrm((DEPTH, SSD_XBC), 0.02),
        'ssd_dt_bias': dt0 + jnp.log(-jnp.expm1(-dt0)),
        'ssd_a_log': a_log,
        'ssd_d': gain((DEPTH, SSD_HEADS)),
        'ssd_norm': gain((DEPTH, BRANCH_W)),
        'diff_lambda': nrm((DEPTH, 4, DIFF_D), 0.1),
        'diff_norm': gain((DEPTH, 2 * DIFF_D)),
        'mla_q_norm': gain((DEPTH, MLA_Q_RANK)),
        'mla_w_uq': nrm((DEPTH, MLA_Q_RANK, MLA_HEADS * (MLA_NOPE + MLA_ROPE)), MLA_Q_RANK ** -0.5),
        'mla_kv_norm': gain((DEPTH, MLA_KV_RANK)),
        'mla_w_uk': nrm((DEPTH, MLA_KV_RANK, MLA_HEADS * MLA_NOPE), MLA_KV_RANK ** -0.5),
        'mla_w_uv': nrm((DEPTH, MLA_KV_RANK, MLA_HEADS * MLA_V), MLA_KV_RANK ** -0.5),
        'gqa_q_norm': gain((DEPTH, GQA_HEAD_DIM)),
        'gqa_k_norm': gain((DEPTH, GQA_HEAD_DIM)),
        'w_branch': nrm((DEPTH, N_BRANCH, BRANCH_W, D_MODEL), DEEPNORM_BETA * BRANCH_W ** -0.5),
        'w_out': nrm((DEPTH, D_MODEL, D_MODEL), DEEPNORM_BETA * D_MODEL ** -0.5),
        'ln_mix_g': gain((DEPTH, D_MODEL)),
        'ln_mix_b': nrm((DEPTH, D_MODEL), 0.02),
        'router_w': nrm((DEPTH, D_MODEL, N_EXPERTS), D_MODEL ** -0.5),
        'router_b': nrm((DEPTH, N_EXPERTS), 0.01),
        'exp_w_gu': nrm((DEPTH, N_EXPERTS, D_MODEL, 2 * EXPERT_FF), D_MODEL ** -0.5),
        'exp_b_gu': nrm((DEPTH, N_EXPERTS, 2 * EXPERT_FF), 0.02),
        'exp_w_down': nrm((DEPTH, N_EXPERTS, EXPERT_FF, D_MODEL), DEEPNORM_BETA * EXPERT_FF ** -0.5),
        'exp_b_down': nrm((DEPTH, N_EXPERTS, D_MODEL), 0.02),
        'ln_ffn_g': gain((DEPTH, D_MODEL)),
        'ln_ffn_b': nrm((DEPTH, D_MODEL), 0.02),
    }


def reference(x_prompt, x_sample, state_ssd, cache_diff_k, cache_diff_v, cache_mla_ckv, cache_mla_kr,
              cache_gqa_k, cache_gqa_v, c, c_ctx, w_mod, b_mod, w_in, ssd_conv_w, ssd_conv_b,
              ssd_dt_bias, ssd_a_log, ssd_d, ssd_norm, diff_lambda, diff_norm, mla_q_norm, mla_w_uq,
              mla_kv_norm, mla_w_uk, mla_w_uv, gqa_q_norm, gqa_k_norm, w_branch, w_out, ln_mix_g,
              ln_mix_b, router_w, router_b, exp_w_gu, exp_b_gu, exp_w_down, exp_b_down, ln_ffn_g,
              ln_ffn_b):
    layers = [dict(w_in=w_in[l], ssd_conv_w=ssd_conv_w[l], ssd_conv_b=ssd_conv_b[l],
                   ssd_dt_bias=ssd_dt_bias[l], ssd_a_log=ssd_a_log[l], ssd_d=ssd_d[l],
                   ssd_norm=ssd_norm[l], diff_lambda=diff_lambda[l], diff_norm=diff_norm[l],
                   mla_q_norm=mla_q_norm[l], mla_w_uq=mla_w_uq[l], mla_kv_norm=mla_kv_norm[l],
                   mla_w_uk=mla_w_uk[l], mla_w_uv=mla_w_uv[l], gqa_q_norm=gqa_q_norm[l],
                   gqa_k_norm=gqa_k_norm[l], w_branch=w_branch[l], w_out=w_out[l],
                   ln_mix_g=ln_mix_g[l], ln_mix_b=ln_mix_b[l], router_w=router_w[l],
                   router_b=router_b[l], exp_w_gu=exp_w_gu[l], exp_b_gu=exp_b_gu[l],
                   exp_w_down=exp_w_down[l], exp_b_down=exp_b_down[l], ln_ffn_g=ln_ffn_g[l],
                   ln_ffn_b=ln_ffn_b[l])
              for l in range(DEPTH)]

    h = x_prompt
    ctx_cols = ([], [], [], [], [], [], [])
    for l in range(DEPTH):
        mod = (jax.nn.silu(c_ctx) @ w_mod[l] + b_mod[l])[None, None, :]
        h, ctx_out = trunk_layer(h, mod, layers[l], l, None)
        for col, t in zip(ctx_cols, ctx_out):
            col.append(t)
    y_prompt = h
    new_state_ssd = jnp.stack(ctx_cols[0], axis=1)
    new_cache_diff_k = jnp.stack(ctx_cols[1], axis=1)
    new_cache_diff_v = jnp.stack(ctx_cols[2], axis=1)
    new_cache_mla_ckv = jnp.stack(ctx_cols[3], axis=1)
    new_cache_mla_kr = jnp.stack(ctx_cols[4], axis=1)
    new_cache_gqa_k = jnp.stack(ctx_cols[5], axis=1)
    new_cache_gqa_v = jnp.stack(ctx_cols[6], axis=1)

    z = x_sample
    for l in range(DEPTH):
        ctx = (state_ssd[:, l], cache_diff_k[:, l], cache_diff_v[:, l], cache_mla_ckv[:, l],
               cache_mla_kr[:, l], cache_gqa_k[:, l], cache_gqa_v[:, l])
        mod = (jax.nn.silu(c) @ w_mod[l] + b_mod[l])[:, None, :]
        z, _ = trunk_layer(z, mod, layers[l], l, ctx)
    y_sample = z

    return (y_prompt, y_sample, new_state_ssd, new_cache_diff_k, new_cache_diff_v,
            new_cache_mla_ckv, new_cache_mla_kr, new_cache_gqa_k, new_cache_gqa_v)
```

```python
import functools
import math

import jax
import jax.numpy as jnp
from jax import lax
from jax.experimental import pallas as pl
from jax.experimental.pallas import tpu as pltpu

F32 = jnp.float32
BF16 = jnp.bfloat16

D_MODEL = 2048
DEPTH = 4
GRID_W = 64
ROPE_THETA = 10000.0
BRANCH_W = 512
N_BRANCH = 4
SSD_HEADS = 8
SSD_HEAD_DIM = 64
SSD_STATE = 128
SSD_CONV = 5
SSD_CHUNK = 128
SSD_XBC = 1024
DIFF_HEADS = 4
DIFF_D = 64
MLA_HEADS = 4
MLA_NOPE = 128
MLA_ROPE = 64
MLA_Q_RANK = 512
MLA_KV_RANK = 256
GQA_HEAD_DIM = 128
GQA_HEADS = 4
GQA_KV_HEADS = 2
N_EXPERTS = 32
TOP_K = 4
EXPERT_FF = 1024
SWIGLU_LIMIT = 7.0
SWIGLU_ALPHA = 1.702
DEEPNORM_ALPHA = (2.0 * DEPTH) ** 0.25
EPS = 1e-6

LANES = 128
SUBLANES = 8

_O_DT = 1536
_O_DQ = 1552
_O_MKR = 3856
_O_GQ = 3920
_O_GATE = 4944
IN_WIDTH = 13136
C_XBC = 0
C_Z = 1024
C_DQ = 1536
C_DK = 2048
C_DV = 2560
C_MCQ = 3072
C_MCKV = 3584
C_GQ = 3840
C_GK = 4352
C_GV = 4608
C_SMALL = 4864
C_GATE = 5120
PROJ_W = C_GATE + N_BRANCH * D_MODEL
SMALL_DT = MLA_ROPE

VMEM_LIMIT = 52 * 1024 * 1024

TM_IN = 512
TN_IN = 1024
TM_MERGE = 512
TN_MERGE = 512
TM_LN = 512
TM_MOE = 256
TF_MOE = 512
TN_DOWN = 1024
TN_MOD = 1024
TQ = 256


def _cparams(sem):
    return pltpu.CompilerParams(dimension_semantics=sem, vmem_limit_bytes=VMEM_LIMIT)


def _dot(a, b):
    return jnp.dot(a, b, preferred_element_type=F32)


def _dot_nt(a, b):
    return lax.dot_general(a, b, (((1,), (1,)), ((), ())), preferred_element_type=F32)


def _ln(x):
    mu = jnp.mean(x, axis=-1, keepdims=True)
    xc = x - mu
    var = jnp.mean(xc * xc, axis=-1, keepdims=True)
    return xc * lax.rsqrt(var + EPS)


def _rms(x, g):
    return x * lax.rsqrt(jnp.mean(x * x, axis=-1, keepdims=True) + EPS) * g


def _sigmoid(x):
    return 1.0 / (1.0 + jnp.exp(-x))


def _silu(x):
    return x * _sigmoid(x)


def _rope(x, c, s, half):
    w = x.shape[-1]
    lane = lax.broadcasted_iota(jnp.int32, x.shape, 1)
    lo = (lane & (2 * half - 1)) < half
    swapped = jnp.where(lo, pltpu.roll(x, w - half, 1), pltpu.roll(x, half, 1))
    return x * c + swapped * s


def _softmax_pv(q, k, v, scale):
    s = _dot_nt(q, k) * scale
    m = jnp.max(s, axis=-1, keepdims=True)
    p = jnp.exp(s - m)
    l = jnp.sum(p, axis=-1, keepdims=True)
    return _dot(p.astype(BF16), v) / l


def _mod_row(i, tm, n_ctx_rows, lat_len):
    r = i * tm
    return jnp.where(r < n_ctx_rows, 0, 1 + (r - n_ctx_rows) // lat_len)


def _mod_kernel(c_ref, w_ref, b_ref, o_ref):
    c = c_ref[...]
    o_ref[...] = _dot(_silu(c).astype(BF16), w_ref[...].astype(BF16)) + b_ref[...]


def modulation(cvec, w_mod, b_mod):
    n = w_mod.shape[-1]
    return pl.pallas_call(
        _mod_kernel,
        grid=(DEPTH, n // TN_MOD),
        in_specs=[
            pl.BlockSpec((SUBLANES, D_MODEL), lambda l, j: (0, 0)),
            pl.BlockSpec((None, D_MODEL, TN_MOD), lambda l, j: (l, 0, j)),
            pl.BlockSpec((None, 1, TN_MOD), lambda l, j: (l, 0, j)),
        ],
        out_specs=pl.BlockSpec((None, SUBLANES, TN_MOD), lambda l, j: (l, 0, j)),
        out_shape=jax.ShapeDtypeStruct((DEPTH, SUBLANES, n), F32),
        compiler_params=_cparams(("parallel", "parallel")),
        name="modulation",
    )(cvec, w_mod, b_mod.reshape(DEPTH, 1, n))


def _inproj_kernel(x_ref, sh_ref, sc_ref, w_ref, o_ref, u_sc):
    @pl.when(pl.program_id(1) == 0)
    def _():
        u = _ln(x_ref[...]) * (1.0 + sc_ref[...]) + sh_ref[...]
        u_sc[...] = u.astype(BF16)

    o_ref[...] = _dot(u_sc[...], w_ref[...])


def in_projection(x, mods, w_in_p, layer, n_ctx_rows, lat_len):
    t = x.shape[0]
    tm = TM_IN
    row = functools.partial(_mod_row, tm=tm, n_ctx_rows=n_ctx_rows, lat_len=lat_len)
    return pl.pallas_call(
        _inproj_kernel,
        grid=(t // tm, PROJ_W // TN_IN),
        in_specs=[
            pl.BlockSpec((tm, D_MODEL), lambda i, j: (i, 0)),
            pl.BlockSpec((None, 1, D_MODEL), lambda i, j: (row(i), 0, 0)),
            pl.BlockSpec((None, 1, D_MODEL), lambda i, j: (row(i), 0, 1)),
            pl.BlockSpec((None, D_MODEL, TN_IN), lambda i, j: (layer, 0, j)),
        ],
        out_specs=pl.BlockSpec((tm, TN_IN), lambda i, j: (i, j)),
        out_shape=jax.ShapeDtypeStruct((t, PROJ_W), F32),
        scratch_shapes=[pltpu.VMEM((tm, D_MODEL), BF16)],
        compiler_params=_cparams(("parallel", "arbitrary")),
        name="in_projection",
    )(x, mods, mods, w_in_p)


def _lane_col(a, lane):
    idx = lax.broadcasted_iota(jnp.int32, a.shape, 1)
    return jnp.sum(jnp.where(idx == lane, a, 0.0), axis=1, keepdims=True)


def _ssd_kernel(*refs, seq, latent):
    if latent:
        (z_ref, xbc_ref, small_ref, cw_ref, cb_ref, dtb_ref, arow_ref, drow_ref, nw_ref, h0_ref,
         o_ref, xpad_sc, xs_sc, y_sc, dt_sc, h_sc) = refs
    else:
        (z_ref, xbc_ref, small_ref, cw_ref, cb_ref, dtb_ref, arow_ref, drow_ref, nw_ref,
         o_ref, hout_ref, xpad_sc, xs_sc, y_sc, dt_sc, h_sc) = refs
    q = SSD_CHUNK
    nc = seq // q
    pad = SUBLANES
    ext = q + 2 * pad

    xpad_sc[0:pad, :] = jnp.zeros((pad, SSD_XBC), F32)
    xpad_sc[pad + seq:2 * pad + seq, :] = jnp.zeros((pad, SSD_XBC), F32)

    def copy_in(c, carry):
        r0 = pl.multiple_of(c * q, q)
        xpad_sc[pl.ds(pad + r0, q), :] = xbc_ref[pl.ds(r0, q), :]
        return carry

    lax.fori_loop(0, nc, copy_in, 0)

    def conv(c, carry):
        r0 = pl.multiple_of(c * q, q)
        for ct in range(SSD_XBC // LANES):
            cols = slice(ct * LANES, (ct + 1) * LANES)
            win = xpad_sc[pl.ds(r0, ext), cols]
            acc = cb_ref[:, cols] + win[pad:pad + q] * cw_ref[2:3, cols]
            for k in (0, 1, 3, 4):
                shifted = pltpu.roll(win, (2 - k) % ext, 0)[pad:pad + q]
                acc = acc + shifted * cw_ref[k:k + 1, cols]
            xs_sc[pl.ds(r0, q), cols] = _silu(acc)
        sd = small_ref[pl.ds(r0, q), :] + dtb_ref[...]
        dt_sc[pl.ds(r0, q), :] = jnp.maximum(sd, 0.0) + jnp.log(1.0 + jnp.exp(-jnp.abs(sd)))
        return carry

    lax.fori_loop(0, nc, conv, 0)

    if latent:
        h_sc[...] = h0_ref[...]
    else:
        h_sc[...] = jnp.zeros(h_sc.shape, F32)

    ri = lax.broadcasted_iota(jnp.int32, (q, q), 0)
    ci = lax.broadcasted_iota(jnp.int32, (q, q), 1)
    lane = lax.broadcasted_iota(jnp.int32, (q, LANES), 1)
    lane_lo = lane < SSD_HEAD_DIM
    row_lo = lax.broadcasted_iota(jnp.int32, (2 * SSD_HEAD_DIM, q), 0) < SSD_HEAD_DIM
    row_lo_n = lax.broadcasted_iota(jnp.int32, (2 * SSD_HEAD_DIM, SSD_STATE), 0) < SSD_HEAD_DIM

    def chunk(c, d):
        keep = (ci <= ri) if d == 0 else (ci >= ri)
        tri = jnp.where(keep, 1.0, 0.0)
        r0 = pl.multiple_of(c * q, q)
        xc = xs_sc[pl.ds(r0, q), 0:BRANCH_W]
        dtc = dt_sc[pl.ds(r0, q), :]
        ac = dtc * arow_ref[...]
        acum = jnp.dot(tri, ac, preferred_element_type=F32, precision=lax.Precision.HIGHEST)
        acum_t = acum.T
        dt_t = dtc.T
        tot = acum[q - 1:q, :] if d == 0 else acum[0:1, :]
        eac = jnp.exp(acum)
        w_t = (jnp.exp(tot - acum) * dtc).T
        etot = jnp.exp(tot)
        x_t = xc.T
        for g in range(2):
            bg = xs_sc[pl.ds(r0, q), BRANCH_W + g * SSD_STATE:BRANCH_W + (g + 1) * SSD_STATE].astype(BF16)
            cg = xs_sc[pl.ds(r0, q), BRANCH_W + 2 * SSD_STATE + g * SSD_STATE:
                       BRANCH_W + 2 * SSD_STATE + (g + 1) * SSD_STATE].astype(BF16)
            cb = _dot_nt(cg, bg)
            for pr in range(2):
                pair = g * 2 + pr
                h0 = pair * 2
                l0 = SMALL_DT + d * SSD_HEADS + h0
                atts = []
                for hh in range(2):
                    ln = l0 + hh
                    seg = _lane_col(acum, ln) - acum_t[ln:ln + 1, :]
                    dec = jnp.exp(jnp.where(keep, seg, -1e30))
                    atts.append(cb * dec * dt_t[ln:ln + 1, :])
                att = jnp.concatenate(atts, axis=1).astype(BF16)
                cols = slice(h0 * SSD_HEAD_DIM, (h0 + 2) * SSD_HEAD_DIM)
                xp = xc[:, cols]
                x2 = jnp.concatenate([jnp.where(lane_lo, xp, 0.0), jnp.where(lane_lo, 0.0, xp)],
                                     axis=0).astype(BF16)
                y_diag = _dot(att, x2)
                st_in = h_sc[d, pair]
                y_off = _dot_nt(cg, st_in.astype(BF16))
                esel = jnp.where(lane_lo, _lane_col(eac, l0), _lane_col(eac, l0 + 1))
                y = y_diag + y_off * esel
                if d == 0:
                    y_sc[pl.ds(r0, q), cols] = y
                else:
                    y_sc[pl.ds(r0, q), cols] = y_sc[pl.ds(r0, q), cols] + y
                wsel = jnp.where(row_lo, w_t[l0:l0 + 1, :], w_t[l0 + 1:l0 + 2, :])
                st_new = _dot((x_t[cols, :] * wsel).astype(BF16), bg)
                dsel = jnp.where(row_lo_n, _lane_col(etot, l0), _lane_col(etot, l0 + 1))
                h_sc[d, pair] = dsel * st_in + st_new

    def fwd(c, carry):
        chunk(c, 0)
        return carry

    def bwd(i, carry):
        chunk(nc - 1 - i, 1)
        return carry

    lax.fori_loop(0, nc, fwd, 0)
    lax.fori_loop(0, nc, bwd, 0)

    def finish(c, carry):
        r0 = pl.multiple_of(c * q, q)
        y = y_sc[pl.ds(r0, q), :] + drow_ref[...] * xs_sc[pl.ds(r0, q), 0:BRANCH_W]
        gated = y * _silu(z_ref[pl.ds(r0, q), :])
        o_ref[pl.ds(r0, q), :] = _rms(gated, nw_ref[...]).astype(BF16)
        return carry

    lax.fori_loop(0, nc, finish, 0)
    if not latent:
        hout_ref[...] = h_sc[...]


def ssd_branch(proj, row0, batch, seq, lp, layer, state=None):
    latent = state is not None
    rb = row0 // seq
    n_pair = SSD_HEADS // 2
    st_shape = (2, n_pair, 2 * SSD_HEAD_DIM, SSD_STATE)
    const = lambda b: (0, 0)
    in_specs = [
        pl.BlockSpec((seq, BRANCH_W), lambda b: (rb + b, C_Z // BRANCH_W)),
        pl.BlockSpec((seq, SSD_XBC), lambda b: (rb + b, C_XBC // SSD_XBC)),
        pl.BlockSpec((seq, LANES), lambda b: (rb + b, C_SMALL // LANES)),
        pl.BlockSpec((SUBLANES, SSD_XBC), const),
        pl.BlockSpec((1, SSD_XBC), const),
        pl.BlockSpec((1, LANES), const),
        pl.BlockSpec((1, LANES), const),
        pl.BlockSpec((1, BRANCH_W), const),
        pl.BlockSpec((1, BRANCH_W), const),
    ]
    args = [proj, proj, proj, lp["conv_w"], lp["conv_b"], lp["dt_bias"], lp["a_row"], lp["d_row"],
            lp["ssd_norm"]]
    out_specs = [pl.BlockSpec((seq, BRANCH_W), lambda b: (b, 0))]
    out_shape = [jax.ShapeDtypeStruct((batch * seq, BRANCH_W), BF16)]
    if latent:
        in_specs.append(pl.BlockSpec((None, None) + st_shape, lambda b: (b, layer, 0, 0, 0, 0)))
        args.append(state)
    else:
        out_specs.append(pl.BlockSpec((None,) + st_shape, lambda b: (b, 0, 0, 0, 0)))
        out_shape.append(jax.ShapeDtypeStruct((batch,) + st_shape, F32))
    res = pl.pallas_call(
        functools.partial(_ssd_kernel, seq=seq, latent=latent),
        grid=(batch,),
        in_specs=in_specs,
        out_specs=out_specs,
        out_shape=out_shape,
        scratch_shapes=[
            pltpu.VMEM((seq + 2 * SUBLANES, SSD_XBC), F32),
            pltpu.VMEM((seq, SSD_XBC), F32),
            pltpu.VMEM((seq, BRANCH_W), F32),
            pltpu.VMEM((seq, LANES), F32),
            pltpu.VMEM(st_shape, F32),
        ],
        compiler_params=_cparams(("parallel",)),
        name="ssd_latent" if latent else "ssd_context",
    )(*args)
    return res if not latent else res[0]


def _diff_kernel(*refs, seq, ctx_len, latent, lam_init):
    if latent:
        (q_ref, k_ref, v_ref, kc_ref, vc_ref, c_ref, s_ref, lam_ref, nw_ref, o_ref, k_sc, v_sc) = refs
    else:
        (q_ref, k_ref, v_ref, lam_ref, nw_ref, o_ref, k_sc, v_sc) = refs
    lk = ctx_len + seq
    half = DIFF_D // 2
    if latent:
        k_sc[0:ctx_len, :] = kc_ref[...].astype(BF16)
        v_sc[0:ctx_len, :] = vc_ref[...].astype(BF16)
        kn = _rope(k_ref[...], c_ref[...], s_ref[...], half)
    else:
        kn = k_ref[...]
    k_sc[ctx_len:lk, :] = kn.astype(BF16)
    v_sc[ctx_len:lk, :] = v_ref[...].astype(BF16)
    lv = lam_ref[...]
    lam = (jnp.exp(jnp.sum(lv[0:1] * lv[1:2], axis=1, keepdims=True))
           - jnp.exp(jnp.sum(lv[2:3] * lv[3:4], axis=1, keepdims=True)) + lam_init)
    scale = DIFF_D ** -0.5
    lane_lo = lax.broadcasted_iota(jnp.int32, (TQ, LANES), 1) < DIFF_D

    def qtile(t, carry):
        q0 = pl.multiple_of(t * TQ, TQ)
        qv = q_ref[pl.ds(q0, TQ), :]
        if latent:
            qv = _rope(qv, c_ref[pl.ds(q0, TQ), :], s_ref[pl.ds(q0, TQ), :], half)
        q1 = jnp.where(lane_lo, qv, 0.0).astype(BF16)
        q2 = jnp.where(lane_lo, 0.0, qv).astype(BF16)
        k = k_sc[...]
        v = v_sc[...]
        o = _softmax_pv(q1, k, v, scale) - lam * _softmax_pv(q2, k, v, scale)
        o_ref[pl.ds(q0, TQ), :] = (_rms(o, nw_ref[...]) * (1.0 - lam_init)).astype(BF16)
        return carry

    lax.fori_loop(0, seq // TQ, qtile, 0)


def diff_branch(proj, row0, batch, seq, lp, layer, ctx=None):
    latent = ctx is not None
    rb = row0 // seq
    ctx_len = ctx[0].shape[2] if latent else 0
    lam_init = 0.8 - 0.6 * math.exp(-0.3 * layer)
    wid = 2 * DIFF_D
    in_specs = [
        pl.BlockSpec((seq, wid), lambda b, h: (rb + b, C_DQ // wid + h)),
        pl.BlockSpec((seq, wid), lambda b, h: (rb + b, C_DK // wid + h)),
        pl.BlockSpec((seq, wid), lambda b, h: (rb + b, C_DV // wid + h)),
    ]
    args = [proj, proj, proj]
    if latent:
        in_specs += [
            pl.BlockSpec((None, None, ctx_len, wid), lambda b, h: (b, layer, 0, h)),
            pl.BlockSpec((None, None, ctx_len, wid), lambda b, h: (b, layer, 0, h)),
            pl.BlockSpec((seq, wid), lambda b, h: (0, 0)),
            pl.BlockSpec((seq, wid), lambda b, h: (0, 0)),
        ]
        args += [ctx[0], ctx[1], lp["rope_c32"], lp["rope_s32"]]
    in_specs += [
        pl.BlockSpec((4, DIFF_D), lambda b, h: (0, 0)),
        pl.BlockSpec((1, wid), lambda b, h: (0, 0)),
    ]
    args += [lp["diff_lambda"], lp["diff_norm"]]
    return pl.pallas_call(
        functools.partial(_diff_kernel, seq=seq, ctx_len=ctx_len, latent=latent, lam_init=lam_init),
        grid=(batch, DIFF_HEADS),
        in_specs=in_specs,
        out_specs=pl.BlockSpec((seq, wid), lambda b, h: (b, h)),
        out_shape=jax.ShapeDtypeStruct((batch * seq, BRANCH_W), BF16),
        scratch_shapes=[pltpu.VMEM((ctx_len + seq, wid), BF16), pltpu.VMEM((ctx_len + seq, wid), BF16)],
        compiler_params=_cparams(("parallel", "parallel")),
        name="diff_latent" if latent else "diff_context",
    )(*args)


def _gqa_kernel(*refs, seq, ctx_len, latent):
    if latent:
        (q_ref, k_ref, v_ref, kc_ref, vc_ref, c_ref, s_ref, qw_ref, kw_ref, o_ref, k_sc, v_sc) = refs
    else:
        (q_ref, k_ref, v_ref, qw_ref, kw_ref, o_ref, kout_ref, k_sc, v_sc) = refs
    lk = ctx_len + seq
    half = GQA_HEAD_DIM // 2
    hd = GQA_HEAD_DIM
    kn = _rms(k_ref[...], kw_ref[...])
    if latent:
        k_sc[0:ctx_len, :] = kc_ref[...].astype(BF16)
        v_sc[0:ctx_len, :] = vc_ref[...].astype(BF16)
        kn = _rope(kn, c_ref[...], s_ref[...], half)
    else:
        kout_ref[...] = kn
    k_sc[ctx_len:lk, :] = kn.astype(BF16)
    v_sc[ctx_len:lk, :] = v_ref[...].astype(BF16)
    scale = GQA_HEAD_DIM ** -0.5

    def qtile(t, carry):
        q0 = pl.multiple_of(t * TQ, TQ)
        qs = []
        for g in range(GQA_HEADS // GQA_KV_HEADS):
            qv = _rms(q_ref[pl.ds(q0, TQ), g * hd:(g + 1) * hd], qw_ref[...])
            if latent:
                qv = _rope(qv, c_ref[pl.ds(q0, TQ), :], s_ref[pl.ds(q0, TQ), :], half)
            qs.append(qv)
        qq = jnp.concatenate(qs, axis=0).astype(BF16)
        o = _softmax_pv(qq, k_sc[...], v_sc[...], scale)
        for g in range(GQA_HEADS // GQA_KV_HEADS):
            o_ref[pl.ds(q0, TQ), g * hd:(g + 1) * hd] = o[g * TQ:(g + 1) * TQ].astype(BF16)
        return carry

    lax.fori_loop(0, seq // TQ, qtile, 0)


def gqa_branch(proj, row0, batch, seq, lp, layer, ctx=None):
    latent = ctx is not None
    rb = row0 // seq
    ctx_len = ctx[0].shape[2] if latent else 0
    hd = GQA_HEAD_DIM
    grp = GQA_HEADS // GQA_KV_HEADS
    in_specs = [
        pl.BlockSpec((seq, grp * hd), lambda b, h: (rb + b, C_GQ // (grp * hd) + h)),
        pl.BlockSpec((seq, hd), lambda b, h: (rb + b, C_GK // hd + h)),
        pl.BlockSpec((seq, hd), lambda b, h: (rb + b, C_GV // hd + h)),
    ]
    args = [proj, proj, proj]
    if latent:
        in_specs += [
            pl.BlockSpec((None, None, ctx_len, hd), lambda b, h: (b, layer, 0, h)),
            pl.BlockSpec((None, None, ctx_len, hd), lambda b, h: (b, layer, 0, h)),
            pl.BlockSpec((seq, hd), lambda b, h: (0, 0)),
            pl.BlockSpec((seq, hd), lambda b, h: (0, 0)),
        ]
        args += [ctx[0], ctx[1], lp["rope_c64"], lp["rope_s64"]]
    in_specs += [pl.BlockSpec((1, hd), lambda b, h: (0, 0)), pl.BlockSpec((1, hd), lambda b, h: (0, 0))]
    args += [lp["gqa_q_norm"], lp["gqa_k_norm"]]
    out_specs = [pl.BlockSpec((seq, grp * hd), lambda b, h: (b, h))]
    out_shape = [jax.ShapeDtypeStruct((batch * seq, BRANCH_W), BF16)]
    if not latent:
        out_specs.append(pl.BlockSpec((seq, hd), lambda b, h: (b, h)))
        out_shape.append(jax.ShapeDtypeStruct((batch * seq, GQA_KV_HEADS * hd), F32))
    res = pl.pallas_call(
        functools.partial(_gqa_kernel, seq=seq, ctx_len=ctx_len, latent=latent),
        grid=(batch, GQA_KV_HEADS),
        in_specs=in_specs,
        out_specs=out_specs,
        out_shape=out_shape,
        scratch_shapes=[pltpu.VMEM((ctx_len + seq, hd), BF16), pltpu.VMEM((ctx_len + seq, hd), BF16)],
        compiler_params=_cparams(("parallel", "parallel")),
        name="gqa_latent" if latent else "gqa_context",
    )(*args)
    return res if not latent else res[0]


def _mla_kernel(*refs, seq, ctx_len, latent):
    if latent:
        (cq_ref, ckv_ref, small_ref, ckvc_ref, krc_ref, c_ref, s_ref, qw_ref, kvw_ref, wuq_ref, wuk_ref,
         wuv_ref, o_ref, kn_sc, v_sc, kr_sc) = refs
    else:
        (cq_ref, ckv_ref, small_ref, qw_ref, kvw_ref, wuq_ref, wuk_ref, wuv_ref,
         o_ref, ckv_out_ref, kn_sc, v_sc, kr_sc) = refs
    lk = ctx_len + seq
    half = MLA_ROPE // 2
    ckv = _rms(ckv_ref[...], kvw_ref[...])
    if latent:
        cc = ckvc_ref[...].astype(BF16)
        kn_sc[0:ctx_len, :] = _dot(cc, wuk_ref[...]).astype(BF16)
        v_sc[0:ctx_len, :] = _dot(cc, wuv_ref[...]).astype(BF16)
        kr_sc[0:ctx_len, :] = krc_ref[...].astype(BF16)
    else:
        ckv_out_ref[...] = ckv
    cb = ckv.astype(BF16)
    kn_sc[ctx_len:lk, :] = _dot(cb, wuk_ref[...]).astype(BF16)
    v_sc[ctx_len:lk, :] = _dot(cb, wuv_ref[...]).astype(BF16)
    kr = small_ref[...]
    if latent:
        kr = _rope(kr, c_ref[...], s_ref[...], half)
    lane_lo = lax.broadcasted_iota(jnp.int32, (seq, LANES), 1) < MLA_ROPE
    kr_sc[ctx_len:lk, :] = jnp.where(lane_lo, kr, 0.0).astype(BF16)
    scale = (MLA_NOPE + MLA_ROPE) ** -0.5
    nope_w = MLA_HEADS * MLA_NOPE

    def qtile(t, carry):
        q0 = pl.multiple_of(t * TQ, TQ)
        cq = _dot(_rms(cq_ref[pl.ds(q0, TQ), :], qw_ref[...]).astype(BF16), wuq_ref[...])
        for h in range(MLA_HEADS):
            hs = slice(h * MLA_NOPE, (h + 1) * MLA_NOPE)
            qn = cq[:, hs].astype(BF16)
            qr = cq[:, nope_w + h * LANES:nope_w + (h + 1) * LANES]
            if latent:
                qr = _rope(qr, c_ref[pl.ds(q0, TQ), :], s_ref[pl.ds(q0, TQ), :], half)
            s = (_dot_nt(qn, kn_sc[:, hs]) + _dot_nt(qr.astype(BF16), kr_sc[...])) * scale
            m = jnp.max(s, axis=-1, keepdims=True)
            p = jnp.exp(s - m)
            l = jnp.sum(p, axis=-1, keepdims=True)
            o = _dot(p.astype(BF16), v_sc[:, hs]) / l
            o_ref[pl.ds(q0, TQ), hs] = o.astype(BF16)
        return carry

    lax.fori_loop(0, seq // TQ, qtile, 0)


def mla_branch(proj, row0, batch, seq, lp, layer, ctx=None):
    latent = ctx is not None
    rb = row0 // seq
    ctx_len = ctx[0].shape[2] if latent else 0
    const = lambda b: (0, 0)
    in_specs = [
        pl.BlockSpec((seq, MLA_Q_RANK), lambda b: (rb + b, C_MCQ // MLA_Q_RANK)),
        pl.BlockSpec((seq, MLA_KV_RANK), lambda b: (rb + b, C_MCKV // MLA_KV_RANK)),
        pl.BlockSpec((seq, LANES), lambda b: (rb + b, C_SMALL // LANES)),
    ]
    args = [proj, proj, proj]
    if latent:
        in_specs += [
            pl.BlockSpec((None, None, ctx_len, MLA_KV_RANK), lambda b: (b, layer, 0, 0)),
            pl.BlockSpec((None, None, ctx_len, LANES), lambda b: (b, layer, 0, 0)),
            pl.BlockSpec((seq, LANES), const),
            pl.BlockSpec((seq, LANES), const),
        ]
        args += [ctx[0], ctx[1], lp["rope_c32"], lp["rope_s32"]]
    in_specs += [
        pl.BlockSpec((1, MLA_Q_RANK), const),
        pl.BlockSpec((1, MLA_KV_RANK), const),
        pl.BlockSpec((MLA_Q_RANK, 2 * MLA_HEADS * LANES), const),
        pl.BlockSpec((MLA_KV_RANK, MLA_HEADS * MLA_NOPE), const),
        pl.BlockSpec((MLA_KV_RANK, BRANCH_W), const),
    ]
    args += [lp["mla_q_norm"], lp["mla_kv_norm"], lp["w_uq"], lp["w_uk"], lp["w_uv"]]
    out_specs = [pl.BlockSpec((seq, BRANCH_W), lambda b: (b, 0))]
    out_shape = [jax.ShapeDtypeStruct((batch * seq, BRANCH_W), BF16)]
    if not latent:
        out_specs.append(pl.BlockSpec((seq, MLA_KV_RANK), lambda b: (b, 0)))
        out_shape.append(jax.ShapeDtypeStruct((batch * seq, MLA_KV_RANK), F32))
    lk = ctx_len + seq
    res = pl.pallas_call(
        functools.partial(_mla_kernel, seq=seq, ctx_len=ctx_len, latent=latent),
        grid=(batch,),
        in_specs=in_specs,
        out_specs=out_specs,
        out_shape=out_shape,
        scratch_shapes=[pltpu.VMEM((lk, MLA_HEADS * MLA_NOPE), BF16), pltpu.VMEM((lk, BRANCH_W), BF16),
                        pltpu.VMEM((lk, LANES), BF16)],
        compiler_params=_cparams(("parallel",)),
        name="mla_latent" if latent else "mla_context",
    )(*args)
    return res if not latent else res[0]


def _merge_kernel(o0_ref, o1_ref, o2_ref, o3_ref, g0_ref, g1_ref, g2_ref, g3_ref, wb_ref, wo_ref, x_ref,
                  ga_ref, shf_ref, scf_ref, lg_ref, lb_ref, rw_ref, rb_ref, x1_ref, u2_ref, lo_ref, acc_sc):
    j = pl.program_id(1)

    @pl.when(j == 0)
    def _():
        acc_sc[...] = jnp.zeros(acc_sc.shape, F32)

    merged = None
    for n, (o_ref, g_ref) in enumerate(((o0_ref, g0_ref), (o1_ref, g1_ref), (o2_ref, g2_ref), (o3_ref, g3_ref))):
        term = _sigmoid(g_ref[...]) * _dot(o_ref[...], wb_ref[n])
        merged = term if merged is None else merged + term
    acc_sc[...] += _dot(merged.astype(BF16), wo_ref[...])

    @pl.when(j == pl.num_programs(1) - 1)
    def _():
        h = DEEPNORM_ALPHA * x_ref[...] + ga_ref[...] * acc_sc[...]
        x1 = _ln(h) * lg_ref[...] + lb_ref[...]
        x1_ref[...] = x1
        u2 = (_ln(x1) * (1.0 + scf_ref[...]) + shf_ref[...]).astype(BF16)
        u2_ref[...] = u2
        lo_ref[...] = _dot(u2, rw_ref[...]) + rb_ref[...]


def merge_and_route(outs, proj, x, mods, lp, layer, n_ctx_rows, lat_len):
    t = x.shape[0]
    tm, tn = TM_MERGE, TN_MERGE
    row = functools.partial(_mod_row, tm=tm, n_ctx_rows=n_ctx_rows, lat_len=lat_len)
    gate_spec = lambda n: pl.BlockSpec((tm, tn), lambda i, j: (i, (C_GATE + n * D_MODEL) // tn + j))
    vec = lambda k: pl.BlockSpec((None, 1, D_MODEL), lambda i, j: (row(i), 0, k))
    cvec = pl.BlockSpec((None, 1, D_MODEL), lambda i, j: (layer, 0, 0))
    in_specs = (
        [pl.BlockSpec((tm, BRANCH_W), lambda i, j: (i, 0))] * N_BRANCH
        + [gate_spec(n) for n in range(N_BRANCH)]
        + [
            pl.BlockSpec((None, N_BRANCH, BRANCH_W, tn), lambda i, j: (layer, 0, 0, j)),
            pl.BlockSpec((None, tn, D_MODEL), lambda i, j: (layer, j, 0)),
            pl.BlockSpec((tm, D_MODEL), lambda i, j: (i, 0)),
            vec(2), vec(3), vec(4), cvec, cvec,
            pl.BlockSpec((None, D_MODEL, LANES), lambda i, j: (layer, 0, 0)),
            pl.BlockSpec((None, 1, LANES), lambda i, j: (layer, 0, 0)),
        ]
    )
    return pl.pallas_call(
        _merge_kernel,
        grid=(t // tm, D_MODEL // tn),
        in_specs=in_specs,
        out_specs=[
            pl.BlockSpec((tm, D_MODEL), lambda i, j: (i, 0)),
            pl.BlockSpec((tm, D_MODEL), lambda i, j: (i, 0)),
            pl.BlockSpec((tm, LANES), lambda i, j: (i, 0)),
        ],
        out_shape=[
            jax.ShapeDtypeStruct((t, D_MODEL), F32),
            jax.ShapeDtypeStruct((t, D_MODEL), BF16),
            jax.ShapeDtypeStruct((t, LANES), F32),
        ],
        scratch_shapes=[pltpu.VMEM((tm, D_MODEL), F32)],
        compiler_params=_cparams(("parallel", "arbitrary")),
        name="merge_route",
    )(*outs, proj, proj, proj, proj, lp["w_branch"], lp["w_out"], x, mods, mods, mods,
      lp["ln_mix_g"], lp["ln_mix_b"], lp["router_w"], lp["router_b"])


def _moe_up_kernel(e_ref, src_ref, valid_ref, first_ref, x_ref, wg_ref, wl_ref, bg_ref, bl_ref, h_ref,
                   wg_sc, wl_sc):
    blk = pl.program_id(1)

    @pl.when(first_ref[blk] == 1)
    def _():
        wg_sc[...] = wg_ref[...].astype(BF16)
        wl_sc[...] = wl_ref[...].astype(BF16)

    @pl.when(valid_ref[blk] == 1)
    def _():
        x = x_ref[...]
        g = jnp.minimum(_dot(x, wg_sc[...]) + bg_ref[...], SWIGLU_LIMIT)
        lin = jnp.clip(_dot(x, wl_sc[...]) + bl_ref[...], -SWIGLU_LIMIT, SWIGLU_LIMIT)
        h_ref[...] = ((lin + 1.0) * (g * _sigmoid(SWIGLU_ALPHA * g))).astype(BF16)


def _moe_down_kernel(e_ref, src_ref, valid_ref, first_ref, h_ref, wd_ref, bd_ref, y_ref, wd_sc):
    blk = pl.program_id(1)

    @pl.when(first_ref[blk] == 1)
    def _():
        wd_sc[...] = wd_ref[...].astype(BF16)

    @pl.when(valid_ref[blk] == 1)
    def _():
        y_ref[...] = _dot(h_ref[...], wd_sc[...]) + bd_ref[...]


def moe_experts(xs, plan, w_gu, b_gu, w_down, b_down, layer):
    p = xs.shape[0]
    nb = p // TM_MOE
    tm, tf, tn = TM_MOE, TF_MOE, TN_DOWN
    nf = EXPERT_FF // tf
    hidden = pl.pallas_call(
        _moe_up_kernel,
        grid_spec=pltpu.PrefetchScalarGridSpec(
            num_scalar_prefetch=4,
            grid=(nf, nb),
            in_specs=[
                pl.BlockSpec((tm, D_MODEL), lambda f, b, e, s, v, fi: (s[b], 0)),
                pl.BlockSpec((None, None, D_MODEL, tf), lambda f, b, e, s, v, fi: (layer, e[b], 0, f)),
                pl.BlockSpec((None, None, D_MODEL, tf), lambda f, b, e, s, v, fi: (layer, e[b], 0, nf + f)),
                pl.BlockSpec((None, None, 1, tf), lambda f, b, e, s, v, fi: (layer, e[b], 0, f)),
                pl.BlockSpec((None, None, 1, tf), lambda f, b, e, s, v, fi: (layer, e[b], 0, nf + f)),
            ],
            out_specs=pl.BlockSpec((tm, tf), lambda f, b, e, s, v, fi: (s[b], f)),
            scratch_shapes=[pltpu.VMEM((D_MODEL, tf), BF16), pltpu.VMEM((D_MODEL, tf), BF16)],
        ),
        out_shape=jax.ShapeDtypeStruct((p, EXPERT_FF), BF16),
        compiler_params=_cparams(("arbitrary", "arbitrary")),
        name="moe_up",
    )(*plan, xs, w_gu, w_gu, b_gu, b_gu)
    return pl.pallas_call(
        _moe_down_kernel,
        grid_spec=pltpu.PrefetchScalarGridSpec(
            num_scalar_prefetch=4,
            grid=(D_MODEL // tn, nb),
            in_specs=[
                pl.BlockSpec((tm, EXPERT_FF), lambda n, b, e, s, v, fi: (s[b], 0)),
                pl.BlockSpec((None, None, EXPERT_FF, tn), lambda n, b, e, s, v, fi: (layer, e[b], 0, n)),
                pl.BlockSpec((None, None, 1, tn), lambda n, b, e, s, v, fi: (layer, e[b], 0, n)),
            ],
            out_specs=pl.BlockSpec((tm, tn), lambda n, b, e, s, v, fi: (s[b], n)),
            scratch_shapes=[pltpu.VMEM((EXPERT_FF, tn), BF16)],
        ),
        out_shape=jax.ShapeDtypeStruct((p, D_MODEL), F32),
        compiler_params=_cparams(("arbitrary", "arbitrary")),
        name="moe_down",
    )(*plan, hidden, w_down, b_down)


def route(logits):
    t = logits.shape[0]
    a = t * TOP_K
    top_v, top_i = lax.top_k(logits, TOP_K)
    gate = jax.nn.softmax(top_v, axis=-1)
    flat_e = top_i.reshape(-1)
    onehot = (flat_e[:, None] == jnp.arange(N_EXPERTS, dtype=jnp.int32)[None, :]).astype(jnp.int32)
    csum = jnp.cumsum(onehot, axis=0)
    rank = jnp.take_along_axis(csum, flat_e[:, None], axis=1)[:, 0] - 1
    counts = csum[-1]
    padded = (counts + TM_MOE - 1) // TM_MOE * TM_MOE
    ends = jnp.cumsum(padded)
    slot = (ends - padded)[flat_e] + rank
    nb = a // TM_MOE + N_EXPERTS
    flat_t = jnp.repeat(jnp.arange(t, dtype=jnp.int32), TOP_K)
    tok = jnp.zeros((nb * TM_MOE,), jnp.int32).at[slot].set(flat_t)
    blk = jnp.arange(nb, dtype=jnp.int32)
    n_used = (ends[-1] // TM_MOE).astype(jnp.int32)
    src = jnp.minimum(blk, n_used - 1)
    blk_e = jnp.minimum(jnp.searchsorted(ends, src * TM_MOE, side="right"), N_EXPERTS - 1).astype(jnp.int32)
    valid = (blk < n_used).astype(jnp.int32)
    first = jnp.where((blk == 0) | (blk_e != jnp.roll(blk_e, 1)), 1, 0).astype(jnp.int32)
    return gate, slot.reshape(t, TOP_K), tok, (blk_e, src, valid, first)


def _ffn_norm_kernel(x_ref, f_ref, gf_ref, lg_ref, lb_ref, o_ref):
    h = DEEPNORM_ALPHA * x_ref[...] + gf_ref[...] * f_ref[...]
    o_ref[...] = _ln(h) * lg_ref[...] + lb_ref[...]


def ffn_norm(x1, ffn, mods, lp, layer, n_ctx_rows, lat_len):
    t = x1.shape[0]
    tm = TM_LN
    row = functools.partial(_mod_row, tm=tm, n_ctx_rows=n_ctx_rows, lat_len=lat_len)
    cvec = pl.BlockSpec((None, 1, D_MODEL), lambda i: (layer, 0, 0))
    return pl.pallas_call(
        _ffn_norm_kernel,
        grid=(t // tm,),
        in_specs=[
            pl.BlockSpec((tm, D_MODEL), lambda i: (i, 0)),
            pl.BlockSpec((tm, D_MODEL), lambda i: (i, 0)),
            pl.BlockSpec((None, 1, D_MODEL), lambda i: (row(i), 0, 5)),
            cvec, cvec,
        ],
        out_specs=pl.BlockSpec((tm, D_MODEL), lambda i: (i, 0)),
        out_shape=jax.ShapeDtypeStruct((t, D_MODEL), F32),
        compiler_params=_cparams(("parallel",)),
        name="ffn_norm",
    )(x1, ffn, mods, lp["ln_ffn_g"], lp["ln_ffn_b"])


def _rope_tables(n_tokens, seg, width):
    rows = n_tokens // GRID_W
    row = jnp.repeat(jnp.arange(rows, dtype=F32), GRID_W)
    col = jnp.tile(jnp.arange(GRID_W, dtype=F32), rows)
    n_freq = seg // 4
    inv = ROPE_THETA ** (-jnp.arange(n_freq, dtype=F32) / n_freq)
    ang = jnp.concatenate([row[:, None] * inv, col[:, None] * inv], axis=-1)
    cos, sin = jnp.cos(ang), jnp.sin(ang)
    reps = width // seg
    return (jnp.tile(jnp.concatenate([cos, cos], axis=-1), (1, reps)),
            jnp.tile(jnp.concatenate([-sin, sin], axis=-1), (1, reps)))


def _pack_w_in(w_in):
    pad = jnp.zeros(w_in.shape[:-1] + (C_GATE - C_SMALL - MLA_ROPE - 2 * SSD_HEADS,), w_in.dtype)
    return jnp.concatenate([
        w_in[..., BRANCH_W:_O_DT], w_in[..., :BRANCH_W], w_in[..., _O_DQ:_O_MKR], w_in[..., _O_GQ:_O_GATE],
        w_in[..., _O_MKR:_O_GQ], w_in[..., _O_DT:_O_DQ], pad, w_in[..., _O_GATE:],
    ], axis=-1).astype(BF16)


def _pack_w_uq(w_uq):
    w = w_uq.reshape(MLA_Q_RANK, MLA_HEADS, MLA_NOPE + MLA_ROPE)
    nope = w[:, :, :MLA_NOPE].reshape(MLA_Q_RANK, MLA_HEADS * MLA_NOPE)
    rope = jnp.pad(w[:, :, MLA_NOPE:], ((0, 0), (0, 0), (0, LANES - MLA_ROPE)))
    return jnp.concatenate([nope, rope.reshape(MLA_Q_RANK, MLA_HEADS * LANES)], axis=-1).astype(BF16)


def _small_row(v16):
    return jnp.zeros((1, LANES), F32).at[0, SMALL_DT:SMALL_DT + 2 * SSD_HEADS].set(v16.reshape(-1))


def kernel(x_prompt, x_sample, state_ssd, cache_diff_k, cache_diff_v, cache_mla_ckv, cache_mla_kr, cache_gqa_k, cache_gqa_v, c, c_ctx, w_mod, b_mod, w_in, ssd_conv_w, ssd_conv_b, ssd_dt_bias, ssd_a_log, ssd_d, ssd_norm, diff_lambda, diff_norm, mla_q_norm, mla_w_uq, mla_kv_norm, mla_w_uk, mla_w_uv, gqa_q_norm, gqa_k_norm, w_branch, w_out, ln_mix_g, ln_mix_b, router_w, router_b, exp_w_gu, exp_b_gu, exp_w_down, exp_b_down, ln_ffn_g, ln_ffn_b):
    nb_ctx, len_ctx, _ = x_prompt.shape
    nb_lat, len_lat, _ = x_sample.shape
    past = cache_diff_k.shape[2]
    n_ctx_rows = nb_ctx * len_ctx
    t = n_ctx_rows + nb_lat * len_lat

    cvec = jnp.zeros((SUBLANES, D_MODEL), F32).at[0].set(c_ctx).at[1:1 + nb_lat].set(c)
    mods_all = modulation(cvec, w_mod, b_mod)

    w_in_p = _pack_w_in(w_in)
    w_branch_b = w_branch.astype(BF16)
    w_out_b = w_out.astype(BF16)
    router_w_p = jnp.pad(router_w, ((0, 0), (0, 0), (0, LANES - N_EXPERTS))).astype(BF16)
    router_b_p = jnp.pad(router_b, ((0, 0), (0, LANES - N_EXPERTS))).reshape(DEPTH, 1, LANES)
    c32, s32 = _rope_tables(len_lat, 2 * (DIFF_D // 2), LANES)
    c64, s64 = _rope_tables(len_lat, GQA_HEAD_DIM, LANES)
    n_pair = SSD_HEADS // 2
    state_in = state_ssd.reshape(nb_lat, DEPTH, 2, n_pair, 2 * SSD_HEAD_DIM, SSD_STATE)
    diff_kc = cache_diff_k.reshape(nb_lat, DEPTH, past, BRANCH_W)
    diff_vc = cache_diff_v.reshape(nb_lat, DEPTH, past, BRANCH_W)
    mla_krc = jnp.pad(cache_mla_kr, ((0, 0), (0, 0), (0, 0), (0, LANES - MLA_ROPE)))
    gqa_kc = cache_gqa_k.reshape(nb_lat, DEPTH, past, GQA_KV_HEADS * GQA_HEAD_DIM)
    gqa_vc = cache_gqa_v.reshape(nb_lat, DEPTH, past, GQA_KV_HEADS * GQA_HEAD_DIM)
    vecs = dict(ln_mix_g=ln_mix_g.reshape(DEPTH, 1, D_MODEL), ln_mix_b=ln_mix_b.reshape(DEPTH, 1, D_MODEL),
                ln_ffn_g=ln_ffn_g.reshape(DEPTH, 1, D_MODEL), ln_ffn_b=ln_ffn_b.reshape(DEPTH, 1, D_MODEL),
                w_branch=w_branch_b, w_out=w_out_b, router_w=router_w_p, router_b=router_b_p)
    b_gu = exp_b_gu.reshape(DEPTH, N_EXPERTS, 1, 2 * EXPERT_FF)
    b_down = exp_b_down.reshape(DEPTH, N_EXPERTS, 1, D_MODEL)

    x = jnp.concatenate([x_prompt.reshape(n_ctx_rows, D_MODEL), x_sample.reshape(-1, D_MODEL)], axis=0)
    new_state, new_dk, new_dv, new_ckv, new_kr, new_gk, new_gv = [], [], [], [], [], [], []
    for l in range(DEPTH):
        lp = dict(vecs)
        lp.update(
            conv_w=jnp.pad(ssd_conv_w[l], ((0, SUBLANES - SSD_CONV), (0, 0))),
            conv_b=ssd_conv_b[l].reshape(1, SSD_XBC),
            dt_bias=_small_row(ssd_dt_bias[l]),
            a_row=_small_row(-jnp.exp(ssd_a_log[l])),
            d_row=jnp.repeat(ssd_d[l], SSD_HEAD_DIM).reshape(1, BRANCH_W),
            ssd_norm=ssd_norm[l].reshape(1, BRANCH_W),
            diff_lambda=diff_lambda[l],
            diff_norm=diff_norm[l].reshape(1, 2 * DIFF_D),
            mla_q_norm=mla_q_norm[l].reshape(1, MLA_Q_RANK),
            mla_kv_norm=mla_kv_norm[l].reshape(1, MLA_KV_RANK),
            w_uq=_pack_w_uq(mla_w_uq[l]),
            w_uk=mla_w_uk[l].astype(BF16),
            w_uv=mla_w_uv[l].astype(BF16),
            gqa_q_norm=gqa_q_norm[l].reshape(1, GQA_HEAD_DIM),
            gqa_k_norm=gqa_k_norm[l].reshape(1, GQA_HEAD_DIM),
            rope_c32=c32, rope_s32=s32, rope_c64=c64, rope_s64=s64,
        )
        mods = mods_all[l].reshape(SUBLANES, 1, 6 * D_MODEL)
        proj = in_projection(x, mods, w_in_p, l, n_ctx_rows, len_lat)

        o_ssd_c, st = ssd_branch(proj, 0, nb_ctx, len_ctx, lp, l)
        o_ssd_l = ssd_branch(proj, n_ctx_rows, nb_lat, len_lat, lp, l, state_in)
        o_diff_c = diff_branch(proj, 0, nb_ctx, len_ctx, lp, l)
        o_diff_l = diff_branch(proj, n_ctx_rows, nb_lat, len_lat, lp, l, (diff_kc, diff_vc))
        o_mla_c, ckv_c = mla_branch(proj, 0, nb_ctx, len_ctx, lp, l)
        o_mla_l = mla_branch(proj, n_ctx_rows, nb_lat, len_lat, lp, l, (cache_mla_ckv, mla_krc))
        o_gqa_c, gk_c = gqa_branch(proj, 0, nb_ctx, len_ctx, lp, l)
        o_gqa_l = gqa_branch(proj, n_ctx_rows, nb_lat, len_lat, lp, l, (gqa_kc, gqa_vc))
        outs = [jnp.concatenate(pair, axis=0) for pair in
                ((o_ssd_c, o_ssd_l), (o_diff_c, o_diff_l), (o_mla_c, o_mla_l), (o_gqa_c, o_gqa_l))]

        ctx_proj = proj[:n_ctx_rows].reshape(nb_ctx, len_ctx, PROJ_W)
        new_state.append(st.reshape(nb_ctx, 2, SSD_HEADS, SSD_HEAD_DIM, SSD_STATE))
        new_dk.append(ctx_proj[..., C_DK:C_DK + BRANCH_W].reshape(nb_ctx, len_ctx, DIFF_HEADS, 2 * DIFF_D))
        new_dv.append(ctx_proj[..., C_DV:C_DV + BRANCH_W].reshape(nb_ctx, len_ctx, DIFF_HEADS, 2 * DIFF_D))
        new_ckv.append(ckv_c.reshape(nb_ctx, len_ctx, MLA_KV_RANK))
        new_kr.append(ctx_proj[..., C_SMALL:C_SMALL + MLA_ROPE])
        new_gk.append(gk_c.reshape(nb_ctx, len_ctx, GQA_KV_HEADS, GQA_HEAD_DIM))
        new_gv.append(ctx_proj[..., C_GV:C_GV + GQA_KV_HEADS * GQA_HEAD_DIM]
                      .reshape(nb_ctx, len_ctx, GQA_KV_HEADS, GQA_HEAD_DIM))

        x1, u2, logits = merge_and_route(outs, proj, x, mods, lp, l, n_ctx_rows, len_lat)
        gate, slot, tok, plan = route(logits[:, :N_EXPERTS])
        yb = moe_experts(u2[tok], plan, exp_w_gu, b_gu, exp_w_down, b_down, l)
        ffn = jnp.sum(yb[slot] * gate[..., None], axis=1)
        x = ffn_norm(x1, ffn, mods, lp, l, n_ctx_rows, len_lat)

    y_prompt = x[:n_ctx_rows].reshape(nb_ctx, len_ctx, D_MODEL)
    y_sample = x[n_ctx_rows:].reshape(nb_lat, len_lat, D_MODEL)
    stack = lambda parts: jnp.stack(parts, axis=1)
    return (y_prompt, y_sample, stack(new_state), stack(new_dk), stack(new_dv), stack(new_ckv),
            stack(new_kr), stack(new_gk), stack(new_gv))
```

```python
import functools
import math

import jax
import jax.numpy as jnp
from jax import lax
from jax.experimental import pallas as pl
from jax.experimental.pallas import tpu as pltpu

F32 = jnp.float32
BF16 = jnp.bfloat16

D_MODEL = 2048
DEPTH = 4
GRID_W = 64
ROPE_THETA = 10000.0
BRANCH_W = 512
N_BRANCH = 4
SSD_HEADS = 8
SSD_HEAD_DIM = 64
SSD_STATE = 128
SSD_CONV = 5
SSD_CHUNK = 128
SSD_XBC = 1024
DIFF_HEADS = 4
DIFF_D = 64
MLA_HEADS = 4
MLA_NOPE = 128
MLA_ROPE = 64
MLA_Q_RANK = 512
MLA_KV_RANK = 256
GQA_HEAD_DIM = 128
GQA_HEADS = 4
GQA_KV_HEADS = 2
N_EXPERTS = 32
TOP_K = 4
EXPERT_FF = 1024
SWIGLU_LIMIT = 7.0
SWIGLU_ALPHA = 1.702
DEEPNORM_ALPHA = (2.0 * DEPTH) ** 0.25
EPS = 1e-6

LANES = 128
SUBLANES = 8

_O_DT = 1536
_O_DQ = 1552
_O_MKR = 3856
_O_GQ = 3920
_O_GATE = 4944
IN_WIDTH = 13136
C_XBC = 0
C_Z = 1024
C_DQ = 1536
C_DK = 2048
C_DV = 2560
C_MCQ = 3072
C_MCKV = 3584
C_GQ = 3840
C_GK = 4352
C_GV = 4608
C_SMALL = 4864
C_GATE = 5120
PROJ_W = C_GATE + N_BRANCH * D_MODEL
SMALL_DT = MLA_ROPE

VMEM_LIMIT = 60000 * 1024

TM_IN = 1024
TN_IN = 1024
TM_MERGE = 512
TN_MERGE = 512
TM_LN = 256
TM_MOE = 512
TF_MOE = 512
TN_DOWN = 1024
TN_MOD = 1024
TQ = 256


def _cparams(sem):
    return pltpu.CompilerParams(dimension_semantics=sem, vmem_limit_bytes=VMEM_LIMIT)


def _dot(a, b):
    return jnp.dot(a, b, preferred_element_type=F32)


def _dot_nt(a, b):
    return lax.dot_general(a, b, (((1,), (1,)), ((), ())), preferred_element_type=F32)


def _ln(x):
    mu = jnp.mean(x, axis=-1, keepdims=True)
    xc = x - mu
    var = jnp.mean(xc * xc, axis=-1, keepdims=True)
    return xc * lax.rsqrt(var + EPS)


def _rms(x, g):
    return x * lax.rsqrt(jnp.mean(x * x, axis=-1, keepdims=True) + EPS) * g


def _sigmoid(x):
    return 1.0 / (1.0 + jnp.exp(-x))


def _silu(x):
    return x * _sigmoid(x)


def _rope(x, c, s, half):
    w = x.shape[-1]
    lane = lax.broadcasted_iota(jnp.int32, x.shape, 1)
    lo = (lane & (2 * half - 1)) < half
    swapped = jnp.where(lo, pltpu.roll(x, w - half, 1), pltpu.roll(x, half, 1))
    return x * c + swapped * s


def _softmax_pv(q, k, v, scale):
    s = _dot_nt(q, k) * scale
    m = jnp.max(s, axis=-1, keepdims=True)
    p = jnp.exp(s - m)
    l = jnp.sum(p, axis=-1, keepdims=True)
    return _dot(p.astype(BF16), v) / l


def _mod_row(i, tm, n_ctx_rows, lat_len):
    r = i * tm
    return jnp.where(r < n_ctx_rows, 0, 1 + (r - n_ctx_rows) // lat_len)


def _mod_kernel(c_ref, w_ref, b_ref, o_ref):
    c = c_ref[...]
    o_ref[...] = _dot(_silu(c).astype(BF16), w_ref[...].astype(BF16)) + b_ref[...]


def modulation(cvec, w_mod, b_mod):
    n = w_mod.shape[-1]
    return pl.pallas_call(
        _mod_kernel,
        grid=(DEPTH, n // TN_MOD),
        in_specs=[
            pl.BlockSpec((SUBLANES, D_MODEL), lambda l, j: (0, 0)),
            pl.BlockSpec((None, D_MODEL, TN_MOD), lambda l, j: (l, 0, j)),
            pl.BlockSpec((None, 1, TN_MOD), lambda l, j: (l, 0, j)),
        ],
        out_specs=pl.BlockSpec((None, SUBLANES, TN_MOD), lambda l, j: (l, 0, j)),
        out_shape=jax.ShapeDtypeStruct((DEPTH, SUBLANES, n), F32),
        compiler_params=_cparams(("parallel", "parallel")),
        name="modulation",
    )(cvec, w_mod, b_mod.reshape(DEPTH, 1, n))


def _inproj_kernel(x_ref, sh_ref, sc_ref, w_ref, o_ref, u_sc):
    @pl.when(pl.program_id(1) == 0)
    def _():
        u = _ln(x_ref[...]) * (1.0 + sc_ref[...]) + sh_ref[...]
        u_sc[...] = u.astype(BF16)

    o_ref[...] = _dot(u_sc[...], w_ref[...])


def in_projection(x, mods, w_in_p, layer, n_ctx_rows, lat_len):
    t = x.shape[0]
    tm = TM_IN
    row = functools.partial(_mod_row, tm=tm, n_ctx_rows=n_ctx_rows, lat_len=lat_len)
    return pl.pallas_call(
        _inproj_kernel,
        grid=(t // tm, PROJ_W // TN_IN),
        in_specs=[
            pl.BlockSpec((tm, D_MODEL), lambda i, j: (i, 0)),
            pl.BlockSpec((None, 1, D_MODEL), lambda i, j: (row(i), 0, 0)),
            pl.BlockSpec((None, 1, D_MODEL), lambda i, j: (row(i), 0, 1)),
            pl.BlockSpec((None, D_MODEL, TN_IN), lambda i, j: (layer, 0, j)),
        ],
        out_specs=pl.BlockSpec((tm, TN_IN), lambda i, j: (i, j)),
        out_shape=jax.ShapeDtypeStruct((t, PROJ_W), F32),
        scratch_shapes=[pltpu.VMEM((tm, D_MODEL), BF16)],
        compiler_params=_cparams(("parallel", "arbitrary")),
        name="in_projection",
    )(x, mods, mods, w_in_p)


def _lane_col(a, lane):
    idx = lax.broadcasted_iota(jnp.int32, a.shape, 1)
    return jnp.sum(jnp.where(idx == lane, a, 0.0), axis=1, keepdims=True)


def _ssd_kernel(*refs, seq, latent):
    if latent:
        (z_ref, xbc_ref, small_ref, cw_ref, cb_ref, dtb_ref, arow_ref, drow_ref, nw_ref, h0_ref,
         o_ref, xpad_sc, xs_sc, y_sc, dt_sc, h_sc) = refs
    else:
        (z_ref, xbc_ref, small_ref, cw_ref, cb_ref, dtb_ref, arow_ref, drow_ref, nw_ref,
         o_ref, hout_ref, xpad_sc, xs_sc, y_sc, dt_sc, h_sc) = refs
    q = SSD_CHUNK
    nc = seq // q
    pad = SUBLANES
    ext = q + 2 * pad

    xpad_sc[0:pad, :] = jnp.zeros((pad, SSD_XBC), F32)
    xpad_sc[pad + seq:2 * pad + seq, :] = jnp.zeros((pad, SSD_XBC), F32)

    def copy_in(c, carry):
        r0 = pl.multiple_of(c * q, q)
        xpad_sc[pl.ds(pad + r0, q), :] = xbc_ref[pl.ds(r0, q), :]
        return carry

    lax.fori_loop(0, nc, copy_in, 0)

    def conv(c, carry):
        r0 = pl.multiple_of(c * q, q)
        for ct in range(SSD_XBC // LANES):
            cols = slice(ct * LANES, (ct + 1) * LANES)
            win = xpad_sc[pl.ds(r0, ext), cols]
            acc = cb_ref[:, cols] + win[pad:pad + q] * cw_ref[2:3, cols]
            for k in (0, 1, 3, 4):
                shifted = pltpu.roll(win, (2 - k) % ext, 0)[pad:pad + q]
                acc = acc + shifted * cw_ref[k:k + 1, cols]
            xs_sc[pl.ds(r0, q), cols] = _silu(acc)
        sd = small_ref[pl.ds(r0, q), :] + dtb_ref[...]
        dt_sc[pl.ds(r0, q), :] = jnp.maximum(sd, 0.0) + jnp.log(1.0 + jnp.exp(-jnp.abs(sd)))
        return carry

    lax.fori_loop(0, nc, conv, 0)

    if latent:
        h_sc[...] = h0_ref[...]
    else:
        h_sc[...] = jnp.zeros(h_sc.shape, F32)

    ri = lax.broadcasted_iota(jnp.int32, (q, q), 0)
    ci = lax.broadcasted_iota(jnp.int32, (q, q), 1)
    lane = lax.broadcasted_iota(jnp.int32, (q, LANES), 1)
    lane_lo = lane < SSD_HEAD_DIM
    row_lo = lax.broadcasted_iota(jnp.int32, (2 * SSD_HEAD_DIM, q), 0) < SSD_HEAD_DIM
    row_lo_n = lax.broadcasted_iota(jnp.int32, (2 * SSD_HEAD_DIM, SSD_STATE), 0) < SSD_HEAD_DIM

    def chunk(c, d):
        keep = (ci <= ri) if d == 0 else (ci >= ri)
        tri = jnp.where(keep, 1.0, 0.0)
        r0 = pl.multiple_of(c * q, q)
        xc = xs_sc[pl.ds(r0, q), 0:BRANCH_W]
        dtc = dt_sc[pl.ds(r0, q), :]
        ac = dtc * arow_ref[...]
        acum = jnp.dot(tri, ac, preferred_element_type=F32, precision=lax.Precision.HIGHEST)
        acum_t = acum.T
        dt_t = dtc.T
        tot = acum[q - 1:q, :] if d == 0 else acum[0:1, :]
        eac = jnp.exp(acum)
        w_t = (jnp.exp(tot - acum) * dtc).T
        etot = jnp.exp(tot)
        x_t = xc.T
        for g in range(2):
            bg = xs_sc[pl.ds(r0, q), BRANCH_W + g * SSD_STATE:BRANCH_W + (g + 1) * SSD_STATE].astype(BF16)
            cg = xs_sc[pl.ds(r0, q), BRANCH_W + 2 * SSD_STATE + g * SSD_STATE:
                       BRANCH_W + 2 * SSD_STATE + (g + 1) * SSD_STATE].astype(BF16)
            cb = _dot_nt(cg, bg)
            for pr in range(2):
                pair = g * 2 + pr
                h0 = pair * 2
                l0 = SMALL_DT + d * SSD_HEADS + h0
                atts = []
                for hh in range(2):
                    ln = l0 + hh
                    seg = _lane_col(acum, ln) - acum_t[ln:ln + 1, :]
                    dec = jnp.exp(jnp.where(keep, seg, -1e30))
                    atts.append(cb * dec * dt_t[ln:ln + 1, :])
                att = jnp.concatenate(atts, axis=1).astype(BF16)
                cols = slice(h0 * SSD_HEAD_DIM, (h0 + 2) * SSD_HEAD_DIM)
                xp = xc[:, cols]
                x2 = jnp.concatenate([jnp.where(lane_lo, xp, 0.0), jnp.where(lane_lo, 0.0, xp)],
                                     axis=0).astype(BF16)
                y_diag = _dot(att, x2)
                st_in = h_sc[d, pair]
                y_off = _dot_nt(cg, st_in.astype(BF16))
                esel = jnp.where(lane_lo, _lane_col(eac, l0), _lane_col(eac, l0 + 1))
                y = y_diag + y_off * esel
                if d == 0:
                    y_sc[pl.ds(r0, q), cols] = y
                else:
                    y_sc[pl.ds(r0, q), cols] = y_sc[pl.ds(r0, q), cols] + y
                wsel = jnp.where(row_lo, w_t[l0:l0 + 1, :], w_t[l0 + 1:l0 + 2, :])
                st_new = _dot((x_t[cols, :] * wsel).astype(BF16), bg)
                dsel = jnp.where(row_lo_n, _lane_col(etot, l0), _lane_col(etot, l0 + 1))
                h_sc[d, pair] = dsel * st_in + st_new

    def fwd(c, carry):
        chunk(c, 0)
        return carry

    def bwd(i, carry):
        chunk(nc - 1 - i, 1)
        return carry

    lax.fori_loop(0, nc, fwd, 0)
    lax.fori_loop(0, nc, bwd, 0)

    def finish(c, carry):
        r0 = pl.multiple_of(c * q, q)
        y = y_sc[pl.ds(r0, q), :] + drow_ref[...] * xs_sc[pl.ds(r0, q), 0:BRANCH_W]
        gated = y * _silu(z_ref[pl.ds(r0, q), :])
        o_ref[pl.ds(r0, q), :] = _rms(gated, nw_ref[...]).astype(BF16)
        return carry

    lax.fori_loop(0, nc, finish, 0)
    if not latent:
        hout_ref[...] = h_sc[...]


def ssd_branch(proj, row0, batch, seq, lp, layer, state=None):
    latent = state is not None
    rb = row0 // seq
    n_pair = SSD_HEADS // 2
    st_shape = (2, n_pair, 2 * SSD_HEAD_DIM, SSD_STATE)
    const = lambda b: (0, 0)
    in_specs = [
        pl.BlockSpec((seq, BRANCH_W), lambda b: (rb + b, C_Z // BRANCH_W)),
        pl.BlockSpec((seq, SSD_XBC), lambda b: (rb + b, C_XBC // SSD_XBC)),
        pl.BlockSpec((seq, LANES), lambda b: (rb + b, C_SMALL // LANES)),
        pl.BlockSpec((SUBLANES, SSD_XBC), const),
        pl.BlockSpec((1, SSD_XBC), const),
        pl.BlockSpec((1, LANES), const),
        pl.BlockSpec((1, LANES), const),
        pl.BlockSpec((1, BRANCH_W), const),
        pl.BlockSpec((1, BRANCH_W), const),
    ]
    args = [proj, proj, proj, lp["conv_w"], lp["conv_b"], lp["dt_bias"], lp["a_row"], lp["d_row"],
            lp["ssd_norm"]]
    out_specs = [pl.BlockSpec((seq, BRANCH_W), lambda b: (b, 0))]
    out_shape = [jax.ShapeDtypeStruct((batch * seq, BRANCH_W), BF16)]
    if latent:
        in_specs.append(pl.BlockSpec((None, None) + st_shape, lambda b: (b, layer, 0, 0, 0, 0)))
        args.append(state)
    else:
        out_specs.append(pl.BlockSpec((None,) + st_shape, lambda b: (b, 0, 0, 0, 0)))
        out_shape.append(jax.ShapeDtypeStruct((batch,) + st_shape, F32))
    res = pl.pallas_call(
        functools.partial(_ssd_kernel, seq=seq, latent=latent),
        grid=(batch,),
        in_specs=in_specs,
        out_specs=out_specs,
        out_shape=out_shape,
        scratch_shapes=[
            pltpu.VMEM((seq + 2 * SUBLANES, SSD_XBC), F32),
            pltpu.VMEM((seq, SSD_XBC), F32),
            pltpu.VMEM((seq, BRANCH_W), F32),
            pltpu.VMEM((seq, LANES), F32),
            pltpu.VMEM(st_shape, F32),
        ],
        compiler_params=_cparams(("parallel",)),
        name="ssd_latent" if latent else "ssd_context",
    )(*args)
    return res if not latent else res[0]


def _diff_kernel(*refs, seq, ctx_len, latent, lam_init):
    if latent:
        (q_ref, k_ref, v_ref, kc_ref, vc_ref, c_ref, s_ref, lam_ref, nw_ref, o_ref, k_sc, v_sc) = refs
    else:
        (q_ref, k_ref, v_ref, lam_ref, nw_ref, o_ref, kout_ref, vout_ref, k_sc, v_sc) = refs
    lk = ctx_len + seq
    half = DIFF_D // 2
    if latent:
        k_sc[0:ctx_len, :] = kc_ref[...].astype(BF16)
        v_sc[0:ctx_len, :] = vc_ref[...].astype(BF16)
        kn = _rope(k_ref[...], c_ref[...], s_ref[...], half)
    else:
        kn = k_ref[...]
        kout_ref[...] = kn
        vout_ref[...] = v_ref[...]
    k_sc[ctx_len:lk, :] = kn.astype(BF16)
    v_sc[ctx_len:lk, :] = v_ref[...].astype(BF16)
    lv = lam_ref[...]
    lam = (jnp.exp(jnp.sum(lv[0:1] * lv[1:2], axis=1, keepdims=True))
           - jnp.exp(jnp.sum(lv[2:3] * lv[3:4], axis=1, keepdims=True)) + lam_init)
    scale = DIFF_D ** -0.5
    lane_lo = lax.broadcasted_iota(jnp.int32, (TQ, LANES), 1) < DIFF_D

    def qtile(t, carry):
        q0 = pl.multiple_of(t * TQ, TQ)
        qv = q_ref[pl.ds(q0, TQ), :]
        if latent:
            qv = _rope(qv, c_ref[pl.ds(q0, TQ), :], s_ref[pl.ds(q0, TQ), :], half)
        q1 = jnp.where(lane_lo, qv, 0.0).astype(BF16)
        q2 = jnp.where(lane_lo, 0.0, qv).astype(BF16)
        k = k_sc[...]
        v = v_sc[...]
        o = _softmax_pv(q1, k, v, scale) - lam * _softmax_pv(q2, k, v, scale)
        o_ref[pl.ds(q0, TQ), :] = (_rms(o, nw_ref[...]) * (1.0 - lam_init)).astype(BF16)
        return carry

    lax.fori_loop(0, seq // TQ, qtile, 0)


def diff_branch(proj, row0, batch, seq, lp, layer, ctx=None):
    latent = ctx is not None
    rb = row0 // seq
    ctx_len = ctx[0].shape[2] if latent else 0
    lam_init = 0.8 - 0.6 * math.exp(-0.3 * layer)
    wid = 2 * DIFF_D
    in_specs = [
        pl.BlockSpec((seq, wid), lambda b, h: (rb + b, C_DQ // wid + h)),
        pl.BlockSpec((seq, wid), lambda b, h: (rb + b, C_DK // wid + h)),
        pl.BlockSpec((seq, wid), lambda b, h: (rb + b, C_DV // wid + h)),
    ]
    args = [proj, proj, proj]
    if latent:
        in_specs += [
            pl.BlockSpec((None, None, ctx_len, wid), lambda b, h: (b, layer, 0, h)),
            pl.BlockSpec((None, None, ctx_len, wid), lambda b, h: (b, layer, 0, h)),
            pl.BlockSpec((seq, wid), lambda b, h: (0, 0)),
            pl.BlockSpec((seq, wid), lambda b, h: (0, 0)),
        ]
        args += [ctx[0], ctx[1], lp["rope_c32"], lp["rope_s32"]]
    in_specs += [
        pl.BlockSpec((4, DIFF_D), lambda b, h: (0, 0)),
        pl.BlockSpec((1, wid), lambda b, h: (0, 0)),
    ]
    args += [lp["diff_lambda"], lp["diff_norm"]]
    out_specs = [pl.BlockSpec((seq, wid), lambda b, h: (b, h))]
    out_shape = [jax.ShapeDtypeStruct((batch * seq, BRANCH_W), BF16)]
    if not latent:
        out_specs += [pl.BlockSpec((seq, wid), lambda b, h: (b, h))] * 2
        out_shape += [jax.ShapeDtypeStruct((batch * seq, BRANCH_W), F32)] * 2
    res = pl.pallas_call(
        functools.partial(_diff_kernel, seq=seq, ctx_len=ctx_len, latent=latent, lam_init=lam_init),
        grid=(batch, DIFF_HEADS),
        in_specs=in_specs,
        out_specs=out_specs,
        out_shape=out_shape,
        scratch_shapes=[pltpu.VMEM((ctx_len + seq, wid), BF16), pltpu.VMEM((ctx_len + seq, wid), BF16)],
        compiler_params=_cparams(("parallel", "parallel")),
        name="diff_latent" if latent else "diff_context",
    )(*args)
    return res if not latent else res[0]


def _gqa_kernel(*refs, seq, ctx_len, latent):
    if latent:
        (q_ref, k_ref, v_ref, kc_ref, vc_ref, c_ref, s_ref, qw_ref, kw_ref, o_ref, k_sc, v_sc) = refs
    else:
        (q_ref, k_ref, v_ref, qw_ref, kw_ref, o_ref, kout_ref, vout_ref, k_sc, v_sc) = refs
    lk = ctx_len + seq
    half = GQA_HEAD_DIM // 2
    hd = GQA_HEAD_DIM
    kn = _rms(k_ref[...], kw_ref[...])
    if latent:
        k_sc[0:ctx_len, :] = kc_ref[...].astype(BF16)
        v_sc[0:ctx_len, :] = vc_ref[...].astype(BF16)
        kn = _rope(kn, c_ref[...], s_ref[...], half)
    else:
        kout_ref[...] = kn
        vout_ref[...] = v_ref[...]
    k_sc[ctx_len:lk, :] = kn.astype(BF16)
    v_sc[ctx_len:lk, :] = v_ref[...].astype(BF16)
    scale = GQA_HEAD_DIM ** -0.5

    def qtile(t, carry):
        q0 = pl.multiple_of(t * TQ, TQ)
        qs = []
        for g in range(GQA_HEADS // GQA_KV_HEADS):
            qv = _rms(q_ref[pl.ds(q0, TQ), g * hd:(g + 1) * hd], qw_ref[...])
            if latent:
                qv = _rope(qv, c_ref[pl.ds(q0, TQ), :], s_ref[pl.ds(q0, TQ), :], half)
            qs.append(qv)
        qq = jnp.concatenate(qs, axis=0).astype(BF16)
        o = _softmax_pv(qq, k_sc[...], v_sc[...], scale)
        for g in range(GQA_HEADS // GQA_KV_HEADS):
            o_ref[pl.ds(q0, TQ), g * hd:(g + 1) * hd] = o[g * TQ:(g + 1) * TQ].astype(BF16)
        return carry

    lax.fori_loop(0, seq // TQ, qtile, 0)


def gqa_branch(proj, row0, batch, seq, lp, layer, ctx=None):
    latent = ctx is not None
    rb = row0 // seq
    ctx_len = ctx[0].shape[2] if latent else 0
    hd = GQA_HEAD_DIM
    grp = GQA_HEADS // GQA_KV_HEADS
    in_specs = [
        pl.BlockSpec((seq, grp * hd), lambda b, h: (rb + b, C_GQ // (grp * hd) + h)),
        pl.BlockSpec((seq, hd), lambda b, h: (rb + b, C_GK // hd + h)),
        pl.BlockSpec((seq, hd), lambda b, h: (rb + b, C_GV // hd + h)),
    ]
    args = [proj, proj, proj]
    if latent:
        in_specs += [
            pl.BlockSpec((None, None, ctx_len, hd), lambda b, h: (b, layer, 0, h)),
            pl.BlockSpec((None, None, ctx_len, hd), lambda b, h: (b, layer, 0, h)),
            pl.BlockSpec((seq, hd), lambda b, h: (0, 0)),
            pl.BlockSpec((seq, hd), lambda b, h: (0, 0)),
        ]
        args += [ctx[0], ctx[1], lp["rope_c64"], lp["rope_s64"]]
    in_specs += [pl.BlockSpec((1, hd), lambda b, h: (0, 0)), pl.BlockSpec((1, hd), lambda b, h: (0, 0))]
    args += [lp["gqa_q_norm"], lp["gqa_k_norm"]]
    out_specs = [pl.BlockSpec((seq, grp * hd), lambda b, h: (b, h))]
    out_shape = [jax.ShapeDtypeStruct((batch * seq, BRANCH_W), BF16)]
    if not latent:
        out_specs += [pl.BlockSpec((seq, hd), lambda b, h: (b, h))] * 2
        out_shape += [jax.ShapeDtypeStruct((batch * seq, GQA_KV_HEADS * hd), F32)] * 2
    res = pl.pallas_call(
        functools.partial(_gqa_kernel, seq=seq, ctx_len=ctx_len, latent=latent),
        grid=(batch, GQA_KV_HEADS),
        in_specs=in_specs,
        out_specs=out_specs,
        out_shape=out_shape,
        scratch_shapes=[pltpu.VMEM((ctx_len + seq, hd), BF16), pltpu.VMEM((ctx_len + seq, hd), BF16)],
        compiler_params=_cparams(("parallel", "parallel")),
        name="gqa_latent" if latent else "gqa_context",
    )(*args)
    return res if not latent else res[0]


def _mla_kernel(*refs, seq, ctx_len, latent):
    if latent:
        (cq_ref, ckv_ref, small_ref, ckvc_ref, krc_ref, c_ref, s_ref, qw_ref, kvw_ref, wuq_ref, wuk_ref,
         wuv_ref, o_ref, kn_sc, v_sc, kr_sc) = refs
    else:
        (cq_ref, ckv_ref, small_ref, qw_ref, kvw_ref, wuq_ref, wuk_ref, wuv_ref,
         o_ref, ckv_out_ref, kr_out_ref, kn_sc, v_sc, kr_sc) = refs
    lk = ctx_len + seq
    half = MLA_ROPE // 2
    ckv = _rms(ckv_ref[...], kvw_ref[...])
    if latent:
        cc = ckvc_ref[...].astype(BF16)
        kn_sc[0:ctx_len, :] = _dot(cc, wuk_ref[...]).astype(BF16)
        v_sc[0:ctx_len, :] = _dot(cc, wuv_ref[...]).astype(BF16)
        kr_sc[0:ctx_len, :] = krc_ref[...].astype(BF16)
    else:
        ckv_out_ref[...] = ckv
        kr_out_ref[...] = small_ref[:, 0:MLA_ROPE]
    cb = ckv.astype(BF16)
    kn_sc[ctx_len:lk, :] = _dot(cb, wuk_ref[...]).astype(BF16)
    v_sc[ctx_len:lk, :] = _dot(cb, wuv_ref[...]).astype(BF16)
    kr = small_ref[...]
    if latent:
        kr = _rope(kr, c_ref[...], s_ref[...], half)
    lane_lo = lax.broadcasted_iota(jnp.int32, (seq, LANES), 1) < MLA_ROPE
    kr_sc[ctx_len:lk, :] = jnp.where(lane_lo, kr, 0.0).astype(BF16)
    scale = (MLA_NOPE + MLA_ROPE) ** -0.5
    nope_w = MLA_HEADS * MLA_NOPE

    def qtile(t, carry):
        q0 = pl.multiple_of(t * TQ, TQ)
        cq = _dot(_rms(cq_ref[pl.ds(q0, TQ), :], qw_ref[...]).astype(BF16), wuq_ref[...])
        for h in range(MLA_HEADS):
            hs = slice(h * MLA_NOPE, (h + 1) * MLA_NOPE)
            qn = cq[:, hs].astype(BF16)
            qr = cq[:, nope_w + h * LANES:nope_w + (h + 1) * LANES]
            if latent:
                qr = _rope(qr, c_ref[pl.ds(q0, TQ), :], s_ref[pl.ds(q0, TQ), :], half)
            s = (_dot_nt(qn, kn_sc[:, hs]) + _dot_nt(qr.astype(BF16), kr_sc[...])) * scale
            m = jnp.max(s, axis=-1, keepdims=True)
            p = jnp.exp(s - m)
            l = jnp.sum(p, axis=-1, keepdims=True)
            o = _dot(p.astype(BF16), v_sc[:, hs]) / l
            o_ref[pl.ds(q0, TQ), hs] = o.astype(BF16)
        return carry

    lax.fori_loop(0, seq // TQ, qtile, 0)


def mla_branch(proj, row0, batch, seq, lp, layer, ctx=None):
    latent = ctx is not None
    rb = row0 // seq
    ctx_len = ctx[0].shape[2] if latent else 0
    const = lambda b: (0, 0)
    in_specs = [
        pl.BlockSpec((seq, MLA_Q_RANK), lambda b: (rb + b, C_MCQ // MLA_Q_RANK)),
        pl.BlockSpec((seq, MLA_KV_RANK), lambda b: (rb + b, C_MCKV // MLA_KV_RANK)),
        pl.BlockSpec((seq, LANES), lambda b: (rb + b, C_SMALL // LANES)),
    ]
    args = [proj, proj, proj]
    if latent:
        in_specs += [
            pl.BlockSpec((None, None, ctx_len, MLA_KV_RANK), lambda b: (b, layer, 0, 0)),
            pl.BlockSpec((None, None, ctx_len, LANES), lambda b: (b, layer, 0, 0)),
            pl.BlockSpec((seq, LANES), const),
            pl.BlockSpec((seq, LANES), const),
        ]
        args += [ctx[0], ctx[1], lp["rope_c32"], lp["rope_s32"]]
    in_specs += [
        pl.BlockSpec((1, MLA_Q_RANK), const),
        pl.BlockSpec((1, MLA_KV_RANK), const),
        pl.BlockSpec((MLA_Q_RANK, 2 * MLA_HEADS * LANES), const),
        pl.BlockSpec((MLA_KV_RANK, MLA_HEADS * MLA_NOPE), const),
        pl.BlockSpec((MLA_KV_RANK, BRANCH_W), const),
    ]
    args += [lp["mla_q_norm"], lp["mla_kv_norm"], lp["w_uq"], lp["w_uk"], lp["w_uv"]]
    out_specs = [pl.BlockSpec((seq, BRANCH_W), lambda b: (b, 0))]
    out_shape = [jax.ShapeDtypeStruct((batch * seq, BRANCH_W), BF16)]
    if not latent:
        out_specs += [pl.BlockSpec((seq, MLA_KV_RANK), lambda b: (b, 0)),
                      pl.BlockSpec((seq, MLA_ROPE), lambda b: (b, 0))]
        out_shape += [jax.ShapeDtypeStruct((batch * seq, MLA_KV_RANK), F32),
                      jax.ShapeDtypeStruct((batch * seq, MLA_ROPE), F32)]
    lk = ctx_len + seq
    res = pl.pallas_call(
        functools.partial(_mla_kernel, seq=seq, ctx_len=ctx_len, latent=latent),
        grid=(batch,),
        in_specs=in_specs,
        out_specs=out_specs,
        out_shape=out_shape,
        scratch_shapes=[pltpu.VMEM((lk, MLA_HEADS * MLA_NOPE), BF16), pltpu.VMEM((lk, BRANCH_W), BF16),
                        pltpu.VMEM((lk, LANES), BF16)],
        compiler_params=_cparams(("parallel",)),
        name="mla_latent" if latent else "mla_context",
    )(*args)
    return res if not latent else res[0]


def _merge_kernel(o0_ref, o1_ref, o2_ref, o3_ref, g0_ref, g1_ref, g2_ref, g3_ref, wb_ref, wo_ref, x_ref,
                  ga_ref, shf_ref, scf_ref, lg_ref, lb_ref, rw_ref, rb_ref, x1_ref, u2_ref, lo_ref, acc_sc):
    j = pl.program_id(1)

    @pl.when(j == 0)
    def _():
        acc_sc[...] = jnp.zeros(acc_sc.shape, F32)

    merged = None
    for n, (o_ref, g_ref) in enumerate(((o0_ref, g0_ref), (o1_ref, g1_ref), (o2_ref, g2_ref), (o3_ref, g3_ref))):
        term = _sigmoid(g_ref[...]) * _dot(o_ref[...], wb_ref[n])
        merged = term if merged is None else merged + term
    acc_sc[...] += _dot(merged.astype(BF16), wo_ref[...])

    @pl.when(j == pl.num_programs(1) - 1)
    def _():
        h = DEEPNORM_ALPHA * x_ref[...] + ga_ref[...] * acc_sc[...]
        x1 = _ln(h) * lg_ref[...] + lb_ref[...]
        x1_ref[...] = x1
        u2 = (_ln(x1) * (1.0 + scf_ref[...]) + shf_ref[...]).astype(BF16)
        u2_ref[...] = u2.astype(F32)
        lo_ref[...] = _dot(u2, rw_ref[...]) + rb_ref[...]


def merge_and_route(outs, proj, x, mods, lp, layer, n_ctx_rows, lat_len):
    t = x.shape[0]
    tm, tn = TM_MERGE, TN_MERGE
    row = functools.partial(_mod_row, tm=tm, n_ctx_rows=n_ctx_rows, lat_len=lat_len)
    gate_spec = lambda n: pl.BlockSpec((tm, tn), lambda i, j: (i, (C_GATE + n * D_MODEL) // tn + j))
    vec = lambda k: pl.BlockSpec((None, 1, D_MODEL), lambda i, j: (row(i), 0, k))
    cvec = pl.BlockSpec((None, 1, D_MODEL), lambda i, j: (layer, 0, 0))
    in_specs = (
        [pl.BlockSpec((tm, BRANCH_W), lambda i, j: (i, 0))] * N_BRANCH
        + [gate_spec(n) for n in range(N_BRANCH)]
        + [
            pl.BlockSpec((None, N_BRANCH, BRANCH_W, tn), lambda i, j: (layer, 0, 0, j)),
            pl.BlockSpec((None, tn, D_MODEL), lambda i, j: (layer, j, 0)),
            pl.BlockSpec((tm, D_MODEL), lambda i, j: (i, 0)),
            vec(2), vec(3), vec(4), cvec, cvec,
            pl.BlockSpec((None, D_MODEL, LANES), lambda i, j: (layer, 0, 0)),
            pl.BlockSpec((None, 1, LANES), lambda i, j: (layer, 0, 0)),
        ]
    )
    return pl.pallas_call(
        _merge_kernel,
        grid=(t // tm, D_MODEL // tn),
        in_specs=in_specs,
        out_specs=[
            pl.BlockSpec((tm, D_MODEL), lambda i, j: (i, 0)),
            pl.BlockSpec((tm, D_MODEL), lambda i, j: (i, 0)),
            pl.BlockSpec((tm, LANES), lambda i, j: (i, 0)),
        ],
        out_shape=[
            jax.ShapeDtypeStruct((t, D_MODEL), F32),
            jax.ShapeDtypeStruct((t, D_MODEL), F32),
            jax.ShapeDtypeStruct((t, LANES), F32),
        ],
        scratch_shapes=[pltpu.VMEM((tm, D_MODEL), F32)],
        compiler_params=_cparams(("parallel", "arbitrary")),
        name="merge_route",
    )(*outs, proj, proj, proj, proj, lp["w_branch"], lp["w_out"], x, mods, mods, mods,
      lp["ln_mix_g"], lp["ln_mix_b"], lp["router_w"], lp["router_b"])


def _moe_up_kernel(e_ref, src_ref, valid_ref, first_ref, x_ref, wg_ref, wl_ref, bg_ref, bl_ref, h_ref,
                   wg_sc, wl_sc):
    blk = pl.program_id(1)

    @pl.when(first_ref[blk] == 1)
    def _():
        wg_sc[...] = wg_ref[...].astype(BF16)
        wl_sc[...] = wl_ref[...].astype(BF16)

    @pl.when(valid_ref[blk] == 1)
    def _():
        x = x_ref[...].astype(BF16)
        g = jnp.minimum(_dot(x, wg_sc[...]) + bg_ref[...], SWIGLU_LIMIT)
        lin = jnp.clip(_dot(x, wl_sc[...]) + bl_ref[...], -SWIGLU_LIMIT, SWIGLU_LIMIT)
        h_ref[...] = ((lin + 1.0) * (g * _sigmoid(SWIGLU_ALPHA * g))).astype(BF16)


def _moe_down_kernel(e_ref, src_ref, valid_ref, first_ref, h_ref, wd_ref, bd_ref, y_ref, wd_sc):
    blk = pl.program_id(1)

    @pl.when(first_ref[blk] == 1)
    def _():
        wd_sc[...] = wd_ref[...].astype(BF16)

    @pl.when(valid_ref[blk] == 1)
    def _():
        y_ref[...] = _dot(h_ref[...], wd_sc[...]) + bd_ref[...]


def moe_experts(xs, plan, w_gu, b_gu, w_down, b_down, layer):
    p = xs.shape[0]
    nb = p // TM_MOE
    tm, tf, tn = TM_MOE, TF_MOE, TN_DOWN
    nf = EXPERT_FF // tf
    hidden = pl.pallas_call(
        _moe_up_kernel,
        grid_spec=pltpu.PrefetchScalarGridSpec(
            num_scalar_prefetch=4,
            grid=(nf, nb),
            in_specs=[
                pl.BlockSpec((tm, D_MODEL), lambda f, b, e, s, v, fi: (s[b], 0)),
                pl.BlockSpec((None, None, D_MODEL, tf), lambda f, b, e, s, v, fi: (layer, e[b], 0, f)),
                pl.BlockSpec((None, None, D_MODEL, tf), lambda f, b, e, s, v, fi: (layer, e[b], 0, nf + f)),
                pl.BlockSpec((None, None, 1, tf), lambda f, b, e, s, v, fi: (layer, e[b], 0, f)),
                pl.BlockSpec((None, None, 1, tf), lambda f, b, e, s, v, fi: (layer, e[b], 0, nf + f)),
            ],
            out_specs=pl.BlockSpec((tm, tf), lambda f, b, e, s, v, fi: (s[b], f)),
            scratch_shapes=[pltpu.VMEM((D_MODEL, tf), BF16), pltpu.VMEM((D_MODEL, tf), BF16)],
        ),
        out_shape=jax.ShapeDtypeStruct((p, EXPERT_FF), BF16),
        compiler_params=_cparams(("arbitrary", "arbitrary")),
        name="moe_up",
    )(*plan, xs, w_gu, w_gu, b_gu, b_gu)
    return pl.pallas_call(
        _moe_down_kernel,
        grid_spec=pltpu.PrefetchScalarGridSpec(
            num_scalar_prefetch=4,
            grid=(D_MODEL // tn, nb),
            in_specs=[
                pl.BlockSpec((tm, EXPERT_FF), lambda n, b, e, s, v, fi: (s[b], 0)),
                pl.BlockSpec((None, None, EXPERT_FF, tn), lambda n, b, e, s, v, fi: (layer, e[b], 0, n)),
                pl.BlockSpec((None, None, 1, tn), lambda n, b, e, s, v, fi: (layer, e[b], 0, n)),
            ],
            out_specs=pl.BlockSpec((tm, tn), lambda n, b, e, s, v, fi: (s[b], n)),
            scratch_shapes=[pltpu.VMEM((EXPERT_FF, tn), BF16)],
        ),
        out_shape=jax.ShapeDtypeStruct((p, D_MODEL), F32),
        compiler_params=_cparams(("arbitrary", "arbitrary")),
        name="moe_down",
    )(*plan, hidden, w_down, b_down)


def route(logits):
    t = logits.shape[0]
    a = t * TOP_K
    top_v, top_i = lax.top_k(logits, TOP_K)
    gate = jax.nn.softmax(top_v, axis=-1)
    flat_e = top_i.reshape(-1)
    onehot = (flat_e[:, None] == jnp.arange(N_EXPERTS, dtype=jnp.int32)[None, :]).astype(jnp.int32)
    csum = jnp.cumsum(onehot, axis=0)
    rank = jnp.take_along_axis(csum, flat_e[:, None], axis=1)[:, 0] - 1
    counts = csum[-1]
    padded = (counts + TM_MOE - 1) // TM_MOE * TM_MOE
    ends = jnp.cumsum(padded)
    slot = (ends - padded)[flat_e] + rank
    nb = a // TM_MOE + N_EXPERTS
    flat_t = jnp.repeat(jnp.arange(t, dtype=jnp.int32), TOP_K)
    tok = jnp.zeros((nb * TM_MOE,), jnp.int32).at[slot].set(flat_t)
    blk = jnp.arange(nb, dtype=jnp.int32)
    n_used = (ends[-1] // TM_MOE).astype(jnp.int32)
    src = jnp.minimum(blk, n_used - 1)
    blk_e = jnp.minimum(jnp.searchsorted(ends, src * TM_MOE, side="right"), N_EXPERTS - 1).astype(jnp.int32)
    valid = (blk < n_used).astype(jnp.int32)
    first = jnp.where((blk == 0) | (blk_e != jnp.roll(blk_e, 1)), 1, 0).astype(jnp.int32)
    return gate, slot.reshape(t, TOP_K), tok, (blk_e, src, valid, first)


def _ffn_norm_kernel(x_ref, y_ref, gate_ref, gf_ref, lg_ref, lb_ref, o_ref):
    gate = gate_ref[...]
    ffn = y_ref[0] * gate[:, 0:1]
    for k in range(1, TOP_K):
        ffn = ffn + y_ref[k] * gate[:, k:k + 1]
    h = DEEPNORM_ALPHA * x_ref[...] + gf_ref[...] * ffn
    o_ref[...] = _ln(h) * lg_ref[...] + lb_ref[...]


def ffn_norm(x1, y_rows, gate, mods, lp, layer, n_ctx_rows, lat_len):
    t = x1.shape[0]
    tm = TM_LN
    row = functools.partial(_mod_row, tm=tm, n_ctx_rows=n_ctx_rows, lat_len=lat_len)
    cvec = pl.BlockSpec((None, 1, D_MODEL), lambda i: (layer, 0, 0))
    return pl.pallas_call(
        _ffn_norm_kernel,
        grid=(t // tm,),
        in_specs=[
            pl.BlockSpec((tm, D_MODEL), lambda i: (i, 0)),
            pl.BlockSpec((TOP_K, tm, D_MODEL), lambda i: (0, i, 0)),
            pl.BlockSpec((tm, TOP_K), lambda i: (i, 0)),
            pl.BlockSpec((None, 1, D_MODEL), lambda i: (row(i), 0, 5)),
            cvec, cvec,
        ],
        out_specs=pl.BlockSpec((tm, D_MODEL), lambda i: (i, 0)),
        out_shape=jax.ShapeDtypeStruct((t, D_MODEL), F32),
        compiler_params=_cparams(("parallel",)),
        name="ffn_norm",
    )(x1, y_rows, gate, mods, lp["ln_ffn_g"], lp["ln_ffn_b"])


def _rope_tables(n_tokens, seg, width):
    rows = n_tokens // GRID_W
    row = jnp.repeat(jnp.arange(rows, dtype=F32), GRID_W)
    col = jnp.tile(jnp.arange(GRID_W, dtype=F32), rows)
    n_freq = seg // 4
    inv = ROPE_THETA ** (-jnp.arange(n_freq, dtype=F32) / n_freq)
    ang = jnp.concatenate([row[:, None] * inv, col[:, None] * inv], axis=-1)
    cos, sin = jnp.cos(ang), jnp.sin(ang)
    reps = width // seg
    return (jnp.tile(jnp.concatenate([cos, cos], axis=-1), (1, reps)),
            jnp.tile(jnp.concatenate([-sin, sin], axis=-1), (1, reps)))


def _pack_w_in(w_in):
    pad = jnp.zeros(w_in.shape[:-1] + (C_GATE - C_SMALL - MLA_ROPE - 2 * SSD_HEADS,), w_in.dtype)
    return jnp.concatenate([
        w_in[..., BRANCH_W:_O_DT], w_in[..., :BRANCH_W], w_in[..., _O_DQ:_O_MKR], w_in[..., _O_GQ:_O_GATE],
        w_in[..., _O_MKR:_O_GQ], w_in[..., _O_DT:_O_DQ], pad, w_in[..., _O_GATE:],
    ], axis=-1).astype(BF16)


def _pack_w_uq(w_uq):
    w = w_uq.reshape(MLA_Q_RANK, MLA_HEADS, MLA_NOPE + MLA_ROPE)
    nope = w[:, :, :MLA_NOPE].reshape(MLA_Q_RANK, MLA_HEADS * MLA_NOPE)
    rope = jnp.pad(w[:, :, MLA_NOPE:], ((0, 0), (0, 0), (0, LANES - MLA_ROPE)))
    return jnp.concatenate([nope, rope.reshape(MLA_Q_RANK, MLA_HEADS * LANES)], axis=-1).astype(BF16)


def _small_row(v16):
    return jnp.zeros((1, LANES), F32).at[0, SMALL_DT:SMALL_DT + 2 * SSD_HEADS].set(v16.reshape(-1))


def kernel(x_prompt, x_sample, state_ssd, cache_diff_k, cache_diff_v, cache_mla_ckv, cache_mla_kr, cache_gqa_k, cache_gqa_v, c, c_ctx, w_mod, b_mod, w_in, ssd_conv_w, ssd_conv_b, ssd_dt_bias, ssd_a_log, ssd_d, ssd_norm, diff_lambda, diff_norm, mla_q_norm, mla_w_uq, mla_kv_norm, mla_w_uk, mla_w_uv, gqa_q_norm, gqa_k_norm, w_branch, w_out, ln_mix_g, ln_mix_b, router_w, router_b, exp_w_gu, exp_b_gu, exp_w_down, exp_b_down, ln_ffn_g, ln_ffn_b):
    nb_ctx, len_ctx, _ = x_prompt.shape
    nb_lat, len_lat, _ = x_sample.shape
    past = cache_diff_k.shape[2]
    n_ctx_rows = nb_ctx * len_ctx
    t = n_ctx_rows + nb_lat * len_lat

    cvec = jnp.zeros((SUBLANES, D_MODEL), F32).at[0].set(c_ctx).at[1:1 + nb_lat].set(c)
    mods_all = modulation(cvec, w_mod, b_mod)

    w_in_p = _pack_w_in(w_in)
    w_branch_b = w_branch.astype(BF16)
    w_out_b = w_out.astype(BF16)
    router_w_p = jnp.pad(router_w, ((0, 0), (0, 0), (0, LANES - N_EXPERTS))).astype(BF16)
    router_b_p = jnp.pad(router_b, ((0, 0), (0, LANES - N_EXPERTS))).reshape(DEPTH, 1, LANES)
    c32, s32 = _rope_tables(len_lat, 2 * (DIFF_D // 2), LANES)
    c64, s64 = _rope_tables(len_lat, GQA_HEAD_DIM, LANES)
    n_pair = SSD_HEADS // 2
    state_in = state_ssd.reshape(nb_lat, DEPTH, 2, n_pair, 2 * SSD_HEAD_DIM, SSD_STATE)
    diff_kc = cache_diff_k.reshape(nb_lat, DEPTH, past, BRANCH_W)
    diff_vc = cache_diff_v.reshape(nb_lat, DEPTH, past, BRANCH_W)
    mla_krc = jnp.pad(cache_mla_kr, ((0, 0), (0, 0), (0, 0), (0, LANES - MLA_ROPE)))
    gqa_kc = cache_gqa_k.reshape(nb_lat, DEPTH, past, GQA_KV_HEADS * GQA_HEAD_DIM)
    gqa_vc = cache_gqa_v.reshape(nb_lat, DEPTH, past, GQA_KV_HEADS * GQA_HEAD_DIM)
    vecs = dict(ln_mix_g=ln_mix_g.reshape(DEPTH, 1, D_MODEL), ln_mix_b=ln_mix_b.reshape(DEPTH, 1, D_MODEL),
                ln_ffn_g=ln_ffn_g.reshape(DEPTH, 1, D_MODEL), ln_ffn_b=ln_ffn_b.reshape(DEPTH, 1, D_MODEL),
                w_branch=w_branch_b, w_out=w_out_b, router_w=router_w_p, router_b=router_b_p)
    b_gu = exp_b_gu.reshape(DEPTH, N_EXPERTS, 1, 2 * EXPERT_FF)
    b_down = exp_b_down.reshape(DEPTH, N_EXPERTS, 1, D_MODEL)

    x = jnp.concatenate([x_prompt.reshape(n_ctx_rows, D_MODEL), x_sample.reshape(-1, D_MODEL)], axis=0)
    new_state, new_dk, new_dv, new_ckv, new_kr, new_gk, new_gv = [], [], [], [], [], [], []
    for l in range(DEPTH):
        lp = dict(vecs)
        lp.update(
            conv_w=jnp.pad(ssd_conv_w[l], ((0, SUBLANES - SSD_CONV), (0, 0))),
            conv_b=ssd_conv_b[l].reshape(1, SSD_XBC),
            dt_bias=_small_row(ssd_dt_bias[l]),
            a_row=_small_row(-jnp.exp(ssd_a_log[l])),
            d_row=jnp.repeat(ssd_d[l], SSD_HEAD_DIM).reshape(1, BRANCH_W),
            ssd_norm=ssd_norm[l].reshape(1, BRANCH_W),
            diff_lambda=diff_lambda[l],
            diff_norm=diff_norm[l].reshape(1, 2 * DIFF_D),
            mla_q_norm=mla_q_norm[l].reshape(1, MLA_Q_RANK),
            mla_kv_norm=mla_kv_norm[l].reshape(1, MLA_KV_RANK),
            w_uq=_pack_w_uq(mla_w_uq[l]),
            w_uk=mla_w_uk[l].astype(BF16),
            w_uv=mla_w_uv[l].astype(BF16),
            gqa_q_norm=gqa_q_norm[l].reshape(1, GQA_HEAD_DIM),
            gqa_k_norm=gqa_k_norm[l].reshape(1, GQA_HEAD_DIM),
            rope_c32=c32, rope_s32=s32, rope_c64=c64, rope_s64=s64,
        )
        mods = mods_all[l].reshape(SUBLANES, 1, 6 * D_MODEL)
        proj = in_projection(x, mods, w_in_p, l, n_ctx_rows, len_lat)

        o_ssd_c, st = ssd_branch(proj, 0, nb_ctx, len_ctx, lp, l)
        o_ssd_l = ssd_branch(proj, n_ctx_rows, nb_lat, len_lat, lp, l, state_in)
        o_diff_c, dk_c, dv_c = diff_branch(proj, 0, nb_ctx, len_ctx, lp, l)
        o_diff_l = diff_branch(proj, n_ctx_rows, nb_lat, len_lat, lp, l, (diff_kc, diff_vc))
        o_mla_c, ckv_c, kr_c = mla_branch(proj, 0, nb_ctx, len_ctx, lp, l)
        o_mla_l = mla_branch(proj, n_ctx_rows, nb_lat, len_lat, lp, l, (cache_mla_ckv, mla_krc))
        o_gqa_c, gk_c, gv_c = gqa_branch(proj, 0, nb_ctx, len_ctx, lp, l)
        o_gqa_l = gqa_branch(proj, n_ctx_rows, nb_lat, len_lat, lp, l, (gqa_kc, gqa_vc))
        outs = [jnp.concatenate(pair, axis=0) for pair in
                ((o_ssd_c, o_ssd_l), (o_diff_c, o_diff_l), (o_mla_c, o_mla_l), (o_gqa_c, o_gqa_l))]

        new_state.append(st.reshape(nb_ctx, 2, SSD_HEADS, SSD_HEAD_DIM, SSD_STATE))
        new_dk.append(dk_c.reshape(nb_ctx, len_ctx, DIFF_HEADS, 2 * DIFF_D))
        new_dv.append(dv_c.reshape(nb_ctx, len_ctx, DIFF_HEADS, 2 * DIFF_D))
        new_ckv.append(ckv_c.reshape(nb_ctx, len_ctx, MLA_KV_RANK))
        new_kr.append(kr_c.reshape(nb_ctx, len_ctx, MLA_ROPE))
        new_gk.append(gk_c.reshape(nb_ctx, len_ctx, GQA_KV_HEADS, GQA_HEAD_DIM))
        new_gv.append(gv_c.reshape(nb_ctx, len_ctx, GQA_KV_HEADS, GQA_HEAD_DIM))

        x1, u2, logits = merge_and_route(outs, proj, x, mods, lp, l, n_ctx_rows, len_lat)
        gate, slot, tok, plan = route(logits[:, :N_EXPERTS])
        yb = moe_experts(u2[tok], plan, exp_w_gu, b_gu, exp_w_down, b_down, l)
        y_rows = yb[slot.T.reshape(-1)].reshape(TOP_K, t, D_MODEL)
        x = ffn_norm(x1, y_rows, gate, mods, lp, l, n_ctx_rows, len_lat)

    y_prompt = x[:n_ctx_rows].reshape(nb_ctx, len_ctx, D_MODEL)
    y_sample = x[n_ctx_rows:].reshape(nb_lat, len_lat, D_MODEL)
    stack = lambda parts: jnp.stack(parts, axis=1)
    return (y_prompt, y_sample, stack(new_state), stack(new_dk), stack(new_dv), stack(new_ckv),
            stack(new_kr), stack(new_gk), stack(new_gv))
```

```python
import functools
import math

import jax
import jax.numpy as jnp
from jax import lax
from jax.experimental import pallas as pl
from jax.experimental.pallas import tpu as pltpu

F32 = jnp.float32
BF16 = jnp.bfloat16

D_MODEL = 2048
DEPTH = 4
GRID_W = 64
ROPE_THETA = 10000.0
BRANCH_W = 512
N_BRANCH = 4
SSD_HEADS = 8
SSD_HEAD_DIM = 64
SSD_STATE = 128
SSD_CONV = 5
SSD_CHUNK = 128
SSD_XBC = 1024
DIFF_HEADS = 4
DIFF_D = 64
MLA_HEADS = 4
MLA_NOPE = 128
MLA_ROPE = 64
MLA_Q_RANK = 512
MLA_KV_RANK = 256
GQA_HEAD_DIM = 128
GQA_HEADS = 4
GQA_KV_HEADS = 2
N_EXPERTS = 32
TOP_K = 4
EXPERT_FF = 1024
SWIGLU_LIMIT = 7.0
SWIGLU_ALPHA = 1.702
DEEPNORM_ALPHA = (2.0 * DEPTH) ** 0.25
EPS = 1e-6

LANES = 128
SUBLANES = 8

_O_DT = 1536
_O_DQ = 1552
_O_MKR = 3856
_O_GQ = 3920
_O_GATE = 4944
IN_WIDTH = 13136
C_XBC = 0
C_Z = 1024
C_DQ = 1536
C_DK = 2048
C_DV = 2560
C_MCQ = 3072
C_MCKV = 3584
C_GQ = 3840
C_GK = 4352
C_GV = 4608
C_SMALL = 4864
C_GATE = 5120
PROJ_W = C_GATE + N_BRANCH * D_MODEL
SMALL_DT = MLA_ROPE

VMEM_LIMIT = 60000 * 1024

TM_IN = 1024
TN_IN = 1024
TM_MERGE = 512
TN_MERGE = 512
TM_LN = 512
TM_MOE = 512
TN_MOD = 1024
TQ = 256


def _cparams(sem):
    return pltpu.CompilerParams(dimension_semantics=sem, vmem_limit_bytes=VMEM_LIMIT)


def _dot(a, b):
    return jnp.dot(a, b, preferred_element_type=F32)


def _dot_nt(a, b):
    return lax.dot_general(a, b, (((1,), (1,)), ((), ())), preferred_element_type=F32)


def _ln(x):
    mu = jnp.mean(x, axis=-1, keepdims=True)
    xc = x - mu
    var = jnp.mean(xc * xc, axis=-1, keepdims=True)
    return xc * lax.rsqrt(var + EPS)


def _rms(x, g):
    return x * lax.rsqrt(jnp.mean(x * x, axis=-1, keepdims=True) + EPS) * g


def _sigmoid(x):
    return 1.0 / (1.0 + jnp.exp(-x))


def _silu(x):
    return x * _sigmoid(x)


def _rope(x, c, s, half):
    w = x.shape[-1]
    lane = lax.broadcasted_iota(jnp.int32, x.shape, 1)
    lo = (lane & (2 * half - 1)) < half
    swapped = jnp.where(lo, pltpu.roll(x, w - half, 1), pltpu.roll(x, half, 1))
    return x * c + swapped * s


def _softmax_pv(q, k, v, scale):
    s = _dot_nt(q, k) * scale
    m = jnp.max(s, axis=-1, keepdims=True)
    p = jnp.exp(s - m)
    l = jnp.sum(p, axis=-1, keepdims=True)
    return _dot(p.astype(BF16), v) / l


def _mod_row(i, tm, n_ctx_rows, lat_len):
    r = i * tm
    return jnp.where(r < n_ctx_rows, 0, 1 + (r - n_ctx_rows) // lat_len)


def _mod_kernel(c_ref, w_ref, b_ref, o_ref):
    c = c_ref[...]
    o_ref[...] = _dot(_silu(c).astype(BF16), w_ref[...].astype(BF16)) + b_ref[...]


def modulation(cvec, w_mod, b_mod):
    n = w_mod.shape[-1]
    return pl.pallas_call(
        _mod_kernel,
        grid=(DEPTH, n // TN_MOD),
        in_specs=[
            pl.BlockSpec((SUBLANES, D_MODEL), lambda l, j: (0, 0)),
            pl.BlockSpec((None, D_MODEL, TN_MOD), lambda l, j: (l, 0, j)),
            pl.BlockSpec((None, 1, TN_MOD), lambda l, j: (l, 0, j)),
        ],
        out_specs=pl.BlockSpec((None, SUBLANES, TN_MOD), lambda l, j: (l, 0, j)),
        out_shape=jax.ShapeDtypeStruct((DEPTH, SUBLANES, n), F32),
        compiler_params=_cparams(("parallel", "parallel")),
        name="modulation",
    )(cvec, w_mod, b_mod.reshape(DEPTH, 1, n))


def _inproj_kernel(x_ref, sh_ref, sc_ref, w_ref, o_ref, u_sc):
    @pl.when(pl.program_id(1) == 0)
    def _():
        u = _ln(x_ref[...]) * (1.0 + sc_ref[...]) + sh_ref[...]
        u_sc[...] = u.astype(BF16)

    o_ref[...] = _dot(u_sc[...], w_ref[...])


def in_projection(x, mods, w_in_p, layer, n_ctx_rows, lat_len):
    t = x.shape[0]
    tm = TM_IN
    row = functools.partial(_mod_row, tm=tm, n_ctx_rows=n_ctx_rows, lat_len=lat_len)
    return pl.pallas_call(
        _inproj_kernel,
        grid=(t // tm, PROJ_W // TN_IN),
        in_specs=[
            pl.BlockSpec((tm, D_MODEL), lambda i, j: (i, 0)),
            pl.BlockSpec((None, 1, D_MODEL), lambda i, j: (row(i), 0, 0)),
            pl.BlockSpec((None, 1, D_MODEL), lambda i, j: (row(i), 0, 1)),
            pl.BlockSpec((None, D_MODEL, TN_IN), lambda i, j: (layer, 0, j)),
        ],
        out_specs=pl.BlockSpec((tm, TN_IN), lambda i, j: (i, j)),
        out_shape=jax.ShapeDtypeStruct((t, PROJ_W), F32),
        scratch_shapes=[pltpu.VMEM((tm, D_MODEL), BF16)],
        compiler_params=_cparams(("parallel", "arbitrary")),
        name="in_projection",
    )(x, mods, mods, w_in_p)


def _lane_col(a, lane):
    idx = lax.broadcasted_iota(jnp.int32, a.shape, 1)
    return jnp.sum(jnp.where(idx == lane, a, 0.0), axis=1, keepdims=True)


def _ssd_kernel(*refs, seq, latent):
    if latent:
        (z_ref, xbc_ref, small_ref, cw_ref, cb_ref, dtb_ref, arow_ref, drow_ref, nw_ref, h0_ref,
         o_ref, xpad_sc, xs_sc, y_sc, dt_sc, h_sc) = refs
    else:
        (z_ref, xbc_ref, small_ref, cw_ref, cb_ref, dtb_ref, arow_ref, drow_ref, nw_ref,
         o_ref, hout_ref, xpad_sc, xs_sc, y_sc, dt_sc, h_sc) = refs
    q = SSD_CHUNK
    nc = seq // q
    pad = SUBLANES
    ext = q + 2 * pad

    xpad_sc[0:pad, :] = jnp.zeros((pad, SSD_XBC), F32)
    xpad_sc[pad + seq:2 * pad + seq, :] = jnp.zeros((pad, SSD_XBC), F32)

    def copy_in(c, carry):
        r0 = pl.multiple_of(c * q, q)
        xpad_sc[pl.ds(pad + r0, q), :] = xbc_ref[pl.ds(r0, q), :]
        return carry

    lax.fori_loop(0, nc, copy_in, 0)

    def conv(c, carry):
        r0 = pl.multiple_of(c * q, q)
        for ct in range(SSD_XBC // LANES):
            cols = slice(ct * LANES, (ct + 1) * LANES)
            win = xpad_sc[pl.ds(r0, ext), cols]
            acc = cb_ref[:, cols] + win[pad:pad + q] * cw_ref[2:3, cols]
            for k in (0, 1, 3, 4):
                shifted = pltpu.roll(win, (2 - k) % ext, 0)[pad:pad + q]
                acc = acc + shifted * cw_ref[k:k + 1, cols]
            xs_sc[pl.ds(r0, q), cols] = _silu(acc)
        sd = small_ref[pl.ds(r0, q), :] + dtb_ref[...]
        dt_sc[pl.ds(r0, q), :] = jnp.maximum(sd, 0.0) + jnp.log(1.0 + jnp.exp(-jnp.abs(sd)))
        return carry

    lax.fori_loop(0, nc, conv, 0)

    if latent:
        h_sc[...] = h0_ref[...]
    else:
        h_sc[...] = jnp.zeros(h_sc.shape, F32)

    ri = lax.broadcasted_iota(jnp.int32, (q, q), 0)
    ci = lax.broadcasted_iota(jnp.int32, (q, q), 1)
    lane = lax.broadcasted_iota(jnp.int32, (q, LANES), 1)
    lane_lo = lane < SSD_HEAD_DIM
    row_lo = lax.broadcasted_iota(jnp.int32, (2 * SSD_HEAD_DIM, q), 0) < SSD_HEAD_DIM
    row_lo_n = lax.broadcasted_iota(jnp.int32, (2 * SSD_HEAD_DIM, SSD_STATE), 0) < SSD_HEAD_DIM

    def chunk(c, d):
        keep = (ci <= ri) if d == 0 else (ci >= ri)
        tri = jnp.where(keep, 1.0, 0.0)
        r0 = pl.multiple_of(c * q, q)
        xc = xs_sc[pl.ds(r0, q), 0:BRANCH_W]
        dtc = dt_sc[pl.ds(r0, q), :]
        ac = dtc * arow_ref[...]
        acum = jnp.dot(tri, ac, preferred_element_type=F32, precision=lax.Precision.HIGHEST)
        acum_t = acum.T
        dt_t = dtc.T
        tot = acum[q - 1:q, :] if d == 0 else acum[0:1, :]
        eac = jnp.exp(acum)
        w_t = (jnp.exp(tot - acum) * dtc).T
        etot = jnp.exp(tot)
        x_t = xc.T
        for g in range(2):
            bg = xs_sc[pl.ds(r0, q), BRANCH_W + g * SSD_STATE:BRANCH_W + (g + 1) * SSD_STATE].astype(BF16)
            cg = xs_sc[pl.ds(r0, q), BRANCH_W + 2 * SSD_STATE + g * SSD_STATE:
                       BRANCH_W + 2 * SSD_STATE + (g + 1) * SSD_STATE].astype(BF16)
            cb = _dot_nt(cg, bg)
            for pr in range(2):
                pair = g * 2 + pr
                h0 = pair * 2
                l0 = SMALL_DT + d * SSD_HEADS + h0
                atts = []
                for hh in range(2):
                    ln = l0 + hh
                    seg = _lane_col(acum, ln) - acum_t[ln:ln + 1, :]
                    dec = jnp.exp(jnp.where(keep, seg, -1e30))
                    atts.append(cb * dec * dt_t[ln:ln + 1, :])
                att = jnp.concatenate(atts, axis=1).astype(BF16)
                cols = slice(h0 * SSD_HEAD_DIM, (h0 + 2) * SSD_HEAD_DIM)
                xp = xc[:, cols]
                x2 = jnp.concatenate([jnp.where(lane_lo, xp, 0.0), jnp.where(lane_lo, 0.0, xp)],
                                     axis=0).astype(BF16)
                y_diag = _dot(att, x2)
                st_in = h_sc[d, pair]
                y_off = _dot_nt(cg, st_in.astype(BF16))
                esel = jnp.where(lane_lo, _lane_col(eac, l0), _lane_col(eac, l0 + 1))
                y = y_diag + y_off * esel
                if d == 0:
                    y_sc[pl.ds(r0, q), cols] = y
                else:
                    y_sc[pl.ds(r0, q), cols] = y_sc[pl.ds(r0, q), cols] + y
                wsel = jnp.where(row_lo, w_t[l0:l0 + 1, :], w_t[l0 + 1:l0 + 2, :])
                st_new = _dot((x_t[cols, :] * wsel).astype(BF16), bg)
                dsel = jnp.where(row_lo_n, _lane_col(etot, l0), _lane_col(etot, l0 + 1))
                h_sc[d, pair] = dsel * st_in + st_new

    def fwd(c, carry):
        chunk(c, 0)
        return carry

    def bwd(i, carry):
        chunk(nc - 1 - i, 1)
        return carry

    lax.fori_loop(0, nc, fwd, 0)
    lax.fori_loop(0, nc, bwd, 0)

    def finish(c, carry):
        r0 = pl.multiple_of(c * q, q)
        y = y_sc[pl.ds(r0, q), :] + drow_ref[...] * xs_sc[pl.ds(r0, q), 0:BRANCH_W]
        gated = y * _silu(z_ref[pl.ds(r0, q), :])
        o_ref[pl.ds(r0, q), :] = _rms(gated, nw_ref[...]).astype(BF16)
        return carry

    lax.fori_loop(0, nc, finish, 0)
    if not latent:
        hout_ref[...] = h_sc[...]


def ssd_branch(proj, row0, batch, seq, lp, layer, state=None):
    latent = state is not None
    rb = row0 // seq
    n_pair = SSD_HEADS // 2
    st_shape = (2, n_pair, 2 * SSD_HEAD_DIM, SSD_STATE)
    const = lambda b: (0, 0)
    in_specs = [
        pl.BlockSpec((seq, BRANCH_W), lambda b: (rb + b, C_Z // BRANCH_W)),
        pl.BlockSpec((seq, SSD_XBC), lambda b: (rb + b, C_XBC // SSD_XBC)),
        pl.BlockSpec((seq, LANES), lambda b: (rb + b, C_SMALL // LANES)),
        pl.BlockSpec((SUBLANES, SSD_XBC), const),
        pl.BlockSpec((1, SSD_XBC), const),
        pl.BlockSpec((1, LANES), const),
        pl.BlockSpec((1, LANES), const),
        pl.BlockSpec((1, BRANCH_W), const),
        pl.BlockSpec((1, BRANCH_W), const),
    ]
    args = [proj, proj, proj, lp["conv_w"], lp["conv_b"], lp["dt_bias"], lp["a_row"], lp["d_row"],
            lp["ssd_norm"]]
    out_specs = [pl.BlockSpec((seq, BRANCH_W), lambda b: (b, 0))]
    out_shape = [jax.ShapeDtypeStruct((batch * seq, BRANCH_W), BF16)]
    if latent:
        in_specs.append(pl.BlockSpec((None, None) + st_shape, lambda b: (b, layer, 0, 0, 0, 0)))
        args.append(state)
    else:
        out_specs.append(pl.BlockSpec((None,) + st_shape, lambda b: (b, 0, 0, 0, 0)))
        out_shape.append(jax.ShapeDtypeStruct((batch,) + st_shape, F32))
    res = pl.pallas_call(
        functools.partial(_ssd_kernel, seq=seq, latent=latent),
        grid=(batch,),
        in_specs=in_specs,
        out_specs=out_specs,
        out_shape=out_shape,
        scratch_shapes=[
            pltpu.VMEM((seq + 2 * SUBLANES, SSD_XBC), F32),
            pltpu.VMEM((seq, SSD_XBC), F32),
            pltpu.VMEM((seq, BRANCH_W), F32),
            pltpu.VMEM((seq, LANES), F32),
            pltpu.VMEM(st_shape, F32),
        ],
        compiler_params=_cparams(("parallel",)),
        name="ssd_latent" if latent else "ssd_context",
    )(*args)
    return res if not latent else res[0]


def _diff_kernel(*refs, seq, ctx_len, latent, lam_init):
    if latent:
        (q_ref, k_ref, v_ref, kc_ref, vc_ref, c_ref, s_ref, lam_ref, nw_ref, o_ref, k_sc, v_sc) = refs
    else:
        (q_ref, k_ref, v_ref, lam_ref, nw_ref, o_ref, kout_ref, vout_ref, k_sc, v_sc) = refs
    lk = ctx_len + seq
    half = DIFF_D // 2
    if latent:
        k_sc[0:ctx_len, :] = kc_ref[...].astype(BF16)
        v_sc[0:ctx_len, :] = vc_ref[...].astype(BF16)
        kn = _rope(k_ref[...], c_ref[...], s_ref[...], half)
    else:
        kn = k_ref[...]
        kout_ref[...] = kn
        vout_ref[...] = v_ref[...]
    k_sc[ctx_len:lk, :] = kn.astype(BF16)
    v_sc[ctx_len:lk, :] = v_ref[...].astype(BF16)
    lv = lam_ref[...]
    lam = (jnp.exp(jnp.sum(lv[0:1] * lv[1:2], axis=1, keepdims=True))
           - jnp.exp(jnp.sum(lv[2:3] * lv[3:4], axis=1, keepdims=True)) + lam_init)
    scale = DIFF_D ** -0.5
    lane_lo = lax.broadcasted_iota(jnp.int32, (TQ, LANES), 1) < DIFF_D

    def qtile(t, carry):
        q0 = pl.multiple_of(t * TQ, TQ)
        qv = q_ref[pl.ds(q0, TQ), :]
        if latent:
            qv = _rope(qv, c_ref[pl.ds(q0, TQ), :], s_ref[pl.ds(q0, TQ), :], half)
        q1 = jnp.where(lane_lo, qv, 0.0).astype(BF16)
        q2 = jnp.where(lane_lo, 0.0, qv).astype(BF16)
        k = k_sc[...]
        v = v_sc[...]
        o = _softmax_pv(q1, k, v, scale) - lam * _softmax_pv(q2, k, v, scale)
        o_ref[pl.ds(q0, TQ), :] = (_rms(o, nw_ref[...]) * (1.0 - lam_init)).astype(BF16)
        return carry

    lax.fori_loop(0, seq // TQ, qtile, 0)


def diff_branch(proj, row0, batch, seq, lp, layer, ctx=None):
    latent = ctx is not None
    rb = row0 // seq
    ctx_len = ctx[0].shape[2] if latent else 0
    lam_init = 0.8 - 0.6 * math.exp(-0.3 * layer)
    wid = 2 * DIFF_D
    in_specs = [
        pl.BlockSpec((seq, wid), lambda b, h: (rb + b, C_DQ // wid + h)),
        pl.BlockSpec((seq, wid), lambda b, h: (rb + b, C_DK // wid + h)),
        pl.BlockSpec((seq, wid), lambda b, h: (rb + b, C_DV // wid + h)),
    ]
    args = [proj, proj, proj]
    if latent:
        in_specs += [
            pl.BlockSpec((None, None, ctx_len, wid), lambda b, h: (b, layer, 0, h)),
            pl.BlockSpec((None, None, ctx_len, wid), lambda b, h: (b, layer, 0, h)),
            pl.BlockSpec((seq, wid), lambda b, h: (0, 0)),
            pl.BlockSpec((seq, wid), lambda b, h: (0, 0)),
        ]
        args += [ctx[0], ctx[1], lp["rope_c32"], lp["rope_s32"]]
    in_specs += [
        pl.BlockSpec((4, DIFF_D), lambda b, h: (0, 0)),
        pl.BlockSpec((1, wid), lambda b, h: (0, 0)),
    ]
    args += [lp["diff_lambda"], lp["diff_norm"]]
    out_specs = [pl.BlockSpec((seq, wid), lambda b, h: (b, h))]
    out_shape = [jax.ShapeDtypeStruct((batch * seq, BRANCH_W), BF16)]
    if not latent:
        out_specs += [pl.BlockSpec((seq, wid), lambda b, h: (b, h))] * 2
        out_shape += [jax.ShapeDtypeStruct((batch * seq, BRANCH_W), F32)] * 2
    res = pl.pallas_call(
        functools.partial(_diff_kernel, seq=seq, ctx_len=ctx_len, latent=latent, lam_init=lam_init),
        grid=(batch, DIFF_HEADS),
        in_specs=in_specs,
        out_specs=out_specs,
        out_shape=out_shape,
        scratch_shapes=[pltpu.VMEM((ctx_len + seq, wid), BF16), pltpu.VMEM((ctx_len + seq, wid), BF16)],
        compiler_params=_cparams(("parallel", "parallel")),
        name="diff_latent" if latent else "diff_context",
    )(*args)
    return res if not latent else res[0]


def _gqa_kernel(*refs, seq, ctx_len, latent):
    if latent:
        (q_ref, k_ref, v_ref, kc_ref, vc_ref, c_ref, s_ref, qw_ref, kw_ref, o_ref, k_sc, v_sc) = refs
    else:
        (q_ref, k_ref, v_ref, qw_ref, kw_ref, o_ref, kout_ref, vout_ref, k_sc, v_sc) = refs
    lk = ctx_len + seq
    half = GQA_HEAD_DIM // 2
    hd = GQA_HEAD_DIM
    kn = _rms(k_ref[...], kw_ref[...])
    if latent:
        k_sc[0:ctx_len, :] = kc_ref[...].astype(BF16)
        v_sc[0:ctx_len, :] = vc_ref[...].astype(BF16)
        kn = _rope(kn, c_ref[...], s_ref[...], half)
    else:
        kout_ref[...] = kn
        vout_ref[...] = v_ref[...]
    k_sc[ctx_len:lk, :] = kn.astype(BF16)
    v_sc[ctx_len:lk, :] = v_ref[...].astype(BF16)
    scale = GQA_HEAD_DIM ** -0.5

    def qtile(t, carry):
        q0 = pl.multiple_of(t * TQ, TQ)
        qs = []
        for g in range(GQA_HEADS // GQA_KV_HEADS):
            qv = _rms(q_ref[pl.ds(q0, TQ), g * hd:(g + 1) * hd], qw_ref[...])
            if latent:
                qv = _rope(qv, c_ref[pl.ds(q0, TQ), :], s_ref[pl.ds(q0, TQ), :], half)
            qs.append(qv)
        qq = jnp.concatenate(qs, axis=0).astype(BF16)
        o = _softmax_pv(qq, k_sc[...], v_sc[...], scale)
        for g in range(GQA_HEADS // GQA_KV_HEADS):
            o_ref[pl.ds(q0, TQ), g * hd:(g + 1) * hd] = o[g * TQ:(g + 1) * TQ].astype(BF16)
        return carry

    lax.fori_loop(0, seq // TQ, qtile, 0)


def gqa_branch(proj, row0, batch, seq, lp, layer, ctx=None):
    latent = ctx is not None
    rb = row0 // seq
    ctx_len = ctx[0].shape[2] if latent else 0
    hd = GQA_HEAD_DIM
    grp = GQA_HEADS // GQA_KV_HEADS
    in_specs = [
        pl.BlockSpec((seq, grp * hd), lambda b, h: (rb + b, C_GQ // (grp * hd) + h)),
        pl.BlockSpec((seq, hd), lambda b, h: (rb + b, C_GK // hd + h)),
        pl.BlockSpec((seq, hd), lambda b, h: (rb + b, C_GV // hd + h)),
    ]
    args = [proj, proj, proj]
    if latent:
        in_specs += [
            pl.BlockSpec((None, None, ctx_len, hd), lambda b, h: (b, layer, 0, h)),
            pl.BlockSpec((None, None, ctx_len, hd), lambda b, h: (b, layer, 0, h)),
            pl.BlockSpec((seq, hd), lambda b, h: (0, 0)),
            pl.BlockSpec((seq, hd), lambda b, h: (0, 0)),
        ]
        args += [ctx[0], ctx[1], lp["rope_c64"], lp["rope_s64"]]
    in_specs += [pl.BlockSpec((1, hd), lambda b, h: (0, 0)), pl.BlockSpec((1, hd), lambda b, h: (0, 0))]
    args += [lp["gqa_q_norm"], lp["gqa_k_norm"]]
    out_specs = [pl.BlockSpec((seq, grp * hd), lambda b, h: (b, h))]
    out_shape = [jax.ShapeDtypeStruct((batch * seq, BRANCH_W), BF16)]
    if not latent:
        out_specs += [pl.BlockSpec((seq, hd), lambda b, h: (b, h))] * 2
        out_shape += [jax.ShapeDtypeStruct((batch * seq, GQA_KV_HEADS * hd), F32)] * 2
    res = pl.pallas_call(
        functools.partial(_gqa_kernel, seq=seq, ctx_len=ctx_len, latent=latent),
        grid=(batch, GQA_KV_HEADS),
        in_specs=in_specs,
        out_specs=out_specs,
        out_shape=out_shape,
        scratch_shapes=[pltpu.VMEM((ctx_len + seq, hd), BF16), pltpu.VMEM((ctx_len + seq, hd), BF16)],
        compiler_params=_cparams(("parallel", "parallel")),
        name="gqa_latent" if latent else "gqa_context",
    )(*args)
    return res if not latent else res[0]


def _mla_kernel(*refs, seq, ctx_len, latent):
    if latent:
        (cq_ref, ckv_ref, small_ref, ckvc_ref, krc_ref, c_ref, s_ref, qw_ref, kvw_ref, wuq_ref, wuk_ref,
         wuv_ref, o_ref, kn_sc, v_sc, kr_sc) = refs
    else:
        (cq_ref, ckv_ref, small_ref, qw_ref, kvw_ref, wuq_ref, wuk_ref, wuv_ref,
         o_ref, ckv_out_ref, kr_out_ref, kn_sc, v_sc, kr_sc) = refs
    lk = ctx_len + seq
    half = MLA_ROPE // 2
    ckv = _rms(ckv_ref[...], kvw_ref[...])
    if latent:
        cc = ckvc_ref[...].astype(BF16)
        kn_sc[0:ctx_len, :] = _dot(cc, wuk_ref[...]).astype(BF16)
        v_sc[0:ctx_len, :] = _dot(cc, wuv_ref[...]).astype(BF16)
        kr_sc[0:ctx_len, :] = krc_ref[...].astype(BF16)
    else:
        ckv_out_ref[...] = ckv
        kr_out_ref[...] = small_ref[:, 0:MLA_ROPE]
    cb = ckv.astype(BF16)
    kn_sc[ctx_len:lk, :] = _dot(cb, wuk_ref[...]).astype(BF16)
    v_sc[ctx_len:lk, :] = _dot(cb, wuv_ref[...]).astype(BF16)
    kr = small_ref[...]
    if latent:
        kr = _rope(kr, c_ref[...], s_ref[...], half)
    lane_lo = lax.broadcasted_iota(jnp.int32, (seq, LANES), 1) < MLA_ROPE
    kr_sc[ctx_len:lk, :] = jnp.where(lane_lo, kr, 0.0).astype(BF16)
    scale = (MLA_NOPE + MLA_ROPE) ** -0.5
    nope_w = MLA_HEADS * MLA_NOPE

    def qtile(t, carry):
        q0 = pl.multiple_of(t * TQ, TQ)
        cq = _dot(_rms(cq_ref[pl.ds(q0, TQ), :], qw_ref[...]).astype(BF16), wuq_ref[...])
        for h in range(MLA_HEADS):
            hs = slice(h * MLA_NOPE, (h + 1) * MLA_NOPE)
            qn = cq[:, hs].astype(BF16)
            qr = cq[:, nope_w + h * LANES:nope_w + (h + 1) * LANES]
            if latent:
                qr = _rope(qr, c_ref[pl.ds(q0, TQ), :], s_ref[pl.ds(q0, TQ), :], half)
            s = (_dot_nt(qn, kn_sc[:, hs]) + _dot_nt(qr.astype(BF16), kr_sc[...])) * scale
            m = jnp.max(s, axis=-1, keepdims=True)
            p = jnp.exp(s - m)
            l = jnp.sum(p, axis=-1, keepdims=True)
            o = _dot(p.astype(BF16), v_sc[:, hs]) / l
            o_ref[pl.ds(q0, TQ), hs] = o.astype(BF16)
        return carry

    lax.fori_loop(0, seq // TQ, qtile, 0)


def mla_branch(proj, row0, batch, seq, lp, layer, ctx=None):
    latent = ctx is not None
    rb = row0 // seq
    ctx_len = ctx[0].shape[2] if latent else 0
    const = lambda b: (0, 0)
    in_specs = [
        pl.BlockSpec((seq, MLA_Q_RANK), lambda b: (rb + b, C_MCQ // MLA_Q_RANK)),
        pl.BlockSpec((seq, MLA_KV_RANK), lambda b: (rb + b, C_MCKV // MLA_KV_RANK)),
        pl.BlockSpec((seq, LANES), lambda b: (rb + b, C_SMALL // LANES)),
    ]
    args = [proj, proj, proj]
    if latent:
        in_specs += [
            pl.BlockSpec((None, None, ctx_len, MLA_KV_RANK), lambda b: (b, layer, 0, 0)),
            pl.BlockSpec((None, None, ctx_len, LANES), lambda b: (b, layer, 0, 0)),
            pl.BlockSpec((seq, LANES), const),
            pl.BlockSpec((seq, LANES), const),
        ]
        args += [ctx[0], ctx[1], lp["rope_c32"], lp["rope_s32"]]
    in_specs += [
        pl.BlockSpec((1, MLA_Q_RANK), const),
        pl.BlockSpec((1, MLA_KV_RANK), const),
        pl.BlockSpec((MLA_Q_RANK, 2 * MLA_HEADS * LANES), const),
        pl.BlockSpec((MLA_KV_RANK, MLA_HEADS * MLA_NOPE), const),
        pl.BlockSpec((MLA_KV_RANK, BRANCH_W), const),
    ]
    args += [lp["mla_q_norm"], lp["mla_kv_norm"], lp["w_uq"], lp["w_uk"], lp["w_uv"]]
    out_specs = [pl.BlockSpec((seq, BRANCH_W), lambda b: (b, 0))]
    out_shape = [jax.ShapeDtypeStruct((batch * seq, BRANCH_W), BF16)]
    if not latent:
        out_specs += [pl.BlockSpec((seq, MLA_KV_RANK), lambda b: (b, 0)),
                      pl.BlockSpec((seq, MLA_ROPE), lambda b: (b, 0))]
        out_shape += [jax.ShapeDtypeStruct((batch * seq, MLA_KV_RANK), F32),
                      jax.ShapeDtypeStruct((batch * seq, MLA_ROPE), F32)]
    lk = ctx_len + seq
    res = pl.pallas_call(
        functools.partial(_mla_kernel, seq=seq, ctx_len=ctx_len, latent=latent),
        grid=(batch,),
        in_specs=in_specs,
        out_specs=out_specs,
        out_shape=out_shape,
        scratch_shapes=[pltpu.VMEM((lk, MLA_HEADS * MLA_NOPE), BF16), pltpu.VMEM((lk, BRANCH_W), BF16),
                        pltpu.VMEM((lk, LANES), BF16)],
        compiler_params=_cparams(("parallel",)),
        name="mla_latent" if latent else "mla_context",
    )(*args)
    return res if not latent else res[0]


def _pack_pairs(lo, hi):
    lo_bits = lax.bitcast_convert_type(lo.astype(BF16).astype(F32), jnp.uint32)
    hi_bits = lax.bitcast_convert_type(hi.astype(BF16).astype(F32), jnp.uint32)
    return (lo_bits >> jnp.uint32(16)) | (hi_bits & jnp.uint32(0xFFFF0000))


def _unpack_pairs(w):
    lo = lax.bitcast_convert_type(w << jnp.uint32(16), F32)
    hi = lax.bitcast_convert_type(w & jnp.uint32(0xFFFF0000), F32)
    return lo, hi


def _merge_kernel(*refs, n_ctx_tiles):
    oc_refs, ol_refs, g_refs = refs[0:4], refs[4:8], refs[8:12]
    (wb_ref, wo_ref, x_ref, ga_ref, shf_ref, scf_ref, lg_ref, lb_ref, rw_ref, rb_ref,
     x1_ref, u2_ref, lo_ref, acc_sc) = refs[12:]
    i = pl.program_id(0)
    j = pl.program_id(1)

    @pl.when(j == 0)
    def _():
        acc_sc[...] = jnp.zeros(acc_sc.shape, F32)

    is_ctx = i < n_ctx_tiles
    merged = None
    for n in range(N_BRANCH):
        o = jnp.where(is_ctx, oc_refs[n][...], ol_refs[n][...])
        term = _sigmoid(g_refs[n][...]) * _dot(o, wb_ref[n])
        merged = term if merged is None else merged + term
    acc_sc[...] += _dot(merged.astype(BF16), wo_ref[...])

    @pl.when(j == pl.num_programs(1) - 1)
    def _():
        h = DEEPNORM_ALPHA * x_ref[...] + ga_ref[...] * acc_sc[...]
        x1 = _ln(h) * lg_ref[...] + lb_ref[...]
        x1_ref[...] = x1
        u2 = (_ln(x1) * (1.0 + scf_ref[...]) + shf_ref[...]).astype(BF16)
        half = D_MODEL // 2
        u2f = u2.astype(F32)
        u2_ref[...] = _pack_pairs(u2f[:, :half], u2f[:, half:])
        lo_ref[...] = _dot(u2, rw_ref[...]) + rb_ref[...]


def merge_and_route(outs_ctx, outs_lat, proj, x, mods, lp, layer, n_ctx_rows, lat_len):
    t = x.shape[0]
    tm, tn = TM_MERGE, TN_MERGE
    n_ctx_tiles = n_ctx_rows // tm
    row = functools.partial(_mod_row, tm=tm, n_ctx_rows=n_ctx_rows, lat_len=lat_len)
    gate_spec = lambda n: pl.BlockSpec((tm, tn), lambda i, j: (i, (C_GATE + n * D_MODEL) // tn + j))
    vec = lambda k: pl.BlockSpec((None, 1, D_MODEL), lambda i, j: (row(i), 0, k))
    cvec = pl.BlockSpec((None, 1, D_MODEL), lambda i, j: (layer, 0, 0))
    in_specs = (
        [pl.BlockSpec((tm, BRANCH_W), lambda i, j: (jnp.minimum(i, n_ctx_tiles - 1), 0))] * N_BRANCH
        + [pl.BlockSpec((tm, BRANCH_W), lambda i, j: (jnp.maximum(i - n_ctx_tiles, 0), 0))] * N_BRANCH
        + [gate_spec(n) for n in range(N_BRANCH)]
        + [
            pl.BlockSpec((None, N_BRANCH, BRANCH_W, tn), lambda i, j: (layer, 0, 0, j)),
            pl.BlockSpec((None, tn, D_MODEL), lambda i, j: (layer, j, 0)),
            pl.BlockSpec((tm, D_MODEL), lambda i, j: (i, 0)),
            vec(2), vec(3), vec(4), cvec, cvec,
            pl.BlockSpec((None, D_MODEL, LANES), lambda i, j: (layer, 0, 0)),
            pl.BlockSpec((None, 1, LANES), lambda i, j: (layer, 0, 0)),
        ]
    )
    return pl.pallas_call(
        functools.partial(_merge_kernel, n_ctx_tiles=n_ctx_tiles),
        grid=(t // tm, D_MODEL // tn),
        in_specs=in_specs,
        out_specs=[
            pl.BlockSpec((tm, D_MODEL), lambda i, j: (i, 0)),
            pl.BlockSpec((tm, D_MODEL // 2), lambda i, j: (i, 0)),
            pl.BlockSpec((tm, LANES), lambda i, j: (i, 0)),
        ],
        out_shape=[
            jax.ShapeDtypeStruct((t, D_MODEL), F32),
            jax.ShapeDtypeStruct((t, D_MODEL // 2), jnp.uint32),
            jax.ShapeDtypeStruct((t, LANES), F32),
        ],
        scratch_shapes=[pltpu.VMEM((tm, D_MODEL), F32)],
        compiler_params=_cparams(("parallel", "arbitrary")),
        name="merge_route",
    )(*outs_ctx, *outs_lat, proj, proj, proj, proj, lp["w_branch"], lp["w_out"], x, mods, mods, mods,
      lp["ln_mix_g"], lp["ln_mix_b"], lp["router_w"], lp["router_b"])


def _moe_up_kernel(e_ref, src_ref, valid_ref, first_ref, x_ref, wg_ref, wl_ref, bg_ref, bl_ref, h_ref,
                   wg_sc, wl_sc):
    blk = pl.program_id(0)

    @pl.when(first_ref[blk] == 1)
    def _():
        wg_sc[...] = wg_ref[...].astype(BF16)
        wl_sc[...] = wl_ref[...].astype(BF16)

    @pl.when(valid_ref[blk] == 1)
    def _():
        lo, hi = _unpack_pairs(x_ref[...])
        x = jnp.concatenate([lo, hi], axis=1).astype(BF16)
        g = jnp.minimum(_dot(x, wg_sc[...]) + bg_ref[...], SWIGLU_LIMIT)
        lin = jnp.clip(_dot(x, wl_sc[...]) + bl_ref[...], -SWIGLU_LIMIT, SWIGLU_LIMIT)
        h_ref[...] = ((lin + 1.0) * (g * _sigmoid(SWIGLU_ALPHA * g))).astype(BF16)


def _moe_down_kernel(e_ref, src_ref, valid_ref, first_ref, h_ref, wd_ref, bd_ref, y_ref, wd_sc):
    blk = pl.program_id(0)

    @pl.when(first_ref[blk] == 1)
    def _():
        wd_sc[...] = wd_ref[...].astype(BF16)

    @pl.when(valid_ref[blk] == 1)
    def _():
        y = _dot(h_ref[...], wd_sc[...]) + bd_ref[...]
        half = D_MODEL // 2
        y_ref[...] = _pack_pairs(y[:, :half], y[:, half:])


def moe_experts(xs, plan, w_gu, b_gu, w_down, b_down, layer):
    p = xs.shape[0]
    nb = p // TM_MOE
    tm = TM_MOE
    half = D_MODEL // 2
    hidden = pl.pallas_call(
        _moe_up_kernel,
        grid_spec=pltpu.PrefetchScalarGridSpec(
            num_scalar_prefetch=4,
            grid=(nb,),
            in_specs=[
                pl.BlockSpec((tm, half), lambda b, e, s, v, fi: (s[b], 0)),
                pl.BlockSpec((None, None, D_MODEL, EXPERT_FF), lambda b, e, s, v, fi: (layer, e[b], 0, 0)),
                pl.BlockSpec((None, None, D_MODEL, EXPERT_FF), lambda b, e, s, v, fi: (layer, e[b], 0, 1)),
                pl.BlockSpec((None, None, 1, EXPERT_FF), lambda b, e, s, v, fi: (layer, e[b], 0, 0)),
                pl.BlockSpec((None, None, 1, EXPERT_FF), lambda b, e, s, v, fi: (layer, e[b], 0, 1)),
            ],
            out_specs=pl.BlockSpec((tm, EXPERT_FF), lambda b, e, s, v, fi: (s[b], 0)),
            scratch_shapes=[pltpu.VMEM((D_MODEL, EXPERT_FF), BF16), pltpu.VMEM((D_MODEL, EXPERT_FF), BF16)],
        ),
        out_shape=jax.ShapeDtypeStruct((p, EXPERT_FF), BF16),
        compiler_params=_cparams(("arbitrary",)),
        name="moe_up",
    )(*plan, xs, w_gu, w_gu, b_gu, b_gu)
    return pl.pallas_call(
        _moe_down_kernel,
        grid_spec=pltpu.PrefetchScalarGridSpec(
            num_scalar_prefetch=4,
            grid=(nb,),
            in_specs=[
                pl.BlockSpec((tm, EXPERT_FF), lambda b, e, s, v, fi: (s[b], 0)),
                pl.BlockSpec((None, None, EXPERT_FF, D_MODEL), lambda b, e, s, v, fi: (layer, e[b], 0, 0)),
                pl.BlockSpec((None, None, 1, D_MODEL), lambda b, e, s, v, fi: (layer, e[b], 0, 0)),
            ],
            out_specs=pl.BlockSpec((tm, half), lambda b, e, s, v, fi: (s[b], 0)),
            scratch_shapes=[pltpu.VMEM((EXPERT_FF, D_MODEL), BF16)],
        ),
        out_shape=jax.ShapeDtypeStruct((p, half), jnp.uint32),
        compiler_params=_cparams(("arbitrary",)),
        name="moe_down",
    )(*plan, hidden, w_down, b_down)


def route(logits):
    t = logits.shape[0]
    a = t * TOP_K
    top_v, top_i = lax.top_k(logits, TOP_K)
    gate = jax.nn.softmax(top_v, axis=-1)
    flat_e = top_i.reshape(-1)
    onehot = (flat_e[:, None] == jnp.arange(N_EXPERTS, dtype=jnp.int32)[None, :]).astype(jnp.int32)
    csum = jnp.cumsum(onehot, axis=0)
    rank = jnp.take_along_axis(csum, flat_e[:, None], axis=1)[:, 0] - 1
    counts = csum[-1]
    padded = (counts + TM_MOE - 1) // TM_MOE * TM_MOE
    ends = jnp.cumsum(padded)
    slot = (ends - padded)[flat_e] + rank
    nb = a // TM_MOE + N_EXPERTS
    blk = jnp.arange(nb, dtype=jnp.int32)
    n_used = (ends[-1] // TM_MOE).astype(jnp.int32)
    src = jnp.minimum(blk, n_used - 1)
    blk_e = jnp.minimum(jnp.searchsorted(ends, src * TM_MOE, side="right"), N_EXPERTS - 1).astype(jnp.int32)
    valid = (blk < n_used).astype(jnp.int32)
    first = jnp.where((blk == 0) | (blk_e != jnp.roll(blk_e, 1)), 1, 0).astype(jnp.int32)
    key_bits = 16
    assert a < (1 << key_bits)
    skey = lax.sort(flat_e * (1 << key_bits) + jnp.arange(a, dtype=jnp.int32))
    sorted_t = (skey & ((1 << key_bits) - 1)) // TOP_K
    start = jnp.cumsum(counts) - counts
    pos = jnp.arange(nb * TM_MOE, dtype=jnp.int32)
    pos_e = jnp.repeat(blk_e, TM_MOE)
    off = pos - (ends - padded)[pos_e]
    live = (off < counts[pos_e]) & jnp.repeat(valid == 1, TM_MOE)
    tok = jnp.where(live, sorted_t[jnp.clip(start[pos_e] + off, 0, a - 1)], 0)
    return gate, slot.reshape(t, TOP_K), tok, (blk_e, src, valid, first)


def _ffn_norm_kernel(x_ref, y_ref, gate_ref, gf_ref, lg_ref, lb_ref, o_ref):
    gate = gate_ref[...]
    ffn_lo = ffn_hi = None
    for k in range(TOP_K):
        lo, hi = _unpack_pairs(y_ref[k])
        gk = gate[:, k:k + 1]
        ffn_lo = lo * gk if ffn_lo is None else ffn_lo + lo * gk
        ffn_hi = hi * gk if ffn_hi is None else ffn_hi + hi * gk
    ffn = jnp.concatenate([ffn_lo, ffn_hi], axis=1)
    h = DEEPNORM_ALPHA * x_ref[...] + gf_ref[...] * ffn
    o_ref[...] = _ln(h) * lg_ref[...] + lb_ref[...]


def ffn_norm(x1, y_rows, gate, mods, lp, layer, n_ctx_rows, lat_len):
    t = x1.shape[0]
    tm = TM_LN
    row = functools.partial(_mod_row, tm=tm, n_ctx_rows=n_ctx_rows, lat_len=lat_len)
    cvec = pl.BlockSpec((None, 1, D_MODEL), lambda i: (layer, 0, 0))
    return pl.pallas_call(
        _ffn_norm_kernel,
        grid=(t // tm,),
        in_specs=[
            pl.BlockSpec((tm, D_MODEL), lambda i: (i, 0)),
            pl.BlockSpec((TOP_K, tm, D_MODEL // 2), lambda i: (0, i, 0)),
            pl.BlockSpec((tm, TOP_K), lambda i: (i, 0)),
            pl.BlockSpec((None, 1, D_MODEL), lambda i: (row(i), 0, 5)),
            cvec, cvec,
        ],
        out_specs=pl.BlockSpec((tm, D_MODEL), lambda i: (i, 0)),
        out_shape=jax.ShapeDtypeStruct((t, D_MODEL), F32),
        compiler_params=_cparams(("parallel",)),
        name="ffn_norm",
    )(x1, y_rows, gate, mods, lp["ln_ffn_g"], lp["ln_ffn_b"])


def _rope_tables(n_tokens, seg, width):
    rows = n_tokens // GRID_W
    row = jnp.repeat(jnp.arange(rows, dtype=F32), GRID_W)
    col = jnp.tile(jnp.arange(GRID_W, dtype=F32), rows)
    n_freq = seg // 4
    inv = ROPE_THETA ** (-jnp.arange(n_freq, dtype=F32) / n_freq)
    ang = jnp.concatenate([row[:, None] * inv, col[:, None] * inv], axis=-1)
    cos, sin = jnp.cos(ang), jnp.sin(ang)
    reps = width // seg
    return (jnp.tile(jnp.concatenate([cos, cos], axis=-1), (1, reps)),
            jnp.tile(jnp.concatenate([-sin, sin], axis=-1), (1, reps)))


def _pack_w_in(w_in):
    pad = jnp.zeros(w_in.shape[:-1] + (C_GATE - C_SMALL - MLA_ROPE - 2 * SSD_HEADS,), w_in.dtype)
    return jnp.concatenate([
        w_in[..., BRANCH_W:_O_DT], w_in[..., :BRANCH_W], w_in[..., _O_DQ:_O_MKR], w_in[..., _O_GQ:_O_GATE],
        w_in[..., _O_MKR:_O_GQ], w_in[..., _O_DT:_O_DQ], pad, w_in[..., _O_GATE:],
    ], axis=-1).astype(BF16)


def _pack_w_uq(w_uq):
    w = w_uq.reshape(MLA_Q_RANK, MLA_HEADS, MLA_NOPE + MLA_ROPE)
    nope = w[:, :, :MLA_NOPE].reshape(MLA_Q_RANK, MLA_HEADS * MLA_NOPE)
    rope = jnp.pad(w[:, :, MLA_NOPE:], ((0, 0), (0, 0), (0, LANES - MLA_ROPE)))
    return jnp.concatenate([nope, rope.reshape(MLA_Q_RANK, MLA_HEADS * LANES)], axis=-1).astype(BF16)


def _small_row(v16):
    return jnp.zeros((1, LANES), F32).at[0, SMALL_DT:SMALL_DT + 2 * SSD_HEADS].set(v16.reshape(-1))


def kernel(x_prompt, x_sample, state_ssd, cache_diff_k, cache_diff_v, cache_mla_ckv, cache_mla_kr, cache_gqa_k, cache_gqa_v, c, c_ctx, w_mod, b_mod, w_in, ssd_conv_w, ssd_conv_b, ssd_dt_bias, ssd_a_log, ssd_d, ssd_norm, diff_lambda, diff_norm, mla_q_norm, mla_w_uq, mla_kv_norm, mla_w_uk, mla_w_uv, gqa_q_norm, gqa_k_norm, w_branch, w_out, ln_mix_g, ln_mix_b, router_w, router_b, exp_w_gu, exp_b_gu, exp_w_down, exp_b_down, ln_ffn_g, ln_ffn_b):
    nb_ctx, len_ctx, _ = x_prompt.shape
    nb_lat, len_lat, _ = x_sample.shape
    past = cache_diff_k.shape[2]
    n_ctx_rows = nb_ctx * len_ctx
    t = n_ctx_rows + nb_lat * len_lat

    cvec = jnp.zeros((SUBLANES, D_MODEL), F32).at[0].set(c_ctx).at[1:1 + nb_lat].set(c)
    mods_all = modulation(cvec, w_mod, b_mod)

    w_in_p = _pack_w_in(w_in)
    w_branch_b = w_branch.astype(BF16)
    w_out_b = w_out.astype(BF16)
    router_w_p = jnp.pad(router_w, ((0, 0), (0, 0), (0, LANES - N_EXPERTS))).astype(BF16)
    router_b_p = jnp.pad(router_b, ((0, 0), (0, LANES - N_EXPERTS))).reshape(DEPTH, 1, LANES)
    c32, s32 = _rope_tables(len_lat, 2 * (DIFF_D // 2), LANES)
    c64, s64 = _rope_tables(len_lat, GQA_HEAD_DIM, LANES)
    n_pair = SSD_HEADS // 2
    state_in = state_ssd.reshape(nb_lat, DEPTH, 2, n_pair, 2 * SSD_HEAD_DIM, SSD_STATE)
    diff_kc = cache_diff_k.reshape(nb_lat, DEPTH, past, BRANCH_W)
    diff_vc = cache_diff_v.reshape(nb_lat, DEPTH, past, BRANCH_W)
    mla_krc = jnp.pad(cache_mla_kr, ((0, 0), (0, 0), (0, 0), (0, LANES - MLA_ROPE)))
    gqa_kc = cache_gqa_k.reshape(nb_lat, DEPTH, past, GQA_KV_HEADS * GQA_HEAD_DIM)
    gqa_vc = cache_gqa_v.reshape(nb_lat, DEPTH, past, GQA_KV_HEADS * GQA_HEAD_DIM)
    vecs = dict(ln_mix_g=ln_mix_g.reshape(DEPTH, 1, D_MODEL), ln_mix_b=ln_mix_b.reshape(DEPTH, 1, D_MODEL),
                ln_ffn_g=ln_ffn_g.reshape(DEPTH, 1, D_MODEL), ln_ffn_b=ln_ffn_b.reshape(DEPTH, 1, D_MODEL),
                w_branch=w_branch_b, w_out=w_out_b, router_w=router_w_p, router_b=router_b_p)
    b_gu = exp_b_gu.reshape(DEPTH, N_EXPERTS, 1, 2 * EXPERT_FF)
    b_down = exp_b_down.reshape(DEPTH, N_EXPERTS, 1, D_MODEL)

    x = jnp.concatenate([x_prompt.reshape(n_ctx_rows, D_MODEL), x_sample.reshape(-1, D_MODEL)], axis=0)
    new_state, new_dk, new_dv, new_ckv, new_kr, new_gk, new_gv = [], [], [], [], [], [], []
    for l in range(DEPTH):
        lp = dict(vecs)
        lp.update(
            conv_w=jnp.pad(ssd_conv_w[l], ((0, SUBLANES - SSD_CONV), (0, 0))),
            conv_b=ssd_conv_b[l].reshape(1, SSD_XBC),
            dt_bias=_small_row(ssd_dt_bias[l]),
            a_row=_small_row(-jnp.exp(ssd_a_log[l])),
            d_row=jnp.repeat(ssd_d[l], SSD_HEAD_DIM).reshape(1, BRANCH_W),
            ssd_norm=ssd_norm[l].reshape(1, BRANCH_W),
            diff_lambda=diff_lambda[l],
            diff_norm=diff_norm[l].reshape(1, 2 * DIFF_D),
            mla_q_norm=mla_q_norm[l].reshape(1, MLA_Q_RANK),
            mla_kv_norm=mla_kv_norm[l].reshape(1, MLA_KV_RANK),
            w_uq=_pack_w_uq(mla_w_uq[l]),
            w_uk=mla_w_uk[l].astype(BF16),
            w_uv=mla_w_uv[l].astype(BF16),
            gqa_q_norm=gqa_q_norm[l].reshape(1, GQA_HEAD_DIM),
            gqa_k_norm=gqa_k_norm[l].reshape(1, GQA_HEAD_DIM),
            rope_c32=c32, rope_s32=s32, rope_c64=c64, rope_s64=s64,
        )
        mods = mods_all[l].reshape(SUBLANES, 1, 6 * D_MODEL)
        proj = in_projection(x, mods, w_in_p, l, n_ctx_rows, len_lat)

        o_ssd_c, st = ssd_branch(proj, 0, nb_ctx, len_ctx, lp, l)
        o_ssd_l = ssd_branch(proj, n_ctx_rows, nb_lat, len_lat, lp, l, state_in)
        o_diff_c, dk_c, dv_c = diff_branch(proj, 0, nb_ctx, len_ctx, lp, l)
        o_diff_l = diff_branch(proj, n_ctx_rows, nb_lat, len_lat, lp, l, (diff_kc, diff_vc))
        o_mla_c, ckv_c, kr_c = mla_branch(proj, 0, nb_ctx, len_ctx, lp, l)
        o_mla_l = mla_branch(proj, n_ctx_rows, nb_lat, len_lat, lp, l, (cache_mla_ckv, mla_krc))
        o_gqa_c, gk_c, gv_c = gqa_branch(proj, 0, nb_ctx, len_ctx, lp, l)
        o_gqa_l = gqa_branch(proj, n_ctx_rows, nb_lat, len_lat, lp, l, (gqa_kc, gqa_vc))
        outs_ctx = (o_ssd_c, o_diff_c, o_mla_c, o_gqa_c)
        outs_lat = (o_ssd_l, o_diff_l, o_mla_l, o_gqa_l)

        new_state.append(st.reshape(nb_ctx, 2, SSD_HEADS, SSD_HEAD_DIM, SSD_STATE))
        new_dk.append(dk_c.reshape(nb_ctx, len_ctx, DIFF_HEADS, 2 * DIFF_D))
        new_dv.append(dv_c.reshape(nb_ctx, len_ctx, DIFF_HEADS, 2 * DIFF_D))
        new_ckv.append(ckv_c.reshape(nb_ctx, len_ctx, MLA_KV_RANK))
        new_kr.append(kr_c.reshape(nb_ctx, len_ctx, MLA_ROPE))
        new_gk.append(gk_c.reshape(nb_ctx, len_ctx, GQA_KV_HEADS, GQA_HEAD_DIM))
        new_gv.append(gv_c.reshape(nb_ctx, len_ctx, GQA_KV_HEADS, GQA_HEAD_DIM))

        x1, u2, logits = merge_and_route(outs_ctx, outs_lat, proj, x, mods, lp, l, n_ctx_rows, len_lat)
        gate, slot, tok, plan = route(logits[:, :N_EXPERTS])
        yb = moe_experts(u2[tok], plan, exp_w_gu, b_gu, exp_w_down, b_down, l)
        y_rows = yb[slot.T.reshape(-1)].reshape(TOP_K, t, D_MODEL // 2)
        x = ffn_norm(x1, y_rows, gate, mods, lp, l, n_ctx_rows, len_lat)

    y_prompt = x[:n_ctx_rows].reshape(nb_ctx, len_ctx, D_MODEL)
    y_sample = x[n_ctx_rows:].reshape(nb_lat, len_lat, D_MODEL)
    stack = lambda parts: jnp.stack(parts, axis=1)
    return (y_prompt, y_sample, stack(new_state), stack(new_dk), stack(new_dv), stack(new_ckv),
            stack(new_kr), stack(new_gk), stack(new_gv))
```

```python
import functools
import math

import jax
import jax.numpy as jnp
from jax import lax
from jax.experimental import pallas as pl
from jax.experimental.pallas import tpu as pltpu

F32 = jnp.float32
BF16 = jnp.bfloat16

D_MODEL = 2048
DEPTH = 4
GRID_W = 64
ROPE_THETA = 10000.0
BRANCH_W = 512
N_BRANCH = 4
SSD_HEADS = 8
SSD_HEAD_DIM = 64
SSD_STATE = 128
SSD_CONV = 5
SSD_CHUNK = 128
SSD_XBC = 1024
DIFF_HEADS = 4
DIFF_D = 64
MLA_HEADS = 4
MLA_NOPE = 128
MLA_ROPE = 64
MLA_Q_RANK = 512
MLA_KV_RANK = 256
GQA_HEAD_DIM = 128
GQA_HEADS = 4
GQA_KV_HEADS = 2
N_EXPERTS = 32
TOP_K = 4
EXPERT_FF = 1024
SWIGLU_LIMIT = 7.0
SWIGLU_ALPHA = 1.702
DEEPNORM_ALPHA = (2.0 * DEPTH) ** 0.25
EPS = 1e-6

LANES = 128
SUBLANES = 8

_O_DT = 1536
_O_DQ = 1552
_O_MCKV = 3600
_O_MKR = 3856
_O_GQ = 3920
_O_GK = 4432
_O_GATE = 4944
IN_WIDTH = 13136
C_XBC = 0
C_Z = 1024
C_DQ = 1536
C_DK = 2048
C_DV = 2560
C_MCQ = 3072
C_GQ = 3584
C_MCKV = 4096
C_GK = 4352
C_GV = 4608
C_SMALL = 4864
C_GATE = 5120
PROJ_W = C_GATE + N_BRANCH * D_MODEL
SMALL_DT = MLA_ROPE

VMEM_LIMIT = 60000 * 1024

TM_IN = 1024
TN_IN = 1024
TM_MERGE = 512
TN_MERGE = 512
TM_LN = 512
TM_MOE = 512
TN_MOD = 1024
TR_PACK = 128
TQ = 256
SC_GATHER_MIN_ROWS = 16384


def _cparams(sem):
    return pltpu.CompilerParams(dimension_semantics=sem, vmem_limit_bytes=VMEM_LIMIT)


def _dot(a, b):
    return jnp.dot(a, b, preferred_element_type=F32)


def _dot_nt(a, b):
    return lax.dot_general(a, b, (((1,), (1,)), ((), ())), preferred_element_type=F32)


def _ln(x):
    mu = jnp.mean(x, axis=-1, keepdims=True)
    xc = x - mu
    var = jnp.mean(xc * xc, axis=-1, keepdims=True)
    return xc * lax.rsqrt(var + EPS)


def _rms(x, g):
    return x * lax.rsqrt(jnp.mean(x * x, axis=-1, keepdims=True) + EPS) * g


def _sigmoid(x):
    return 0.5 * jnp.tanh(0.5 * x) + 0.5


def _silu(x):
    return x * _sigmoid(x)


def _rope(x, c, s, half):
    w = x.shape[-1]
    lane = lax.broadcasted_iota(jnp.int32, x.shape, 1)
    lo = (lane & (2 * half - 1)) < half
    swapped = jnp.where(lo, pltpu.roll(x, w - half, 1), pltpu.roll(x, half, 1))
    return x * c + swapped * s


def _softmax_pv(q, k, v, scale):
    s = _dot_nt(q, k) * scale
    m = jnp.max(s, axis=-1, keepdims=True)
    p = jnp.exp(s - m)
    l = jnp.sum(p, axis=-1, keepdims=True)
    return _dot(p.astype(BF16), v) / l


def _mod_row(i, tm, n_ctx_rows, lat_len):
    r = i * tm
    return jnp.where(r < n_ctx_rows, 0, 1 + (r - n_ctx_rows) // lat_len)


def _mod_kernel(c_ref, w_ref, b_ref, o_ref):
    c = c_ref[...]
    o_ref[...] = _dot(_silu(c).astype(BF16), w_ref[...].astype(BF16)) + b_ref[...]


def modulation(cvec, w_mod, b_mod):
    n = w_mod.shape[-1]
    return pl.pallas_call(
        _mod_kernel,
        grid=(DEPTH, n // TN_MOD),
        in_specs=[
            pl.BlockSpec((SUBLANES, D_MODEL), lambda l, j: (0, 0)),
            pl.BlockSpec((None, D_MODEL, TN_MOD), lambda l, j: (l, 0, j)),
            pl.BlockSpec((None, 1, TN_MOD), lambda l, j: (l, 0, j)),
        ],
        out_specs=pl.BlockSpec((None, SUBLANES, TN_MOD), lambda l, j: (l, 0, j)),
        out_shape=jax.ShapeDtypeStruct((DEPTH, SUBLANES, n), F32),
        compiler_params=_cparams(("parallel", "parallel")),
        name="modulation",
    )(cvec, w_mod, b_mod.reshape(DEPTH, 1, n))


def _inproj_kernel(x_ref, sh_ref, sc_ref, w_ref, o_ref, u_sc):
    @pl.when(pl.program_id(1) == 0)
    def _():
        u = _ln(x_ref[...]) * (1.0 + sc_ref[...]) + sh_ref[...]
        u_sc[...] = u.astype(BF16)

    o_ref[...] = _dot(u_sc[...], w_ref[...])


def in_projection(x, mods, w_in_p, layer, n_ctx_rows, lat_len):
    t = x.shape[0]
    tm = TM_IN
    row = functools.partial(_mod_row, tm=tm, n_ctx_rows=n_ctx_rows, lat_len=lat_len)
    return pl.pallas_call(
        _inproj_kernel,
        grid=(t // tm, PROJ_W // TN_IN),
        in_specs=[
            pl.BlockSpec((tm, D_MODEL), lambda i, j: (i, 0)),
            pl.BlockSpec((None, 1, D_MODEL), lambda i, j: (row(i), 0, 0)),
            pl.BlockSpec((None, 1, D_MODEL), lambda i, j: (row(i), 0, 1)),
            pl.BlockSpec((None, D_MODEL, TN_IN), lambda i, j: (layer, 0, j)),
        ],
        out_specs=pl.BlockSpec((tm, TN_IN), lambda i, j: (i, j)),
        out_shape=jax.ShapeDtypeStruct((t, PROJ_W), F32),
        scratch_shapes=[pltpu.VMEM((tm, D_MODEL), BF16)],
        compiler_params=_cparams(("parallel", "arbitrary")),
        name="in_projection",
    )(x, mods, mods, w_in_p)


def _lane_col(a, lane):
    idx = lax.broadcasted_iota(jnp.int32, a.shape, 1)
    return jnp.sum(jnp.where(idx == lane, a, 0.0), axis=1, keepdims=True)


def _ssd_kernel(*refs, seq, latent):
    if latent:
        (z_ref, xbc_ref, small_ref, cw_ref, cb_ref, dtb_ref, arow_ref, drow_ref, nw_ref, h0_ref,
         o_ref, xpad_sc, xs_sc, y_sc, dt_sc, h_sc) = refs
    else:
        (z_ref, xbc_ref, small_ref, cw_ref, cb_ref, dtb_ref, arow_ref, drow_ref, nw_ref,
         o_ref, hout_ref, xpad_sc, xs_sc, y_sc, dt_sc, h_sc) = refs
    q = SSD_CHUNK
    nc = seq // q
    pad = SUBLANES
    ext = q + 2 * pad

    xpad_sc[0:pad, :] = jnp.zeros((pad, SSD_XBC), F32)
    xpad_sc[pad + seq:2 * pad + seq, :] = jnp.zeros((pad, SSD_XBC), F32)

    def copy_in(c, carry):
        r0 = pl.multiple_of(c * q, q)
        xpad_sc[pl.ds(pad + r0, q), :] = xbc_ref[pl.ds(r0, q), :]
        return carry

    lax.fori_loop(0, nc, copy_in, 0)

    def conv(c, carry):
        r0 = pl.multiple_of(c * q, q)
        for ct in range(SSD_XBC // LANES):
            cols = slice(ct * LANES, (ct + 1) * LANES)
            win = xpad_sc[pl.ds(r0, ext), cols]
            acc = cb_ref[:, cols] + win[pad:pad + q] * cw_ref[2:3, cols]
            for k in (0, 1, 3, 4):
                shifted = pltpu.roll(win, (2 - k) % ext, 0)[pad:pad + q]
                acc = acc + shifted * cw_ref[k:k + 1, cols]
            xs_sc[pl.ds(r0, q), cols] = _silu(acc)
        sd = small_ref[pl.ds(r0, q), :] + dtb_ref[...]
        dt_sc[pl.ds(r0, q), :] = jnp.maximum(sd, 0.0) + jnp.log(1.0 + jnp.exp(-jnp.abs(sd)))
        return carry

    lax.fori_loop(0, nc, conv, 0)

    if latent:
        h_sc[...] = h0_ref[...]
    else:
        h_sc[...] = jnp.zeros(h_sc.shape, F32)

    ri = lax.broadcasted_iota(jnp.int32, (q, q), 0)
    ci = lax.broadcasted_iota(jnp.int32, (q, q), 1)
    lane = lax.broadcasted_iota(jnp.int32, (q, LANES), 1)
    lane_lo = lane < SSD_HEAD_DIM
    row_lo = lax.broadcasted_iota(jnp.int32, (2 * SSD_HEAD_DIM, q), 0) < SSD_HEAD_DIM
    row_lo_n = lax.broadcasted_iota(jnp.int32, (2 * SSD_HEAD_DIM, SSD_STATE), 0) < SSD_HEAD_DIM

    def chunk(c, d):
        keep = (ci <= ri) if d == 0 else (ci >= ri)
        tri = jnp.where(keep, 1.0, 0.0)
        r0 = pl.multiple_of(c * q, q)
        xc = xs_sc[pl.ds(r0, q), 0:BRANCH_W]
        dtc = dt_sc[pl.ds(r0, q), :]
        ac = dtc * arow_ref[...]
        acum = jnp.dot(tri, ac, preferred_element_type=F32, precision=lax.Precision.HIGHEST)
        acum_t = acum.T
        dt_t = dtc.T
        tot = acum[q - 1:q, :] if d == 0 else acum[0:1, :]
        eac = jnp.exp(acum)
        w_t = (jnp.exp(tot - acum) * dtc).T
        etot = jnp.exp(tot)
        x_t = xc.T
        for g in range(2):
            bg = xs_sc[pl.ds(r0, q), BRANCH_W + g * SSD_STATE:BRANCH_W + (g + 1) * SSD_STATE].astype(BF16)
            cg = xs_sc[pl.ds(r0, q), BRANCH_W + 2 * SSD_STATE + g * SSD_STATE:
                       BRANCH_W + 2 * SSD_STATE + (g + 1) * SSD_STATE].astype(BF16)
            cb = _dot_nt(cg, bg)
            for pr in range(2):
                pair = g * 2 + pr
                h0 = pair * 2
                l0 = SMALL_DT + d * SSD_HEADS + h0
                atts = []
                for hh in range(2):
                    ln = l0 + hh
                    seg = _lane_col(acum, ln) - acum_t[ln:ln + 1, :]
                    dec = jnp.exp(jnp.where(keep, seg, -1e30))
                    atts.append(cb * dec * dt_t[ln:ln + 1, :])
                att = jnp.concatenate(atts, axis=1).astype(BF16)
                cols = slice(h0 * SSD_HEAD_DIM, (h0 + 2) * SSD_HEAD_DIM)
                xp = xc[:, cols]
                x2 = jnp.concatenate([jnp.where(lane_lo, xp, 0.0), jnp.where(lane_lo, 0.0, xp)],
                                     axis=0).astype(BF16)
                y_diag = _dot(att, x2)
                st_in = h_sc[d, pair]
                y_off = _dot_nt(cg, st_in.astype(BF16))
                esel = jnp.where(lane_lo, _lane_col(eac, l0), _lane_col(eac, l0 + 1))
                y = y_diag + y_off * esel
                if d == 0:
                    y_sc[pl.ds(r0, q), cols] = y
                else:
                    y_sc[pl.ds(r0, q), cols] = y_sc[pl.ds(r0, q), cols] + y
                wsel = jnp.where(row_lo, w_t[l0:l0 + 1, :], w_t[l0 + 1:l0 + 2, :])
                st_new = _dot((x_t[cols, :] * wsel).astype(BF16), bg)
                dsel = jnp.where(row_lo_n, _lane_col(etot, l0), _lane_col(etot, l0 + 1))
                h_sc[d, pair] = dsel * st_in + st_new

    def fwd(c, carry):
        chunk(c, 0)
        return carry

    def bwd(i, carry):
        chunk(nc - 1 - i, 1)
        return carry

    lax.fori_loop(0, nc, fwd, 0)
    lax.fori_loop(0, nc, bwd, 0)

    def finish(c, carry):
        r0 = pl.multiple_of(c * q, q)
        y = y_sc[pl.ds(r0, q), :] + drow_ref[...] * xs_sc[pl.ds(r0, q), 0:BRANCH_W]
        gated = y * _silu(z_ref[pl.ds(r0, q), :])
        o_ref[pl.ds(r0, q), :] = _rms(gated, nw_ref[...]).astype(BF16)
        return carry

    lax.fori_loop(0, nc, finish, 0)
    if not latent:
        hout_ref[...] = h_sc[...]


def ssd_branch(proj, row0, batch, seq, lp, layer, state=None):
    latent = state is not None
    rb = row0 // seq
    n_pair = SSD_HEADS // 2
    st_shape = (2, n_pair, 2 * SSD_HEAD_DIM, SSD_STATE)
    const = lambda b: (0, 0)
    in_specs = [
        pl.BlockSpec((seq, BRANCH_W), lambda b: (rb + b, C_Z // BRANCH_W)),
        pl.BlockSpec((seq, SSD_XBC), lambda b: (rb + b, C_XBC // SSD_XBC)),
        pl.BlockSpec((seq, LANES), lambda b: (rb + b, C_SMALL // LANES)),
        pl.BlockSpec((SUBLANES, SSD_XBC), const),
        pl.BlockSpec((1, SSD_XBC), const),
        pl.BlockSpec((1, LANES), const),
        pl.BlockSpec((1, LANES), const),
        pl.BlockSpec((1, BRANCH_W), const),
        pl.BlockSpec((1, BRANCH_W), const),
    ]
    args = [proj, proj, proj, lp["conv_w"], lp["conv_b"], lp["dt_bias"], lp["a_row"], lp["d_row"],
            lp["ssd_norm"]]
    out_specs = [pl.BlockSpec((seq, BRANCH_W), lambda b: (b, 0))]
    out_shape = [jax.ShapeDtypeStruct((batch * seq, BRANCH_W), BF16)]
    if latent:
        in_specs.append(pl.BlockSpec((None, None) + st_shape, lambda b: (b, layer, 0, 0, 0, 0)))
        args.append(state)
    else:
        out_specs.append(pl.BlockSpec((None,) + st_shape, lambda b: (b, 0, 0, 0, 0)))
        out_shape.append(jax.ShapeDtypeStruct((batch,) + st_shape, F32))
    res = pl.pallas_call(
        functools.partial(_ssd_kernel, seq=seq, latent=latent),
        grid=(batch,),
        in_specs=in_specs,
        out_specs=out_specs,
        out_shape=out_shape,
        scratch_shapes=[
            pltpu.VMEM((seq + 2 * SUBLANES, SSD_XBC), F32),
            pltpu.VMEM((seq, SSD_XBC), F32),
            pltpu.VMEM((seq, BRANCH_W), F32),
            pltpu.VMEM((seq, LANES), F32),
            pltpu.VMEM(st_shape, F32),
        ],
        compiler_params=_cparams(("parallel",)),
        name="ssd_latent" if latent else "ssd_context",
    )(*args)
    return res if not latent else res[0]


def _diff_kernel(*refs, seq, ctx_len, latent, lam_init):
    if latent:
        (q_ref, k_ref, v_ref, kc_ref, vc_ref, c_ref, s_ref, lam_ref, nw_ref, o_ref, k_sc, v_sc) = refs
    else:
        (q_ref, k_ref, v_ref, lam_ref, nw_ref, o_ref, kout_ref, vout_ref, k_sc, v_sc) = refs
    lk = ctx_len + seq
    half = DIFF_D // 2
    wid = 2 * DIFF_D
    lv = lam_ref[...]
    lam = (jnp.exp(jnp.sum(lv[0:1] * lv[1:2], axis=1, keepdims=True))
           - jnp.exp(jnp.sum(lv[2:3] * lv[3:4], axis=1, keepdims=True)) + lam_init)
    scale = DIFF_D ** -0.5
    lane_lo = lax.broadcasted_iota(jnp.int32, (TQ, LANES), 1) < DIFF_D
    if not latent:
        kout_ref[...] = k_ref[...]
        vout_ref[...] = v_ref[...]

    for h in range(DIFF_HEADS):
        hs = slice(h * wid, (h + 1) * wid)
        if latent:
            k_sc[0:ctx_len, hs] = kc_ref[:, hs].astype(BF16)
            v_sc[0:ctx_len, hs] = vc_ref[:, hs].astype(BF16)
            kn = _rope(k_ref[:, hs], c_ref[...], s_ref[...], half)
        else:
            kn = k_ref[:, hs]
        k_sc[ctx_len:lk, hs] = kn.astype(BF16)
        v_sc[ctx_len:lk, hs] = v_ref[:, hs].astype(BF16)

        def qtile(t, carry, hs=hs):
            q0 = pl.multiple_of(t * TQ, TQ)
            qv = q_ref[pl.ds(q0, TQ), hs]
            if latent:
                qv = _rope(qv, c_ref[pl.ds(q0, TQ), :], s_ref[pl.ds(q0, TQ), :], half)
            q1 = jnp.where(lane_lo, qv, 0.0).astype(BF16)
            q2 = jnp.where(lane_lo, 0.0, qv).astype(BF16)
            k = k_sc[:, hs]
            v = v_sc[:, hs]
            o = _softmax_pv(q1, k, v, scale) - lam * _softmax_pv(q2, k, v, scale)
            o_ref[pl.ds(q0, TQ), hs] = (_rms(o, nw_ref[...]) * (1.0 - lam_init)).astype(BF16)
            return carry

        lax.fori_loop(0, seq // TQ, qtile, 0)


def diff_branch(proj, row0, batch, seq, lp, layer, ctx=None):
    latent = ctx is not None
    rb = row0 // seq
    ctx_len = ctx[0].shape[2] if latent else 0
    lam_init = 0.8 - 0.6 * math.exp(-0.3 * layer)
    wid = 2 * DIFF_D
    bw = BRANCH_W
    in_specs = [
        pl.BlockSpec((seq, bw), lambda b: (rb + b, C_DQ // bw)),
        pl.BlockSpec((seq, bw), lambda b: (rb + b, C_DK // bw)),
        pl.BlockSpec((seq, bw), lambda b: (rb + b, C_DV // bw)),
    ]
    args = [proj, proj, proj]
    if latent:
        in_specs += [
            pl.BlockSpec((None, None, ctx_len, bw), lambda b: (b, layer, 0, 0)),
            pl.BlockSpec((None, None, ctx_len, bw), lambda b: (b, layer, 0, 0)),
            pl.BlockSpec((seq, wid), lambda b: (0, 0)),
            pl.BlockSpec((seq, wid), lambda b: (0, 0)),
        ]
        args += [ctx[0], ctx[1], lp["rope_c32"], lp["rope_s32"]]
    in_specs += [
        pl.BlockSpec((4, DIFF_D), lambda b: (0, 0)),
        pl.BlockSpec((1, wid), lambda b: (0, 0)),
    ]
    args += [lp["diff_lambda"], lp["diff_norm"]]
    out_specs = [pl.BlockSpec((seq, bw), lambda b: (b, 0))]
    out_shape = [jax.ShapeDtypeStruct((batch * seq, bw), BF16)]
    if not latent:
        out_specs += [pl.BlockSpec((seq, bw), lambda b: (b, 0))] * 2
        out_shape += [jax.ShapeDtypeStruct((batch * seq, bw), F32)] * 2
    res = pl.pallas_call(
        functools.partial(_diff_kernel, seq=seq, ctx_len=ctx_len, latent=latent, lam_init=lam_init),
        grid=(batch,),
        in_specs=in_specs,
        out_specs=out_specs,
        out_shape=out_shape,
        scratch_shapes=[pltpu.VMEM((ctx_len + seq, bw), BF16), pltpu.VMEM((ctx_len + seq, bw), BF16)],
        compiler_params=_cparams(("parallel",)),
        name="diff_latent" if latent else "diff_context",
    )(*args)
    return res if not latent else res[0]


def _gqa_kernel(*refs, seq, ctx_len, latent):
    if latent:
        (q_ref, k_ref, v_ref, kc_ref, vc_ref, c_ref, s_ref, qw_ref, kw_ref, o_ref, k_sc, v_sc) = refs
    else:
        (q_ref, k_ref, v_ref, qw_ref, kw_ref, o_ref, kout_ref, vout_ref, k_sc, v_sc) = refs
    lk = ctx_len + seq
    half = GQA_HEAD_DIM // 2
    hd = GQA_HEAD_DIM
    grp = GQA_HEADS // GQA_KV_HEADS
    scale = GQA_HEAD_DIM ** -0.5
    if not latent:
        vout_ref[...] = v_ref[...]

    for kvh in range(GQA_KV_HEADS):
        ks = slice(kvh * hd, (kvh + 1) * hd)
        kn = _rms(k_ref[:, ks], kw_ref[...])
        if latent:
            k_sc[0:ctx_len, ks] = kc_ref[:, ks].astype(BF16)
            v_sc[0:ctx_len, ks] = vc_ref[:, ks].astype(BF16)
            kn = _rope(kn, c_ref[...], s_ref[...], half)
        else:
            kout_ref[:, ks] = kn
        k_sc[ctx_len:lk, ks] = kn.astype(BF16)
        v_sc[ctx_len:lk, ks] = v_ref[:, ks].astype(BF16)

        def qtile(t, carry, kvh=kvh, ks=ks):
            q0 = pl.multiple_of(t * TQ, TQ)
            qs = []
            for g in range(grp):
                c0 = (kvh * grp + g) * hd
                qv = _rms(q_ref[pl.ds(q0, TQ), c0:c0 + hd], qw_ref[...])
                if latent:
                    qv = _rope(qv, c_ref[pl.ds(q0, TQ), :], s_ref[pl.ds(q0, TQ), :], half)
                qs.append(qv)
            qq = jnp.concatenate(qs, axis=0).astype(BF16)
            o = _softmax_pv(qq, k_sc[:, ks], v_sc[:, ks], scale)
            for g in range(grp):
                c0 = (kvh * grp + g) * hd
                o_ref[pl.ds(q0, TQ), c0:c0 + hd] = o[g * TQ:(g + 1) * TQ].astype(BF16)
            return carry

        lax.fori_loop(0, seq // TQ, qtile, 0)


def gqa_branch(proj, row0, batch, seq, lp, layer, ctx=None):
    latent = ctx is not None
    rb = row0 // seq
    ctx_len = ctx[0].shape[2] if latent else 0
    hd = GQA_HEAD_DIM
    qw = GQA_HEADS * hd
    kw = GQA_KV_HEADS * hd
    in_specs = [
        pl.BlockSpec((seq, qw), lambda b: (rb + b, C_GQ // qw)),
        pl.BlockSpec((seq, kw), lambda b: (rb + b, C_GK // kw)),
        pl.BlockSpec((seq, kw), lambda b: (rb + b, C_GV // kw)),
    ]
    args = [proj, proj, proj]
    if latent:
        in_specs += [
            pl.BlockSpec((None, None, ctx_len, kw), lambda b: (b, layer, 0, 0)),
            pl.BlockSpec((None, None, ctx_len, kw), lambda b: (b, layer, 0, 0)),
            pl.BlockSpec((seq, hd), lambda b: (0, 0)),
            pl.BlockSpec((seq, hd), lambda b: (0, 0)),
        ]
        args += [ctx[0], ctx[1], lp["rope_c64"], lp["rope_s64"]]
    in_specs += [pl.BlockSpec((1, hd), lambda b: (0, 0)), pl.BlockSpec((1, hd), lambda b: (0, 0))]
    args += [lp["gqa_q_norm"], lp["gqa_k_norm"]]
    out_specs = [pl.BlockSpec((seq, qw), lambda b: (b, 0))]
    out_shape = [jax.ShapeDtypeStruct((batch * seq, qw), BF16)]
    if not latent:
        out_specs += [pl.BlockSpec((seq, kw), lambda b: (b, 0))] * 2
        out_shape += [jax.ShapeDtypeStruct((batch * seq, kw), F32)] * 2
    res = pl.pallas_call(
        functools.partial(_gqa_kernel, seq=seq, ctx_len=ctx_len, latent=latent),
        grid=(batch,),
        in_specs=in_specs,
        out_specs=out_specs,
        out_shape=out_shape,
        scratch_shapes=[pltpu.VMEM((ctx_len + seq, kw), BF16), pltpu.VMEM((ctx_len + seq, kw), BF16)],
        compiler_params=_cparams(("parallel",)),
        name="gqa_latent" if latent else "gqa_context",
    )(*args)
    return res if not latent else res[0]


def _mla_kernel(*refs, seq, ctx_len, latent):
    if latent:
        (cq_ref, ckv_ref, small_ref, ckvc_ref, krc_ref, c_ref, s_ref, qw_ref, kvw_ref, wuq_ref, wuk_ref,
         wuv_ref, o_ref, kn_sc, v_sc, kr_sc) = refs
    else:
        (cq_ref, ckv_ref, small_ref, qw_ref, kvw_ref, wuq_ref, wuk_ref, wuv_ref,
         o_ref, ckv_out_ref, kr_out_ref, kn_sc, v_sc, kr_sc) = refs
    lk = ctx_len + seq
    half = MLA_ROPE // 2
    ckv = _rms(ckv_ref[...], kvw_ref[...])
    if latent:
        cc = ckvc_ref[...].astype(BF16)
        kn_sc[0:ctx_len, :] = _dot(cc, wuk_ref[...]).astype(BF16)
        v_sc[0:ctx_len, :] = _dot(cc, wuv_ref[...]).astype(BF16)
        kr_sc[0:ctx_len, :] = krc_ref[...].astype(BF16)
    else:
        ckv_out_ref[...] = ckv
        kr_out_ref[...] = small_ref[:, 0:MLA_ROPE]
    cb = ckv.astype(BF16)
    kn_sc[ctx_len:lk, :] = _dot(cb, wuk_ref[...]).astype(BF16)
    v_sc[ctx_len:lk, :] = _dot(cb, wuv_ref[...]).astype(BF16)
    kr = small_ref[...]
    if latent:
        kr = _rope(kr, c_ref[...], s_ref[...], half)
    lane_lo = lax.broadcasted_iota(jnp.int32, (seq, LANES), 1) < MLA_ROPE
    kr_sc[ctx_len:lk, :] = jnp.where(lane_lo, kr, 0.0).astype(BF16)
    scale = (MLA_NOPE + MLA_ROPE) ** -0.5
    nope_w = MLA_HEADS * MLA_NOPE

    def qtile(t, carry):
        q0 = pl.multiple_of(t * TQ, TQ)
        cq = _dot(_rms(cq_ref[pl.ds(q0, TQ), :], qw_ref[...]).astype(BF16), wuq_ref[...])
        for h in range(MLA_HEADS):
            hs = slice(h * MLA_NOPE, (h + 1) * MLA_NOPE)
            qn = cq[:, hs].astype(BF16)
            qr = cq[:, nope_w + h * LANES:nope_w + (h + 1) * LANES]
            if latent:
                qr = _rope(qr, c_ref[pl.ds(q0, TQ), :], s_ref[pl.ds(q0, TQ), :], half)
            s = (_dot_nt(qn, kn_sc[:, hs]) + _dot_nt(qr.astype(BF16), kr_sc[...])) * scale
            m = jnp.max(s, axis=-1, keepdims=True)
            p = jnp.exp(s - m)
            l = jnp.sum(p, axis=-1, keepdims=True)
            o = _dot(p.astype(BF16), v_sc[:, hs]) / l
            o_ref[pl.ds(q0, TQ), hs] = o.astype(BF16)
        return carry

    lax.fori_loop(0, seq // TQ, qtile, 0)


def mla_branch(proj, row0, batch, seq, lp, layer, ctx=None):
    latent = ctx is not None
    rb = row0 // seq
    ctx_len = ctx[0].shape[2] if latent else 0
    const = lambda b: (0, 0)
    in_specs = [
        pl.BlockSpec((seq, MLA_Q_RANK), lambda b: (rb + b, C_MCQ // MLA_Q_RANK)),
        pl.BlockSpec((seq, MLA_KV_RANK), lambda b: (rb + b, C_MCKV // MLA_KV_RANK)),
        pl.BlockSpec((seq, LANES), lambda b: (rb + b, C_SMALL // LANES)),
    ]
    args = [proj, proj, proj]
    if latent:
        in_specs += [
            pl.BlockSpec((None, None, ctx_len, MLA_KV_RANK), lambda b: (b, layer, 0, 0)),
            pl.BlockSpec((None, None, ctx_len, LANES), lambda b: (b, layer, 0, 0)),
            pl.BlockSpec((seq, LANES), const),
            pl.BlockSpec((seq, LANES), const),
        ]
        args += [ctx[0], ctx[1], lp["rope_c32"], lp["rope_s32"]]
    in_specs += [
        pl.BlockSpec((1, MLA_Q_RANK), const),
        pl.BlockSpec((1, MLA_KV_RANK), const),
        pl.BlockSpec((MLA_Q_RANK, 2 * MLA_HEADS * LANES), const),
        pl.BlockSpec((MLA_KV_RANK, MLA_HEADS * MLA_NOPE), const),
        pl.BlockSpec((MLA_KV_RANK, BRANCH_W), const),
    ]
    args += [lp["mla_q_norm"], lp["mla_kv_norm"], lp["w_uq"], lp["w_uk"], lp["w_uv"]]
    out_specs = [pl.BlockSpec((seq, BRANCH_W), lambda b: (b, 0))]
    out_shape = [jax.ShapeDtypeStruct((batch * seq, BRANCH_W), BF16)]
    if not latent:
        out_specs += [pl.BlockSpec((seq, MLA_KV_RANK), lambda b: (b, 0)),
                      pl.BlockSpec((seq, MLA_ROPE), lambda b: (b, 0))]
        out_shape += [jax.ShapeDtypeStruct((batch * seq, MLA_KV_RANK), F32),
                      jax.ShapeDtypeStruct((batch * seq, MLA_ROPE), F32)]
    lk = ctx_len + seq
    res = pl.pallas_call(
        functools.partial(_mla_kernel, seq=seq, ctx_len=ctx_len, latent=latent),
        grid=(batch,),
        in_specs=in_specs,
        out_specs=out_specs,
        out_shape=out_shape,
        scratch_shapes=[pltpu.VMEM((lk, MLA_HEADS * MLA_NOPE), BF16), pltpu.VMEM((lk, BRANCH_W), BF16),
                        pltpu.VMEM((lk, LANES), BF16)],
        compiler_params=_cparams(("parallel",)),
        name="mla_latent" if latent else "mla_context",
    )(*args)
    return res if not latent else res[0]


def _pack_pairs(lo, hi):
    lo_bits = lax.bitcast_convert_type(lo.astype(BF16).astype(F32), jnp.uint32)
    hi_bits = lax.bitcast_convert_type(hi.astype(BF16).astype(F32), jnp.uint32)
    return (lo_bits >> jnp.uint32(16)) | (hi_bits & jnp.uint32(0xFFFF0000))


def _unpack_pairs(w):
    lo = lax.bitcast_convert_type(w << jnp.uint32(16), F32)
    hi = lax.bitcast_convert_type(w & jnp.uint32(0xFFFF0000), F32)
    return lo, hi


def _merge_kernel(*refs, n_ctx_tiles):
    oc_refs, ol_refs, g_refs = refs[0:4], refs[4:8], refs[8:12]
    (wb_ref, wo_ref, x_ref, ga_ref, shf_ref, scf_ref, lg_ref, lb_ref, rw_ref, rb_ref,
     x1_ref, u2_ref, lo_ref, acc_sc) = refs[12:]
    i = pl.program_id(0)
    j = pl.program_id(1)

    @pl.when(j == 0)
    def _():
        acc_sc[...] = jnp.zeros(acc_sc.shape, F32)

    is_ctx = i < n_ctx_tiles
    merged = None
    for n in range(N_BRANCH):
        o = jnp.where(is_ctx, oc_refs[n][...], ol_refs[n][...])
        term = _sigmoid(g_refs[n][...]) * _dot(o, wb_ref[n])
        merged = term if merged is None else merged + term
    acc_sc[...] += _dot(merged.astype(BF16), wo_ref[...])

    @pl.when(j == pl.num_programs(1) - 1)
    def _():
        h = DEEPNORM_ALPHA * x_ref[...] + ga_ref[...] * acc_sc[...]
        x1 = _ln(h) * lg_ref[...] + lb_ref[...]
        x1_ref[...] = x1
        u2 = (_ln(x1) * (1.0 + scf_ref[...]) + shf_ref[...]).astype(BF16)
        half = D_MODEL // 2
        u2f = u2.astype(F32)
        u2_ref[...] = _pack_pairs(u2f[:, :half], u2f[:, half:])
        lo_ref[...] = _dot(u2, rw_ref[...]) + rb_ref[...]


def merge_and_route(outs_ctx, outs_lat, proj, x, mods, lp, layer, n_ctx_rows, lat_len):
    t = x.shape[0]
    tm, tn = TM_MERGE, TN_MERGE
    n_ctx_tiles = n_ctx_rows // tm
    row = functools.partial(_mod_row, tm=tm, n_ctx_rows=n_ctx_rows, lat_len=lat_len)
    gate_spec = lambda n: pl.BlockSpec((tm, tn), lambda i, j: (i, (C_GATE + n * D_MODEL) // tn + j))
    vec = lambda k: pl.BlockSpec((None, 1, D_MODEL), lambda i, j: (row(i), 0, k))
    cvec = pl.BlockSpec((None, 1, D_MODEL), lambda i, j: (layer, 0, 0))
    in_specs = (
        [pl.BlockSpec((tm, BRANCH_W), lambda i, j: (jnp.minimum(i, n_ctx_tiles - 1), 0))] * N_BRANCH
        + [pl.BlockSpec((tm, BRANCH_W), lambda i, j: (jnp.maximum(i - n_ctx_tiles, 0), 0))] * N_BRANCH
        + [gate_spec(n) for n in range(N_BRANCH)]
        + [
            pl.BlockSpec((None, N_BRANCH, BRANCH_W, tn), lambda i, j: (layer, 0, 0, j)),
            pl.BlockSpec((None, tn, D_MODEL), lambda i, j: (layer, j, 0)),
            pl.BlockSpec((tm, D_MODEL), lambda i, j: (i, 0)),
            vec(2), vec(3), vec(4), cvec, cvec,
            pl.BlockSpec((None, D_MODEL, LANES), lambda i, j: (layer, 0, 0)),
            pl.BlockSpec((None, 1, LANES), lambda i, j: (layer, 0, 0)),
        ]
    )
    return pl.pallas_call(
        functools.partial(_merge_kernel, n_ctx_tiles=n_ctx_tiles),
        grid=(t // tm, D_MODEL // tn),
        in_specs=in_specs,
        out_specs=[
            pl.BlockSpec((tm, D_MODEL), lambda i, j: (i, 0)),
            pl.BlockSpec((tm, D_MODEL // 2), lambda i, j: (i, 0)),
            pl.BlockSpec((tm, LANES), lambda i, j: (i, 0)),
        ],
        out_shape=[
            jax.ShapeDtypeStruct((t, D_MODEL), F32),
            jax.ShapeDtypeStruct((t, D_MODEL // 2), jnp.uint32),
            jax.ShapeDtypeStruct((t, LANES), F32),
        ],
        scratch_shapes=[pltpu.VMEM((tm, D_MODEL), F32)],
        compiler_params=_cparams(("parallel", "arbitrary")),
        name="merge_route",
    )(*outs_ctx, *outs_lat, proj, proj, proj, proj, lp["w_branch"], lp["w_out"], x, mods, mods, mods,
      lp["ln_mix_g"], lp["ln_mix_b"], lp["router_w"], lp["router_b"])


def _moe_up_kernel(e_ref, src_ref, valid_ref, first_ref, x_ref, wg_ref, wl_ref, bg_ref, bl_ref, h_ref,
                   wg_sc, wl_sc):
    blk = pl.program_id(0)

    @pl.when(first_ref[blk] == 1)
    def _():
        wg_sc[...] = wg_ref[...].astype(BF16)
        wl_sc[...] = wl_ref[...].astype(BF16)

    @pl.when(valid_ref[blk] == 1)
    def _():
        lo, hi = _unpack_pairs(x_ref[...])
        x = jnp.concatenate([lo, hi], axis=1).astype(BF16)
        g = jnp.minimum(_dot(x, wg_sc[...]) + bg_ref[...], SWIGLU_LIMIT)
        lin = jnp.clip(_dot(x, wl_sc[...]) + bl_ref[...], -SWIGLU_LIMIT, SWIGLU_LIMIT)
        h_ref[...] = ((lin + 1.0) * (g * _sigmoid(SWIGLU_ALPHA * g))).astype(BF16)


def _moe_down_kernel(e_ref, src_ref, valid_ref, first_ref, h_ref, wd_ref, bd_ref, y_ref, wd_sc):
    blk = pl.program_id(0)

    @pl.when(first_ref[blk] == 1)
    def _():
        wd_sc[...] = wd_ref[...].astype(BF16)

    @pl.when(valid_ref[blk] == 1)
    def _():
        y = _dot(h_ref[...], wd_sc[...]) + bd_ref[...]
        half = D_MODEL // 2
        y_ref[...] = _pack_pairs(y[:, :half], y[:, half:])


def moe_experts(xs, plan, w_gu, b_gu, w_down, b_down, layer):
    p = xs.shape[0]
    nb = p // TM_MOE
    tm = TM_MOE
    half = D_MODEL // 2
    hidden = pl.pallas_call(
        _moe_up_kernel,
        grid_spec=pltpu.PrefetchScalarGridSpec(
            num_scalar_prefetch=4,
            grid=(nb,),
            in_specs=[
                pl.BlockSpec((tm, half), lambda b, e, s, v, fi: (s[b], 0)),
                pl.BlockSpec((None, None, D_MODEL, EXPERT_FF), lambda b, e, s, v, fi: (layer, e[b], 0, 0)),
                pl.BlockSpec((None, None, D_MODEL, EXPERT_FF), lambda b, e, s, v, fi: (layer, e[b], 0, 1)),
                pl.BlockSpec((None, None, 1, EXPERT_FF), lambda b, e, s, v, fi: (layer, e[b], 0, 0)),
                pl.BlockSpec((None, None, 1, EXPERT_FF), lambda b, e, s, v, fi: (layer, e[b], 0, 1)),
            ],
            out_specs=pl.BlockSpec((tm, EXPERT_FF), lambda b, e, s, v, fi: (s[b], 0)),
            scratch_shapes=[pltpu.VMEM((D_MODEL, EXPERT_FF), BF16), pltpu.VMEM((D_MODEL, EXPERT_FF), BF16)],
        ),
        out_shape=jax.ShapeDtypeStruct((p, EXPERT_FF), BF16),
        compiler_params=_cparams(("arbitrary",)),
        name="moe_up",
    )(*plan, xs, w_gu, w_gu, b_gu, b_gu)
    return pl.pallas_call(
        _moe_down_kernel,
        grid_spec=pltpu.PrefetchScalarGridSpec(
            num_scalar_prefetch=4,
            grid=(nb,),
            in_specs=[
                pl.BlockSpec((tm, EXPERT_FF), lambda b, e, s, v, fi: (s[b], 0)),
                pl.BlockSpec((None, None, EXPERT_FF, D_MODEL), lambda b, e, s, v, fi: (layer, e[b], 0, 0)),
                pl.BlockSpec((None, None, 1, D_MODEL), lambda b, e, s, v, fi: (layer, e[b], 0, 0)),
            ],
            out_specs=pl.BlockSpec((tm, half), lambda b, e, s, v, fi: (s[b], 0)),
            scratch_shapes=[pltpu.VMEM((EXPERT_FF, D_MODEL), BF16)],
        ),
        out_shape=jax.ShapeDtypeStruct((p, half), jnp.uint32),
        compiler_params=_cparams(("arbitrary",)),
        name="moe_down",
    )(*plan, hidden, w_down, b_down)


def route(logits):
    t = logits.shape[0]
    a = t * TOP_K
    top_v, top_i = lax.top_k(logits, TOP_K)
    gate = jax.nn.softmax(top_v, axis=-1)
    flat_e = top_i.reshape(-1)
    onehot = (flat_e[:, None] == jnp.arange(N_EXPERTS, dtype=jnp.int32)[None, :]).astype(jnp.int32)
    csum = jnp.cumsum(onehot, axis=0)
    counts = csum[-1]
    padded = (counts + TM_MOE - 1) // TM_MOE * TM_MOE
    ends = jnp.cumsum(padded)
    slot = jnp.sum(onehot * (csum - 1 + (ends - padded)[None, :]), axis=1)
    nb = a // TM_MOE + N_EXPERTS
    blk = jnp.arange(nb, dtype=jnp.int32)
    n_used = (ends[-1] // TM_MOE).astype(jnp.int32)
    src = jnp.minimum(blk, n_used - 1)
    blk_e = jnp.minimum(jnp.searchsorted(ends, src * TM_MOE, side="right"), N_EXPERTS - 1).astype(jnp.int32)
    valid = (blk < n_used).astype(jnp.int32)
    first = jnp.where((blk == 0) | (blk_e != jnp.roll(blk_e, 1)), 1, 0).astype(jnp.int32)
    key_bits = 16
    assert a < (1 << key_bits)
    skey = lax.sort(flat_e * (1 << key_bits) + jnp.arange(a, dtype=jnp.int32))
    sorted_t = (skey & ((1 << key_bits) - 1)) // TOP_K
    start = jnp.cumsum(counts) - counts
    blk_off = blk * TM_MOE - (ends - padded)[blk_e]
    base = start[blk_e] + blk_off
    room = jnp.where(valid == 1, counts[blk_e] - blk_off, 0)
    r = jnp.arange(TM_MOE, dtype=jnp.int32)[None, :]
    live = r < room[:, None]
    tok = jnp.where(live, sorted_t[jnp.clip(base[:, None] + r, 0, a - 1)], 0).reshape(-1)
    tok = lax.optimization_barrier(tok)
    return gate, slot.reshape(t, TOP_K), tok, (blk_e, src, valid, first)


def _ffn_norm_kernel(x_ref, y_ref, gate_ref, gf_ref, lg_ref, lb_ref, o_ref):
    gate = gate_ref[...]
    ffn_lo = ffn_hi = None
    for k in range(TOP_K):
        lo, hi = _unpack_pairs(y_ref[k])
        gk = gate[:, k:k + 1]
        ffn_lo = lo * gk if ffn_lo is None else ffn_lo + lo * gk
        ffn_hi = hi * gk if ffn_hi is None else ffn_hi + hi * gk
    ffn = jnp.concatenate([ffn_lo, ffn_hi], axis=1)
    h = DEEPNORM_ALPHA * x_ref[...] + gf_ref[...] * ffn
    o_ref[...] = _ln(h) * lg_ref[...] + lb_ref[...]


def ffn_norm(x1, y_rows, gate, mods, lp, layer, n_ctx_rows, lat_len):
    t = x1.shape[0]
    tm = TM_LN
    row = functools.partial(_mod_row, tm=tm, n_ctx_rows=n_ctx_rows, lat_len=lat_len)
    cvec = pl.BlockSpec((None, 1, D_MODEL), lambda i: (layer, 0, 0))
    return pl.pallas_call(
        _ffn_norm_kernel,
        grid=(t // tm,),
        in_specs=[
            pl.BlockSpec((tm, D_MODEL), lambda i: (i, 0)),
            pl.BlockSpec((TOP_K, tm, D_MODEL // 2), lambda i: (0, i, 0)),
            pl.BlockSpec((tm, TOP_K), lambda i: (i, 0)),
            pl.BlockSpec((None, 1, D_MODEL), lambda i: (row(i), 0, 5)),
            cvec, cvec,
        ],
        out_specs=pl.BlockSpec((tm, D_MODEL), lambda i: (i, 0)),
        out_shape=jax.ShapeDtypeStruct((t, D_MODEL), F32),
        compiler_params=_cparams(("parallel",)),
        name="ffn_norm",
    )(x1, y_rows, gate, mods, lp["ln_ffn_g"], lp["ln_ffn_b"])


def _rope_tables(n_tokens, seg, width):
    rows = n_tokens // GRID_W
    row = jnp.repeat(jnp.arange(rows, dtype=F32), GRID_W)
    col = jnp.tile(jnp.arange(GRID_W, dtype=F32), rows)
    n_freq = seg // 4
    inv = ROPE_THETA ** (-jnp.arange(n_freq, dtype=F32) / n_freq)
    ang = jnp.concatenate([row[:, None] * inv, col[:, None] * inv], axis=-1)
    cos, sin = jnp.cos(ang), jnp.sin(ang)
    reps = width // seg
    return (jnp.tile(jnp.concatenate([cos, cos], axis=-1), (1, reps)),
            jnp.tile(jnp.concatenate([-sin, sin], axis=-1), (1, reps)))


def _pack_w_in_kernel(w_ref, o_ref):
    def put(dst, src, n):
        o_ref[:, dst:dst + n] = w_ref[:, src:src + n].astype(BF16)

    put(C_XBC, BRANCH_W, SSD_XBC)
    put(C_Z, 0, BRANCH_W)
    put(C_DQ, _O_DQ, _O_MCKV - _O_DQ)
    put(C_GQ, _O_GQ, _O_GK - _O_GQ)
    put(C_MCKV, _O_MCKV, MLA_KV_RANK)
    put(C_GK, _O_GK, _O_GATE - _O_GK)
    put(C_SMALL, _O_MKR, MLA_ROPE)
    put(C_SMALL + SMALL_DT, _O_DT, 2 * SSD_HEADS)
    pad0 = C_SMALL + SMALL_DT + 2 * SSD_HEADS
    o_ref[:, pad0:C_GATE] = jnp.zeros((o_ref.shape[0], C_GATE - pad0), BF16)
    put(C_GATE, _O_GATE, N_BRANCH * D_MODEL)


def _pack_w_in(w_in):
    tr = TR_PACK
    return pl.pallas_call(
        _pack_w_in_kernel,
        grid=(DEPTH, D_MODEL // tr),
        in_specs=[pl.BlockSpec((None, tr, IN_WIDTH), lambda l, r: (l, r, 0))],
        out_specs=pl.BlockSpec((None, tr, PROJ_W), lambda l, r: (l, r, 0)),
        out_shape=jax.ShapeDtypeStruct((DEPTH, D_MODEL, PROJ_W), BF16),
        compiler_params=_cparams(("parallel", "parallel")),
        name="pack_w_in",
    )(w_in)


def _pack_w_uq(w_uq):
    w = w_uq.reshape(MLA_Q_RANK, MLA_HEADS, MLA_NOPE + MLA_ROPE)
    nope = w[:, :, :MLA_NOPE].reshape(MLA_Q_RANK, MLA_HEADS * MLA_NOPE)
    rope = jnp.pad(w[:, :, MLA_NOPE:], ((0, 0), (0, 0), (0, LANES - MLA_ROPE)))
    return jnp.concatenate([nope, rope.reshape(MLA_Q_RANK, MLA_HEADS * LANES)], axis=-1).astype(BF16)


def _small_row(v16):
    return jnp.zeros((1, LANES), F32).at[0, SMALL_DT:SMALL_DT + 2 * SSD_HEADS].set(v16.reshape(-1))


def kernel(x_prompt, x_sample, state_ssd, cache_diff_k, cache_diff_v, cache_mla_ckv, cache_mla_kr, cache_gqa_k, cache_gqa_v, c, c_ctx, w_mod, b_mod, w_in, ssd_conv_w, ssd_conv_b, ssd_dt_bias, ssd_a_log, ssd_d, ssd_norm, diff_lambda, diff_norm, mla_q_norm, mla_w_uq, mla_kv_norm, mla_w_uk, mla_w_uv, gqa_q_norm, gqa_k_norm, w_branch, w_out, ln_mix_g, ln_mix_b, router_w, router_b, exp_w_gu, exp_b_gu, exp_w_down, exp_b_down, ln_ffn_g, ln_ffn_b):
    nb_ctx, len_ctx, _ = x_prompt.shape
    nb_lat, len_lat, _ = x_sample.shape
    past = cache_diff_k.shape[2]
    n_ctx_rows = nb_ctx * len_ctx
    t = n_ctx_rows + nb_lat * len_lat

    cvec = jnp.zeros((SUBLANES, D_MODEL), F32).at[0].set(c_ctx).at[1:1 + nb_lat].set(c)
    mods_all = modulation(cvec, w_mod, b_mod)

    w_in_p = _pack_w_in(w_in)
    w_branch_b = w_branch.astype(BF16)
    w_out_b = w_out.astype(BF16)
    router_w_p = jnp.pad(router_w, ((0, 0), (0, 0), (0, LANES - N_EXPERTS))).astype(BF16)
    router_b_p = jnp.pad(router_b, ((0, 0), (0, LANES - N_EXPERTS))).reshape(DEPTH, 1, LANES)
    c32, s32 = _rope_tables(len_lat, 2 * (DIFF_D // 2), LANES)
    c64, s64 = _rope_tables(len_lat, GQA_HEAD_DIM, LANES)
    n_pair = SSD_HEADS // 2
    state_in = state_ssd.reshape(nb_lat, DEPTH, 2, n_pair, 2 * SSD_HEAD_DIM, SSD_STATE)
    diff_kc = cache_diff_k.reshape(nb_lat, DEPTH, past, BRANCH_W)
    diff_vc = cache_diff_v.reshape(nb_lat, DEPTH, past, BRANCH_W)
    mla_krc = jnp.pad(cache_mla_kr, ((0, 0), (0, 0), (0, 0), (0, LANES - MLA_ROPE)))
    gqa_kc = cache_gqa_k.reshape(nb_lat, DEPTH, past, GQA_KV_HEADS * GQA_HEAD_DIM)
    gqa_vc = cache_gqa_v.reshape(nb_lat, DEPTH, past, GQA_KV_HEADS * GQA_HEAD_DIM)
    vecs = dict(ln_mix_g=ln_mix_g.reshape(DEPTH, 1, D_MODEL), ln_mix_b=ln_mix_b.reshape(DEPTH, 1, D_MODEL),
                ln_ffn_g=ln_ffn_g.reshape(DEPTH, 1, D_MODEL), ln_ffn_b=ln_ffn_b.reshape(DEPTH, 1, D_MODEL),
                w_branch=w_branch_b, w_out=w_out_b, router_w=router_w_p, router_b=router_b_p)
    b_gu = exp_b_gu.reshape(DEPTH, N_EXPERTS, 1, 2 * EXPERT_FF)
    b_down = exp_b_down.reshape(DEPTH, N_EXPERTS, 1, D_MODEL)

    x = jnp.concatenate([x_prompt.reshape(n_ctx_rows, D_MODEL), x_sample.reshape(-1, D_MODEL)], axis=0)
    new_state, new_dk, new_dv, new_ckv, new_kr, new_gk, new_gv = [], [], [], [], [], [], []
    for l in range(DEPTH):
        lp = dict(vecs)
        lp.update(
            conv_w=jnp.pad(ssd_conv_w[l], ((0, SUBLANES - SSD_CONV), (0, 0))),
            conv_b=ssd_conv_b[l].reshape(1, SSD_XBC),
            dt_bias=_small_row(ssd_dt_bias[l]),
            a_row=_small_row(-jnp.exp(ssd_a_log[l])),
            d_row=jnp.repeat(ssd_d[l], SSD_HEAD_DIM).reshape(1, BRANCH_W),
            ssd_norm=ssd_norm[l].reshape(1, BRANCH_W),
            diff_lambda=diff_lambda[l],
            diff_norm=diff_norm[l].reshape(1, 2 * DIFF_D),
            mla_q_norm=mla_q_norm[l].reshape(1, MLA_Q_RANK),
            mla_kv_norm=mla_kv_norm[l].reshape(1, MLA_KV_RANK),
            w_uq=_pack_w_uq(mla_w_uq[l]),
            w_uk=mla_w_uk[l].astype(BF16),
            w_uv=mla_w_uv[l].astype(BF16),
            gqa_q_norm=gqa_q_norm[l].reshape(1, GQA_HEAD_DIM),
            gqa_k_norm=gqa_k_norm[l].reshape(1, GQA_HEAD_DIM),
            rope_c32=c32, rope_s32=s32, rope_c64=c64, rope_s64=s64,
        )
        mods = mods_all[l].reshape(SUBLANES, 1, 6 * D_MODEL)
        proj = in_projection(x, mods, w_in_p, l, n_ctx_rows, len_lat)

        o_ssd_c, st = ssd_branch(proj, 0, nb_ctx, len_ctx, lp, l)
        o_ssd_l = ssd_branch(proj, n_ctx_rows, nb_lat, len_lat, lp, l, state_in)
        o_diff_c, dk_c, dv_c = diff_branch(proj, 0, nb_ctx, len_ctx, lp, l)
        o_diff_l = diff_branch(proj, n_ctx_rows, nb_lat, len_lat, lp, l, (diff_kc, diff_vc))
        o_mla_c, ckv_c, kr_c = mla_branch(proj, 0, nb_ctx, len_ctx, lp, l)
        o_mla_l = mla_branch(proj, n_ctx_rows, nb_lat, len_lat, lp, l, (cache_mla_ckv, mla_krc))
        o_gqa_c, gk_c, gv_c = gqa_branch(proj, 0, nb_ctx, len_ctx, lp, l)
        o_gqa_l = gqa_branch(proj, n_ctx_rows, nb_lat, len_lat, lp, l, (gqa_kc, gqa_vc))
        outs_ctx = (o_ssd_c, o_diff_c, o_mla_c, o_gqa_c)
        outs_lat = (o_ssd_l, o_diff_l, o_mla_l, o_gqa_l)

        new_state.append(st.reshape(nb_ctx, 2, SSD_HEADS, SSD_HEAD_DIM, SSD_STATE))
        new_dk.append(dk_c.reshape(nb_ctx, len_ctx, DIFF_HEADS, 2 * DIFF_D))
        new_dv.append(dv_c.reshape(nb_ctx, len_ctx, DIFF_HEADS, 2 * DIFF_D))
        new_ckv.append(ckv_c.reshape(nb_ctx, len_ctx, MLA_KV_RANK))
        new_kr.append(kr_c.reshape(nb_ctx, len_ctx, MLA_ROPE))
        new_gk.append(gk_c.reshape(nb_ctx, len_ctx, GQA_KV_HEADS, GQA_HEAD_DIM))
        new_gv.append(gv_c.reshape(nb_ctx, len_ctx, GQA_KV_HEADS, GQA_HEAD_DIM))

        x1, u2, logits = merge_and_route(outs_ctx, outs_lat, proj, x, mods, lp, l, n_ctx_rows, len_lat)
        gate, slot, tok, plan = route(logits[:, :N_EXPERTS])
        u2 = jnp.pad(u2, ((0, max(0, SC_GATHER_MIN_ROWS - t)), (0, 0)))
        yb = moe_experts(u2[tok], plan, exp_w_gu, b_gu, exp_w_down, b_down, l)
        y_rows = yb[slot.T.reshape(-1)].reshape(TOP_K, t, D_MODEL // 2)
        x = ffn_norm(x1, y_rows, gate, mods, lp, l, n_ctx_rows, len_lat)

    y_prompt = x[:n_ctx_rows].reshape(nb_ctx, len_ctx, D_MODEL)
    y_sample = x[n_ctx_rows:].reshape(nb_lat, len_lat, D_MODEL)
    stack = lambda parts: jnp.stack(parts, axis=1)
    return (y_prompt, y_sample, stack(new_state), stack(new_dk), stack(new_dv), stack(new_ckv),
            stack(new_kr), stack(new_gk), stack(new_gv))
```

```python
import functools
import math

import jax
import jax.numpy as jnp
from jax import lax
from jax.experimental import pallas as pl
from jax.experimental.pallas import tpu as pltpu

F32 = jnp.float32
BF16 = jnp.bfloat16

D_MODEL = 2048
DEPTH = 4
GRID_W = 64
ROPE_THETA = 10000.0
BRANCH_W = 512
N_BRANCH = 4
SSD_HEADS = 8
SSD_HEAD_DIM = 64
SSD_STATE = 128
SSD_CONV = 5
SSD_CHUNK = 128
SSD_XBC = 1024
DIFF_HEADS = 4
DIFF_D = 64
MLA_HEADS = 4
MLA_NOPE = 128
MLA_ROPE = 64
MLA_Q_RANK = 512
MLA_KV_RANK = 256
GQA_HEAD_DIM = 128
GQA_HEADS = 4
GQA_KV_HEADS = 2
N_EXPERTS = 32
TOP_K = 4
EXPERT_FF = 1024
SWIGLU_LIMIT = 7.0
SWIGLU_ALPHA = 1.702
DEEPNORM_ALPHA = (2.0 * DEPTH) ** 0.25
EPS = 1e-6

LANES = 128
SUBLANES = 8

_O_DT = 1536
_O_DQ = 1552
_O_MCKV = 3600
_O_MKR = 3856
_O_GQ = 3920
_O_GK = 4432
_O_GATE = 4944
IN_WIDTH = 13136
C_XBC = 0
C_Z = 1024
C_DQ = 1536
C_DK = 2048
C_DV = 2560
C_MCQ = 3072
C_GQ = 3584
C_MCKV = 4096
C_GK = 4352
C_GV = 4608
C_SMALL = 4864
C_GATE = 5120
PROJ_W = C_GATE + N_BRANCH * D_MODEL
SMALL_DT = MLA_ROPE

VMEM_LIMIT = 60000 * 1024

TM_IN = 1024
TN_IN = 1024
TM_MERGE = 512
TN_MERGE = 512
TM_LN = 512
TM_MOE = 512
TN_MOD = 1024
TR_PACK = 128
TQ = 256
SC_GATHER_MIN_ROWS = 16384


def _cparams(sem):
    return pltpu.CompilerParams(dimension_semantics=sem, vmem_limit_bytes=VMEM_LIMIT)


def _dot(a, b):
    return jnp.dot(a, b, preferred_element_type=F32)


def _dot_nt(a, b):
    return lax.dot_general(a, b, (((1,), (1,)), ((), ())), preferred_element_type=F32)


def _ln(x):
    mu = jnp.mean(x, axis=-1, keepdims=True)
    xc = x - mu
    var = jnp.mean(xc * xc, axis=-1, keepdims=True)
    return xc * lax.rsqrt(var + EPS)


def _rms(x, g):
    return x * lax.rsqrt(jnp.mean(x * x, axis=-1, keepdims=True) + EPS) * g


def _sigmoid(x):
    return 0.5 * jnp.tanh(0.5 * x) + 0.5


def _silu(x):
    return x * _sigmoid(x)


def _rope(x, c, s, half):
    w = x.shape[-1]
    lane = lax.broadcasted_iota(jnp.int32, x.shape, 1)
    lo = (lane & (2 * half - 1)) < half
    swapped = jnp.where(lo, pltpu.roll(x, w - half, 1), pltpu.roll(x, half, 1))
    return x * c + swapped * s


def _softmax_pv(q, k, v, scale):
    s = _dot_nt(q, k) * scale
    m = jnp.max(s, axis=-1, keepdims=True)
    p = jnp.exp(s - m)
    l = jnp.sum(p, axis=-1, keepdims=True)
    return _dot(p.astype(BF16), v) / l


def _mod_row(i, tm, n_ctx_rows, lat_len):
    r = i * tm
    return jnp.where(r < n_ctx_rows, 0, 1 + (r - n_ctx_rows) // lat_len)


def _mod_kernel(c_ref, w_ref, b_ref, o_ref):
    c = c_ref[...]
    o_ref[...] = _dot(_silu(c).astype(BF16), w_ref[...].astype(BF16)) + b_ref[...]


def modulation(cvec, w_mod, b_mod):
    n = w_mod.shape[-1]
    return pl.pallas_call(
        _mod_kernel,
        grid=(DEPTH, n // TN_MOD),
        in_specs=[
            pl.BlockSpec((SUBLANES, D_MODEL), lambda l, j: (0, 0)),
            pl.BlockSpec((None, D_MODEL, TN_MOD), lambda l, j: (l, 0, j)),
            pl.BlockSpec((None, 1, TN_MOD), lambda l, j: (l, 0, j)),
        ],
        out_specs=pl.BlockSpec((None, SUBLANES, TN_MOD), lambda l, j: (l, 0, j)),
        out_shape=jax.ShapeDtypeStruct((DEPTH, SUBLANES, n), F32),
        compiler_params=_cparams(("parallel", "parallel")),
        name="modulation",
    )(cvec, w_mod, b_mod.reshape(DEPTH, 1, n))


def _inproj_kernel(x_ref, sh_ref, sc_ref, w_ref, o_ref, u_sc):
    @pl.when(pl.program_id(1) == 0)
    def _():
        u = _ln(x_ref[...]) * (1.0 + sc_ref[...]) + sh_ref[...]
        u_sc[...] = u.astype(BF16)

    o_ref[...] = _dot(u_sc[...], w_ref[...])


def in_projection(x, mods, w_in_p, layer, n_ctx_rows, lat_len):
    t = x.shape[0]
    tm = TM_IN
    row = functools.partial(_mod_row, tm=tm, n_ctx_rows=n_ctx_rows, lat_len=lat_len)
    return pl.pallas_call(
        _inproj_kernel,
        grid=(t // tm, PROJ_W // TN_IN),
        in_specs=[
            pl.BlockSpec((tm, D_MODEL), lambda i, j: (i, 0)),
            pl.BlockSpec((None, 1, D_MODEL), lambda i, j: (row(i), 0, 0)),
            pl.BlockSpec((None, 1, D_MODEL), lambda i, j: (row(i), 0, 1)),
            pl.BlockSpec((None, D_MODEL, TN_IN), lambda i, j: (layer, 0, j)),
        ],
        out_specs=pl.BlockSpec((tm, TN_IN), lambda i, j: (i, j)),
        out_shape=jax.ShapeDtypeStruct((t, PROJ_W), F32),
        scratch_shapes=[pltpu.VMEM((tm, D_MODEL), BF16)],
        compiler_params=_cparams(("parallel", "arbitrary")),
        name="in_projection",
    )(x, mods, mods, w_in_p)


def _lane_col(a, lane):
    idx = lax.broadcasted_iota(jnp.int32, a.shape, 1)
    return jnp.sum(jnp.where(idx == lane, a, 0.0), axis=1, keepdims=True)


def _ssd_kernel(*refs, seq, latent):
    if latent:
        (z_ref, xbc_ref, small_ref, cw_ref, cb_ref, dtb_ref, arow_ref, drow_ref, nw_ref, h0_ref,
         o_ref, xpad_sc, xs_sc, y_sc, dt_sc, h_sc) = refs
    else:
        (z_ref, xbc_ref, small_ref, cw_ref, cb_ref, dtb_ref, arow_ref, drow_ref, nw_ref,
         o_ref, hout_ref, xpad_sc, xs_sc, y_sc, dt_sc, h_sc) = refs
    q = SSD_CHUNK
    nc = seq // q
    pad = SUBLANES
    ext = q + 2 * pad

    xpad_sc[0:pad, :] = jnp.zeros((pad, SSD_XBC), F32)
    xpad_sc[pad + seq:2 * pad + seq, :] = jnp.zeros((pad, SSD_XBC), F32)

    def copy_in(c, carry):
        r0 = pl.multiple_of(c * q, q)
        xpad_sc[pl.ds(pad + r0, q), :] = xbc_ref[pl.ds(r0, q), :]
        return carry

    lax.fori_loop(0, nc, copy_in, 0)

    def conv(c, carry):
        r0 = pl.multiple_of(c * q, q)
        for ct in range(SSD_XBC // LANES):
            cols = slice(ct * LANES, (ct + 1) * LANES)
            win = xpad_sc[pl.ds(r0, ext), cols]
            acc = cb_ref[:, cols] + win[pad:pad + q] * cw_ref[2:3, cols]
            for k in (0, 1, 3, 4):
                shifted = pltpu.roll(win, (2 - k) % ext, 0)[pad:pad + q]
                acc = acc + shifted * cw_ref[k:k + 1, cols]
            xs_sc[pl.ds(r0, q), cols] = _silu(acc)
        sd = small_ref[pl.ds(r0, q), :] + dtb_ref[...]
        dt_sc[pl.ds(r0, q), :] = jnp.maximum(sd, 0.0) + jnp.log(1.0 + jnp.exp(-jnp.abs(sd)))
        return carry

    lax.fori_loop(0, nc, conv, 0)

    if latent:
        h_sc[...] = h0_ref[...]
    else:
        h_sc[...] = jnp.zeros(h_sc.shape, F32)

    ri = lax.broadcasted_iota(jnp.int32, (q, q), 0)
    ci = lax.broadcasted_iota(jnp.int32, (q, q), 1)
    lane = lax.broadcasted_iota(jnp.int32, (q, LANES), 1)
    lane_lo = lane < SSD_HEAD_DIM
    row_lo = lax.broadcasted_iota(jnp.int32, (2 * SSD_HEAD_DIM, q), 0) < SSD_HEAD_DIM
    row_lo_n = lax.broadcasted_iota(jnp.int32, (2 * SSD_HEAD_DIM, SSD_STATE), 0) < SSD_HEAD_DIM

    def chunk(c, d):
        keep = (ci <= ri) if d == 0 else (ci >= ri)
        tri = jnp.where(keep, 1.0, 0.0)
        r0 = pl.multiple_of(c * q, q)
        xc = xs_sc[pl.ds(r0, q), 0:BRANCH_W]
        dtc = dt_sc[pl.ds(r0, q), :]
        ac = dtc * arow_ref[...]
        acum = jnp.dot(tri, ac, preferred_element_type=F32, precision=lax.Precision.HIGHEST)
        acum_t = acum.T
        dt_t = dtc.T
        tot = acum[q - 1:q, :] if d == 0 else acum[0:1, :]
        eac = jnp.exp(acum)
        w_t = (jnp.exp(tot - acum) * dtc).T
        etot = jnp.exp(tot)
        x_t = xc.T
        for g in range(2):
            bg = xs_sc[pl.ds(r0, q), BRANCH_W + g * SSD_STATE:BRANCH_W + (g + 1) * SSD_STATE].astype(BF16)
            cg = xs_sc[pl.ds(r0, q), BRANCH_W + 2 * SSD_STATE + g * SSD_STATE:
                       BRANCH_W + 2 * SSD_STATE + (g + 1) * SSD_STATE].astype(BF16)
            cb = _dot_nt(cg, bg)
            for pr in range(2):
                pair = g * 2 + pr
                h0 = pair * 2
                l0 = SMALL_DT + d * SSD_HEADS + h0
                atts = []
                for hh in range(2):
                    ln = l0 + hh
                    seg = _lane_col(acum, ln) - acum_t[ln:ln + 1, :]
                    dec = jnp.exp(jnp.where(keep, seg, -1e30))
                    atts.append(cb * dec * dt_t[ln:ln + 1, :])
                att = jnp.concatenate(atts, axis=1).astype(BF16)
                cols = slice(h0 * SSD_HEAD_DIM, (h0 + 2) * SSD_HEAD_DIM)
                xp = xc[:, cols]
                x2 = jnp.concatenate([jnp.where(lane_lo, xp, 0.0), jnp.where(lane_lo, 0.0, xp)],
                                     axis=0).astype(BF16)
                y_diag = _dot(att, x2)
                st_in = h_sc[d, pair]
                y_off = _dot_nt(cg, st_in.astype(BF16))
                esel = jnp.where(lane_lo, _lane_col(eac, l0), _lane_col(eac, l0 + 1))
                y = y_diag + y_off * esel
                if d == 0:
                    y_sc[pl.ds(r0, q), cols] = y
                else:
                    y_sc[pl.ds(r0, q), cols] = y_sc[pl.ds(r0, q), cols] + y
                wsel = jnp.where(row_lo, w_t[l0:l0 + 1, :], w_t[l0 + 1:l0 + 2, :])
                st_new = _dot((x_t[cols, :] * wsel).astype(BF16), bg)
                dsel = jnp.where(row_lo_n, _lane_col(etot, l0), _lane_col(etot, l0 + 1))
                h_sc[d, pair] = dsel * st_in + st_new

    def fwd(c, carry):
        chunk(c, 0)
        return carry

    def bwd(i, carry):
        chunk(nc - 1 - i, 1)
        return carry

    lax.fori_loop(0, nc, fwd, 0)
    lax.fori_loop(0, nc, bwd, 0)

    def finish(c, carry):
        r0 = pl.multiple_of(c * q, q)
        y = y_sc[pl.ds(r0, q), :] + drow_ref[...] * xs_sc[pl.ds(r0, q), 0:BRANCH_W]
        gated = y * _silu(z_ref[pl.ds(r0, q), :])
        o_ref[pl.ds(r0, q), :] = _rms(gated, nw_ref[...]).astype(BF16)
        return carry

    lax.fori_loop(0, nc, finish, 0)
    if not latent:
        hout_ref[...] = h_sc[...]


def ssd_branch(proj, row0, batch, seq, lp, layer, state=None):
    latent = state is not None
    rb = row0 // seq
    n_pair = SSD_HEADS // 2
    st_shape = (2, n_pair, 2 * SSD_HEAD_DIM, SSD_STATE)
    const = lambda b: (0, 0)
    in_specs = [
        pl.BlockSpec((seq, BRANCH_W), lambda b: (rb + b, C_Z // BRANCH_W)),
        pl.BlockSpec((seq, SSD_XBC), lambda b: (rb + b, C_XBC // SSD_XBC)),
        pl.BlockSpec((seq, LANES), lambda b: (rb + b, C_SMALL // LANES)),
        pl.BlockSpec((SUBLANES, SSD_XBC), const),
        pl.BlockSpec((1, SSD_XBC), const),
        pl.BlockSpec((1, LANES), const),
        pl.BlockSpec((1, LANES), const),
        pl.BlockSpec((1, BRANCH_W), const),
        pl.BlockSpec((1, BRANCH_W), const),
    ]
    args = [proj, proj, proj, lp["conv_w"], lp["conv_b"], lp["dt_bias"], lp["a_row"], lp["d_row"],
            lp["ssd_norm"]]
    out_specs = [pl.BlockSpec((seq, BRANCH_W), lambda b: (b, 0))]
    out_shape = [jax.ShapeDtypeStruct((batch * seq, BRANCH_W), BF16)]
    if latent:
        in_specs.append(pl.BlockSpec((None, None) + st_shape, lambda b: (b, layer, 0, 0, 0, 0)))
        args.append(state)
    else:
        out_specs.append(pl.BlockSpec((None,) + st_shape, lambda b: (b, 0, 0, 0, 0)))
        out_shape.append(jax.ShapeDtypeStruct((batch,) + st_shape, F32))
    res = pl.pallas_call(
        functools.partial(_ssd_kernel, seq=seq, latent=latent),
        grid=(batch,),
        in_specs=in_specs,
        out_specs=out_specs,
        out_shape=out_shape,
        scratch_shapes=[
            pltpu.VMEM((seq + 2 * SUBLANES, SSD_XBC), F32),
            pltpu.VMEM((seq, SSD_XBC), F32),
            pltpu.VMEM((seq, BRANCH_W), F32),
            pltpu.VMEM((seq, LANES), F32),
            pltpu.VMEM(st_shape, F32),
        ],
        compiler_params=_cparams(("parallel",)),
        name="ssd_latent" if latent else "ssd_context",
    )(*args)
    return res if not latent else res[0]


def _diff_kernel(*refs, seq, ctx_len, latent, lam_init):
    if latent:
        (q_ref, k_ref, v_ref, kc_ref, vc_ref, c_ref, s_ref, lam_ref, nw_ref, o_ref, k_sc, v_sc) = refs
    else:
        (q_ref, k_ref, v_ref, lam_ref, nw_ref, o_ref, kout_ref, vout_ref, k_sc, v_sc) = refs
    lk = ctx_len + seq
    half = DIFF_D // 2
    wid = 2 * DIFF_D
    lv = lam_ref[...]
    lam = (jnp.exp(jnp.sum(lv[0:1] * lv[1:2], axis=1, keepdims=True))
           - jnp.exp(jnp.sum(lv[2:3] * lv[3:4], axis=1, keepdims=True)) + lam_init)
    scale = DIFF_D ** -0.5
    lane_lo = lax.broadcasted_iota(jnp.int32, (TQ, LANES), 1) < DIFF_D
    if not latent:
        kout_ref[...] = k_ref[...]
        vout_ref[...] = v_ref[...]

    for h in range(DIFF_HEADS):
        hs = slice(h * wid, (h + 1) * wid)
        if latent:
            k_sc[0:ctx_len, hs] = kc_ref[:, hs].astype(BF16)
            v_sc[0:ctx_len, hs] = vc_ref[:, hs].astype(BF16)
            kn = _rope(k_ref[:, hs], c_ref[...], s_ref[...], half)
        else:
            kn = k_ref[:, hs]
        k_sc[ctx_len:lk, hs] = kn.astype(BF16)
        v_sc[ctx_len:lk, hs] = v_ref[:, hs].astype(BF16)

        def qtile(t, carry, hs=hs):
            q0 = pl.multiple_of(t * TQ, TQ)
            qv = q_ref[pl.ds(q0, TQ), hs]
            if latent:
                qv = _rope(qv, c_ref[pl.ds(q0, TQ), :], s_ref[pl.ds(q0, TQ), :], half)
            q1 = jnp.where(lane_lo, qv, 0.0).astype(BF16)
            q2 = jnp.where(lane_lo, 0.0, qv).astype(BF16)
            k = k_sc[:, hs]
            v = v_sc[:, hs]
            o = _softmax_pv(q1, k, v, scale) - lam * _softmax_pv(q2, k, v, scale)
            o_ref[pl.ds(q0, TQ), hs] = (_rms(o, nw_ref[...]) * (1.0 - lam_init)).astype(BF16)
            return carry

        lax.fori_loop(0, seq // TQ, qtile, 0)


def diff_branch(proj, row0, batch, seq, lp, layer, ctx=None):
    latent = ctx is not None
    rb = row0 // seq
    ctx_len = ctx[0].shape[2] if latent else 0
    lam_init = 0.8 - 0.6 * math.exp(-0.3 * layer)
    wid = 2 * DIFF_D
    bw = BRANCH_W
    in_specs = [
        pl.BlockSpec((seq, bw), lambda b: (rb + b, C_DQ // bw)),
        pl.BlockSpec((seq, bw), lambda b: (rb + b, C_DK // bw)),
        pl.BlockSpec((seq, bw), lambda b: (rb + b, C_DV // bw)),
    ]
    args = [proj, proj, proj]
    if latent:
        in_specs += [
            pl.BlockSpec((None, None, ctx_len, bw), lambda b: (b, layer, 0, 0)),
            pl.BlockSpec((None, None, ctx_len, bw), lambda b: (b, layer, 0, 0)),
            pl.BlockSpec((seq, wid), lambda b: (0, 0)),
            pl.BlockSpec((seq, wid), lambda b: (0, 0)),
        ]
        args += [ctx[0], ctx[1], lp["rope_c32"], lp["rope_s32"]]
    in_specs += [
        pl.BlockSpec((4, DIFF_D), lambda b: (0, 0)),
        pl.BlockSpec((1, wid), lambda b: (0, 0)),
    ]
    args += [lp["diff_lambda"], lp["diff_norm"]]
    out_specs = [pl.BlockSpec((seq, bw), lambda b: (b, 0))]
    out_shape = [jax.ShapeDtypeStruct((batch * seq, bw), BF16)]
    if not latent:
        out_specs += [pl.BlockSpec((seq, bw), lambda b: (b, 0))] * 2
        out_shape += [jax.ShapeDtypeStruct((batch * seq, bw), F32)] * 2
    res = pl.pallas_call(
        functools.partial(_diff_kernel, seq=seq, ctx_len=ctx_len, latent=latent, lam_init=lam_init),
        grid=(batch,),
        in_specs=in_specs,
        out_specs=out_specs,
        out_shape=out_shape,
        scratch_shapes=[pltpu.VMEM((ctx_len + seq, bw), BF16), pltpu.VMEM((ctx_len + seq, bw), BF16)],
        compiler_params=_cparams(("parallel",)),
        name="diff_latent" if latent else "diff_context",
    )(*args)
    return res if not latent else res[0]


def _gqa_kernel(*refs, seq, ctx_len, latent):
    if latent:
        (q_ref, k_ref, v_ref, kc_ref, vc_ref, c_ref, s_ref, qw_ref, kw_ref, o_ref, k_sc, v_sc) = refs
    else:
        (q_ref, k_ref, v_ref, qw_ref, kw_ref, o_ref, kout_ref, vout_ref, k_sc, v_sc) = refs
    lk = ctx_len + seq
    half = GQA_HEAD_DIM // 2
    hd = GQA_HEAD_DIM
    grp = GQA_HEADS // GQA_KV_HEADS
    scale = GQA_HEAD_DIM ** -0.5
    if not latent:
        vout_ref[...] = v_ref[...]

    for kvh in range(GQA_KV_HEADS):
        ks = slice(kvh * hd, (kvh + 1) * hd)
        kn = _rms(k_ref[:, ks], kw_ref[...])
        if latent:
            k_sc[0:ctx_len, ks] = kc_ref[:, ks].astype(BF16)
            v_sc[0:ctx_len, ks] = vc_ref[:, ks].astype(BF16)
            kn = _rope(kn, c_ref[...], s_ref[...], half)
        else:
            kout_ref[:, ks] = kn
        k_sc[ctx_len:lk, ks] = kn.astype(BF16)
        v_sc[ctx_len:lk, ks] = v_ref[:, ks].astype(BF16)

        def qtile(t, carry, kvh=kvh, ks=ks):
            q0 = pl.multiple_of(t * TQ, TQ)
            qs = []
            for g in range(grp):
                c0 = (kvh * grp + g) * hd
                qv = _rms(q_ref[pl.ds(q0, TQ), c0:c0 + hd], qw_ref[...])
                if latent:
                    qv = _rope(qv, c_ref[pl.ds(q0, TQ), :], s_ref[pl.ds(q0, TQ), :], half)
                qs.append(qv)
            qq = jnp.concatenate(qs, axis=0).astype(BF16)
            o = _softmax_pv(qq, k_sc[:, ks], v_sc[:, ks], scale)
            for g in range(grp):
                c0 = (kvh * grp + g) * hd
                o_ref[pl.ds(q0, TQ), c0:c0 + hd] = o[g * TQ:(g + 1) * TQ].astype(BF16)
            return carry

        lax.fori_loop(0, seq // TQ, qtile, 0)


def gqa_branch(proj, row0, batch, seq, lp, layer, ctx=None):
    latent = ctx is not None
    rb = row0 // seq
    ctx_len = ctx[0].shape[2] if latent else 0
    hd = GQA_HEAD_DIM
    qw = GQA_HEADS * hd
    kw = GQA_KV_HEADS * hd
    in_specs = [
        pl.BlockSpec((seq, qw), lambda b: (rb + b, C_GQ // qw)),
        pl.BlockSpec((seq, kw), lambda b: (rb + b, C_GK // kw)),
        pl.BlockSpec((seq, kw), lambda b: (rb + b, C_GV // kw)),
    ]
    args = [proj, proj, proj]
    if latent:
        in_specs += [
            pl.BlockSpec((None, None, ctx_len, kw), lambda b: (b, layer, 0, 0)),
            pl.BlockSpec((None, None, ctx_len, kw), lambda b: (b, layer, 0, 0)),
            pl.BlockSpec((seq, hd), lambda b: (0, 0)),
            pl.BlockSpec((seq, hd), lambda b: (0, 0)),
        ]
        args += [ctx[0], ctx[1], lp["rope_c64"], lp["rope_s64"]]
    in_specs += [pl.BlockSpec((1, hd), lambda b: (0, 0)), pl.BlockSpec((1, hd), lambda b: (0, 0))]
    args += [lp["gqa_q_norm"], lp["gqa_k_norm"]]
    out_specs = [pl.BlockSpec((seq, qw), lambda b: (b, 0))]
    out_shape = [jax.ShapeDtypeStruct((batch * seq, qw), BF16)]
    if not latent:
        out_specs += [pl.BlockSpec((seq, kw), lambda b: (b, 0))] * 2
        out_shape += [jax.ShapeDtypeStruct((batch * seq, kw), F32)] * 2
    res = pl.pallas_call(
        functools.partial(_gqa_kernel, seq=seq, ctx_len=ctx_len, latent=latent),
        grid=(batch,),
        in_specs=in_specs,
        out_specs=out_specs,
        out_shape=out_shape,
        scratch_shapes=[pltpu.VMEM((ctx_len + seq, kw), BF16), pltpu.VMEM((ctx_len + seq, kw), BF16)],
        compiler_params=_cparams(("parallel",)),
        name="gqa_latent" if latent else "gqa_context",
    )(*args)
    return res if not latent else res[0]


def _mla_kernel(*refs, seq, ctx_len, latent):
    if latent:
        (cq_ref, ckv_ref, small_ref, ckvc_ref, krc_ref, c_ref, s_ref, qw_ref, kvw_ref, wuq_ref, wuk_ref,
         wuv_ref, o_ref, kn_sc, v_sc, kr_sc) = refs
    else:
        (cq_ref, ckv_ref, small_ref, qw_ref, kvw_ref, wuq_ref, wuk_ref, wuv_ref,
         o_ref, ckv_out_ref, kr_out_ref, kn_sc, v_sc, kr_sc) = refs
    lk = ctx_len + seq
    half = MLA_ROPE // 2
    ckv = _rms(ckv_ref[...], kvw_ref[...])
    if latent:
        cc = ckvc_ref[...].astype(BF16)
        kn_sc[0:ctx_len, :] = _dot(cc, wuk_ref[...]).astype(BF16)
        v_sc[0:ctx_len, :] = _dot(cc, wuv_ref[...]).astype(BF16)
        kr_sc[0:ctx_len, :] = krc_ref[...].astype(BF16)
    else:
        ckv_out_ref[...] = ckv
        kr_out_ref[...] = small_ref[:, 0:MLA_ROPE]
    cb = ckv.astype(BF16)
    kn_sc[ctx_len:lk, :] = _dot(cb, wuk_ref[...]).astype(BF16)
    v_sc[ctx_len:lk, :] = _dot(cb, wuv_ref[...]).astype(BF16)
    kr = small_ref[...]
    if latent:
        kr = _rope(kr, c_ref[...], s_ref[...], half)
    lane_lo = lax.broadcasted_iota(jnp.int32, (seq, LANES), 1) < MLA_ROPE
    kr_sc[ctx_len:lk, :] = jnp.where(lane_lo, kr, 0.0).astype(BF16)
    scale = (MLA_NOPE + MLA_ROPE) ** -0.5
    nope_w = MLA_HEADS * MLA_NOPE

    def qtile(t, carry):
        q0 = pl.multiple_of(t * TQ, TQ)
        cq = _dot(_rms(cq_ref[pl.ds(q0, TQ), :], qw_ref[...]).astype(BF16), wuq_ref[...])
        for h in range(MLA_HEADS):
            hs = slice(h * MLA_NOPE, (h + 1) * MLA_NOPE)
            qn = cq[:, hs].astype(BF16)
            qr = cq[:, nope_w + h * LANES:nope_w + (h + 1) * LANES]
            if latent:
                qr = _rope(qr, c_ref[pl.ds(q0, TQ), :], s_ref[pl.ds(q0, TQ), :], half)
            s = (_dot_nt(qn, kn_sc[:, hs]) + _dot_nt(qr.astype(BF16), kr_sc[...])) * scale
            m = jnp.max(s, axis=-1, keepdims=True)
            p = jnp.exp(s - m)
            l = jnp.sum(p, axis=-1, keepdims=True)
            o = _dot(p.astype(BF16), v_sc[:, hs]) / l
            o_ref[pl.ds(q0, TQ), hs] = o.astype(BF16)
        return carry

    lax.fori_loop(0, seq // TQ, qtile, 0)


def mla_branch(proj, row0, batch, seq, lp, layer, ctx=None):
    latent = ctx is not None
    rb = row0 // seq
    ctx_len = ctx[0].shape[2] if latent else 0
    const = lambda b: (0, 0)
    in_specs = [
        pl.BlockSpec((seq, MLA_Q_RANK), lambda b: (rb + b, C_MCQ // MLA_Q_RANK)),
        pl.BlockSpec((seq, MLA_KV_RANK), lambda b: (rb + b, C_MCKV // MLA_KV_RANK)),
        pl.BlockSpec((seq, LANES), lambda b: (rb + b, C_SMALL // LANES)),
    ]
    args = [proj, proj, proj]
    if latent:
        in_specs += [
            pl.BlockSpec((None, None, ctx_len, MLA_KV_RANK), lambda b: (b, layer, 0, 0)),
            pl.BlockSpec((None, None, ctx_len, LANES), lambda b: (b, layer, 0, 0)),
            pl.BlockSpec((seq, LANES), const),
            pl.BlockSpec((seq, LANES), const),
        ]
        args += [ctx[0], ctx[1], lp["rope_c32"], lp["rope_s32"]]
    in_specs += [
        pl.BlockSpec((1, MLA_Q_RANK), const),
        pl.BlockSpec((1, MLA_KV_RANK), const),
        pl.BlockSpec((MLA_Q_RANK, 2 * MLA_HEADS * LANES), const),
        pl.BlockSpec((MLA_KV_RANK, MLA_HEADS * MLA_NOPE), const),
        pl.BlockSpec((MLA_KV_RANK, BRANCH_W), const),
    ]
    args += [lp["mla_q_norm"], lp["mla_kv_norm"], lp["w_uq"], lp["w_uk"], lp["w_uv"]]
    out_specs = [pl.BlockSpec((seq, BRANCH_W), lambda b: (b, 0))]
    out_shape = [jax.ShapeDtypeStruct((batch * seq, BRANCH_W), BF16)]
    if not latent:
        out_specs += [pl.BlockSpec((seq, MLA_KV_RANK), lambda b: (b, 0)),
                      pl.BlockSpec((seq, MLA_ROPE), lambda b: (b, 0))]
        out_shape += [jax.ShapeDtypeStruct((batch * seq, MLA_KV_RANK), F32),
                      jax.ShapeDtypeStruct((batch * seq, MLA_ROPE), F32)]
    lk = ctx_len + seq
    res = pl.pallas_call(
        functools.partial(_mla_kernel, seq=seq, ctx_len=ctx_len, latent=latent),
        grid=(batch,),
        in_specs=in_specs,
        out_specs=out_specs,
        out_shape=out_shape,
        scratch_shapes=[pltpu.VMEM((lk, MLA_HEADS * MLA_NOPE), BF16), pltpu.VMEM((lk, BRANCH_W), BF16),
                        pltpu.VMEM((lk, LANES), BF16)],
        compiler_params=_cparams(("parallel",)),
        name="mla_latent" if latent else "mla_context",
    )(*args)
    return res if not latent else res[0]


def _pack_pairs(lo, hi):
    lo_bits = lax.bitcast_convert_type(lo.astype(BF16).astype(F32), jnp.uint32)
    hi_bits = lax.bitcast_convert_type(hi.astype(BF16).astype(F32), jnp.uint32)
    return (lo_bits >> jnp.uint32(16)) | (hi_bits & jnp.uint32(0xFFFF0000))


def _unpack_pairs(w):
    lo = lax.bitcast_convert_type(w << jnp.uint32(16), F32)
    hi = lax.bitcast_convert_type(w & jnp.uint32(0xFFFF0000), F32)
    return lo, hi


def _merge_kernel(*refs, n_ctx_tiles):
    oc_refs, ol_refs, g_refs = refs[0:4], refs[4:8], refs[8:12]
    (wb_ref, wo_ref, x_ref, ga_ref, shf_ref, scf_ref, lg_ref, lb_ref, rw_ref, rb_ref,
     x1_ref, u2_ref, lo_ref, acc_sc) = refs[12:]
    i = pl.program_id(0)
    j = pl.program_id(1)

    @pl.when(j == 0)
    def _():
        acc_sc[...] = jnp.zeros(acc_sc.shape, F32)

    is_ctx = i < n_ctx_tiles
    merged = None
    for n in range(N_BRANCH):
        o = jnp.where(is_ctx, oc_refs[n][...], ol_refs[n][...])
        term = _sigmoid(g_refs[n][...]) * _dot(o, wb_ref[n])
        merged = term if merged is None else merged + term
    acc_sc[...] += _dot(merged.astype(BF16), wo_ref[...])

    @pl.when(j == pl.num_programs(1) - 1)
    def _():
        h = DEEPNORM_ALPHA * x_ref[...] + ga_ref[...] * acc_sc[...]
        x1 = _ln(h) * lg_ref[...] + lb_ref[...]
        x1_ref[...] = x1
        u2 = (_ln(x1) * (1.0 + scf_ref[...]) + shf_ref[...]).astype(BF16)
        half = D_MODEL // 2
        u2f = u2.astype(F32)
        u2_ref[...] = _pack_pairs(u2f[:, :half], u2f[:, half:])
        lo_ref[...] = _dot(u2, rw_ref[...]) + rb_ref[...]


def merge_and_route(outs_ctx, outs_lat, proj, x, mods, lp, layer, n_ctx_rows, lat_len):
    t = x.shape[0]
    tm, tn = TM_MERGE, TN_MERGE
    n_ctx_tiles = n_ctx_rows // tm
    row = functools.partial(_mod_row, tm=tm, n_ctx_rows=n_ctx_rows, lat_len=lat_len)
    gate_spec = lambda n: pl.BlockSpec((tm, tn), lambda i, j: (i, (C_GATE + n * D_MODEL) // tn + j))
    vec = lambda k: pl.BlockSpec((None, 1, D_MODEL), lambda i, j: (row(i), 0, k))
    cvec = pl.BlockSpec((None, 1, D_MODEL), lambda i, j: (layer, 0, 0))
    in_specs = (
        [pl.BlockSpec((tm, BRANCH_W), lambda i, j: (jnp.minimum(i, n_ctx_tiles - 1), 0))] * N_BRANCH
        + [pl.BlockSpec((tm, BRANCH_W), lambda i, j: (jnp.maximum(i - n_ctx_tiles, 0), 0))] * N_BRANCH
        + [gate_spec(n) for n in range(N_BRANCH)]
        + [
            pl.BlockSpec((None, N_BRANCH, BRANCH_W, tn), lambda i, j: (layer, 0, 0, j)),
            pl.BlockSpec((None, tn, D_MODEL), lambda i, j: (layer, j, 0)),
            pl.BlockSpec((tm, D_MODEL), lambda i, j: (i, 0)),
            vec(2), vec(3), vec(4), cvec, cvec,
            pl.BlockSpec((None, D_MODEL, LANES), lambda i, j: (layer, 0, 0)),
            pl.BlockSpec((None, 1, LANES), lambda i, j: (layer, 0, 0)),
        ]
    )
    return pl.pallas_call(
        functools.partial(_merge_kernel, n_ctx_tiles=n_ctx_tiles),
        grid=(t // tm, D_MODEL // tn),
        in_specs=in_specs,
        out_specs=[
            pl.BlockSpec((tm, D_MODEL), lambda i, j: (i, 0)),
            pl.BlockSpec((tm, D_MODEL // 2), lambda i, j: (i, 0)),
            pl.BlockSpec((tm, LANES), lambda i, j: (i, 0)),
        ],
        out_shape=[
            jax.ShapeDtypeStruct((t, D_MODEL), F32),
            jax.ShapeDtypeStruct((t, D_MODEL // 2), jnp.uint32),
            jax.ShapeDtypeStruct((t, LANES), F32),
        ],
        scratch_shapes=[pltpu.VMEM((tm, D_MODEL), F32)],
        compiler_params=_cparams(("parallel", "arbitrary")),
        name="merge_route",
    )(*outs_ctx, *outs_lat, proj, proj, proj, proj, lp["w_branch"], lp["w_out"], x, mods, mods, mods,
      lp["ln_mix_g"], lp["ln_mix_b"], lp["router_w"], lp["router_b"])


def _moe_up_kernel(e_ref, src_ref, valid_ref, first_ref, x_ref, wg_ref, wl_ref, bg_ref, bl_ref, h_ref,
                   wg_sc, wl_sc):
    blk = pl.program_id(0)

    @pl.when(first_ref[blk] == 1)
    def _():
        wg_sc[...] = wg_ref[...].astype(BF16)
        wl_sc[...] = wl_ref[...].astype(BF16)

    @pl.when(valid_ref[blk] == 1)
    def _():
        lo, hi = _unpack_pairs(x_ref[...])
        x = jnp.concatenate([lo, hi], axis=1).astype(BF16)
        g = jnp.minimum(_dot(x, wg_sc[...]) + bg_ref[...], SWIGLU_LIMIT)
        lin = jnp.clip(_dot(x, wl_sc[...]) + bl_ref[...], -SWIGLU_LIMIT, SWIGLU_LIMIT)
        h_ref[...] = ((lin + 1.0) * (g * _sigmoid(SWIGLU_ALPHA * g))).astype(BF16)


def _moe_down_kernel(e_ref, src_ref, valid_ref, first_ref, h_ref, wd_ref, bd_ref, y_ref, wd_sc):
    blk = pl.program_id(0)

    @pl.when(first_ref[blk] == 1)
    def _():
        wd_sc[...] = wd_ref[...].astype(BF16)

    @pl.when(valid_ref[blk] == 1)
    def _():
        y = _dot(h_ref[...], wd_sc[...]) + bd_ref[...]
        half = D_MODEL // 2
        y_ref[...] = _pack_pairs(y[:, :half], y[:, half:])


def moe_experts(xs, plan, w_gu, b_gu, w_down, b_down, layer):
    p = xs.shape[0]
    nb = p // TM_MOE
    tm = TM_MOE
    half = D_MODEL // 2
    hidden = pl.pallas_call(
        _moe_up_kernel,
        grid_spec=pltpu.PrefetchScalarGridSpec(
            num_scalar_prefetch=4,
            grid=(nb,),
            in_specs=[
                pl.BlockSpec((tm, half), lambda b, e, s, v, fi: (s[b], 0)),
                pl.BlockSpec((None, None, D_MODEL, EXPERT_FF), lambda b, e, s, v, fi: (layer, e[b], 0, 0)),
                pl.BlockSpec((None, None, D_MODEL, EXPERT_FF), lambda b, e, s, v, fi: (layer, e[b], 0, 1)),
                pl.BlockSpec((None, None, 1, EXPERT_FF), lambda b, e, s, v, fi: (layer, e[b], 0, 0)),
                pl.BlockSpec((None, None, 1, EXPERT_FF), lambda b, e, s, v, fi: (layer, e[b], 0, 1)),
            ],
            out_specs=pl.BlockSpec((tm, EXPERT_FF), lambda b, e, s, v, fi: (s[b], 0)),
            scratch_shapes=[pltpu.VMEM((D_MODEL, EXPERT_FF), BF16), pltpu.VMEM((D_MODEL, EXPERT_FF), BF16)],
        ),
        out_shape=jax.ShapeDtypeStruct((p, EXPERT_FF), BF16),
        compiler_params=_cparams(("arbitrary",)),
        name="moe_up",
    )(*plan, xs, w_gu, w_gu, b_gu, b_gu)
    return pl.pallas_call(
        _moe_down_kernel,
        grid_spec=pltpu.PrefetchScalarGridSpec(
            num_scalar_prefetch=4,
            grid=(nb,),
            in_specs=[
                pl.BlockSpec((tm, EXPERT_FF), lambda b, e, s, v, fi: (s[b], 0)),
                pl.BlockSpec((None, None, EXPERT_FF, D_MODEL), lambda b, e, s, v, fi: (layer, e[b], 0, 0)),
                pl.BlockSpec((None, None, 1, D_MODEL), lambda b, e, s, v, fi: (layer, e[b], 0, 0)),
            ],
            out_specs=pl.BlockSpec((tm, half), lambda b, e, s, v, fi: (s[b], 0)),
            scratch_shapes=[pltpu.VMEM((EXPERT_FF, D_MODEL), BF16)],
        ),
        out_shape=jax.ShapeDtypeStruct((p, half), jnp.uint32),
        compiler_params=_cparams(("arbitrary",)),
        name="moe_down",
    )(*plan, hidden, w_down, b_down)


def route(logits):
    t = logits.shape[0]
    a = t * TOP_K
    top_v, top_i = lax.top_k(logits, TOP_K)
    gate = jax.nn.softmax(top_v, axis=-1)
    flat_e = top_i.reshape(-1)
    onehot = (flat_e[:, None] == jnp.arange(N_EXPERTS, dtype=jnp.int32)[None, :]).astype(jnp.int32)
    csum = jnp.cumsum(onehot, axis=0)
    counts = csum[-1]
    padded = (counts + TM_MOE - 1) // TM_MOE * TM_MOE
    ends = jnp.cumsum(padded)
    slot = jnp.sum(onehot * (csum - 1 + (ends - padded)[None, :]), axis=1)
    nb = a // TM_MOE + N_EXPERTS
    blk = jnp.arange(nb, dtype=jnp.int32)
    n_used = (ends[-1] // TM_MOE).astype(jnp.int32)
    src = jnp.minimum(blk, n_used - 1)
    blk_e = jnp.minimum(jnp.searchsorted(ends, src * TM_MOE, side="right"), N_EXPERTS - 1).astype(jnp.int32)
    valid = (blk < n_used).astype(jnp.int32)
    first = jnp.where((blk == 0) | (blk_e != jnp.roll(blk_e, 1)), 1, 0).astype(jnp.int32)
    key_bits = 16
    assert a < (1 << key_bits)
    skey = lax.sort(flat_e * (1 << key_bits) + jnp.arange(a, dtype=jnp.int32))
    sorted_t = (skey & ((1 << key_bits) - 1)) // TOP_K
    start = jnp.cumsum(counts) - counts
    blk_off = blk * TM_MOE - (ends - padded)[blk_e]
    base = start[blk_e] + blk_off
    room = jnp.where(valid == 1, counts[blk_e] - blk_off, 0)
    r = jnp.arange(TM_MOE, dtype=jnp.int32)[None, :]
    live = r < room[:, None]
    filler = (blk[:, None] * TM_MOE + r) % t
    tok = jnp.where(live, sorted_t[jnp.clip(base[:, None] + r, 0, a - 1)], filler).reshape(-1)
    tok = lax.optimization_barrier(tok)
    return gate, slot.reshape(t, TOP_K), tok, (blk_e, src, valid, first)


def _ffn_norm_kernel(x_ref, y_ref, gate_ref, gf_ref, lg_ref, lb_ref, o_ref):
    gate = gate_ref[...]
    ffn_lo = ffn_hi = None
    for k in range(TOP_K):
        lo, hi = _unpack_pairs(y_ref[k])
        gk = gate[:, k:k + 1]
        ffn_lo = lo * gk if ffn_lo is None else ffn_lo + lo * gk
        ffn_hi = hi * gk if ffn_hi is None else ffn_hi + hi * gk
    ffn = jnp.concatenate([ffn_lo, ffn_hi], axis=1)
    h = DEEPNORM_ALPHA * x_ref[...] + gf_ref[...] * ffn
    o_ref[...] = _ln(h) * lg_ref[...] + lb_ref[...]


def ffn_norm(x1, y_rows, gate, mods, lp, layer, n_ctx_rows, lat_len):
    t = x1.shape[0]
    tm = TM_LN
    row = functools.partial(_mod_row, tm=tm, n_ctx_rows=n_ctx_rows, lat_len=lat_len)
    cvec = pl.BlockSpec((None, 1, D_MODEL), lambda i: (layer, 0, 0))
    return pl.pallas_call(
        _ffn_norm_kernel,
        grid=(t // tm,),
        in_specs=[
            pl.BlockSpec((tm, D_MODEL), lambda i: (i, 0)),
            pl.BlockSpec((TOP_K, tm, D_MODEL // 2), lambda i: (0, i, 0)),
            pl.BlockSpec((tm, TOP_K), lambda i: (i, 0)),
            pl.BlockSpec((None, 1, D_MODEL), lambda i: (row(i), 0, 5)),
            cvec, cvec,
        ],
        out_specs=pl.BlockSpec((tm, D_MODEL), lambda i: (i, 0)),
        out_shape=jax.ShapeDtypeStruct((t, D_MODEL), F32),
        compiler_params=_cparams(("parallel",)),
        name="ffn_norm",
    )(x1, y_rows, gate, mods, lp["ln_ffn_g"], lp["ln_ffn_b"])


def _rope_tables(n_tokens, seg, width):
    rows = n_tokens // GRID_W
    row = jnp.repeat(jnp.arange(rows, dtype=F32), GRID_W)
    col = jnp.tile(jnp.arange(GRID_W, dtype=F32), rows)
    n_freq = seg // 4
    inv = ROPE_THETA ** (-jnp.arange(n_freq, dtype=F32) / n_freq)
    ang = jnp.concatenate([row[:, None] * inv, col[:, None] * inv], axis=-1)
    cos, sin = jnp.cos(ang), jnp.sin(ang)
    reps = width // seg
    return (jnp.tile(jnp.concatenate([cos, cos], axis=-1), (1, reps)),
            jnp.tile(jnp.concatenate([-sin, sin], axis=-1), (1, reps)))


def _pack_w_in_kernel(w_ref, o_ref):
    def put(dst, src, n):
        for t0 in range(0, n, LANES):
            o_ref[:, dst + t0:dst + t0 + LANES] = w_ref[src + t0:src + t0 + LANES, :].T.astype(BF16)

    put(C_XBC, BRANCH_W, SSD_XBC)
    put(C_Z, 0, BRANCH_W)
    put(C_DQ, _O_DQ, _O_MCKV - _O_DQ)
    put(C_GQ, _O_GQ, _O_GK - _O_GQ)
    put(C_MCKV, _O_MCKV, MLA_KV_RANK)
    put(C_GK, _O_GK, _O_GATE - _O_GK)
    n_dt = 2 * SSD_HEADS
    small = jnp.concatenate([w_ref[_O_MKR:_O_MKR + MLA_ROPE, :], w_ref[_O_DT:_O_DT + n_dt, :],
                             jnp.zeros((LANES - MLA_ROPE - n_dt, TR_PACK), F32)], axis=0)
    o_ref[:, C_SMALL:C_SMALL + LANES] = small.T.astype(BF16)
    o_ref[:, C_SMALL + LANES:C_GATE] = jnp.zeros((TR_PACK, C_GATE - C_SMALL - LANES), BF16)
    put(C_GATE, _O_GATE, N_BRANCH * D_MODEL)


def _pack_w_in(w_in):
    tr = TR_PACK
    return pl.pallas_call(
        _pack_w_in_kernel,
        grid=(DEPTH, D_MODEL // tr),
        in_specs=[pl.BlockSpec((None, IN_WIDTH, tr), lambda l, r: (l, 0, r))],
        out_specs=pl.BlockSpec((None, tr, PROJ_W), lambda l, r: (l, r, 0)),
        out_shape=jax.ShapeDtypeStruct((DEPTH, D_MODEL, PROJ_W), BF16),
        compiler_params=_cparams(("parallel", "parallel")),
        name="pack_w_in",
    )(jnp.transpose(w_in, (0, 2, 1)))


def _pack_w_uq(w_uq):
    w = w_uq.reshape(MLA_Q_RANK, MLA_HEADS, MLA_NOPE + MLA_ROPE)
    nope = w[:, :, :MLA_NOPE].reshape(MLA_Q_RANK, MLA_HEADS * MLA_NOPE)
    rope = jnp.pad(w[:, :, MLA_NOPE:], ((0, 0), (0, 0), (0, LANES - MLA_ROPE)))
    return jnp.concatenate([nope, rope.reshape(MLA_Q_RANK, MLA_HEADS * LANES)], axis=-1).astype(BF16)


def _small_row(v16):
    return jnp.zeros((1, LANES), F32).at[0, SMALL_DT:SMALL_DT + 2 * SSD_HEADS].set(v16.reshape(-1))


def kernel(x_prompt, x_sample, state_ssd, cache_diff_k, cache_diff_v, cache_mla_ckv, cache_mla_kr, cache_gqa_k, cache_gqa_v, c, c_ctx, w_mod, b_mod, w_in, ssd_conv_w, ssd_conv_b, ssd_dt_bias, ssd_a_log, ssd_d, ssd_norm, diff_lambda, diff_norm, mla_q_norm, mla_w_uq, mla_kv_norm, mla_w_uk, mla_w_uv, gqa_q_norm, gqa_k_norm, w_branch, w_out, ln_mix_g, ln_mix_b, router_w, router_b, exp_w_gu, exp_b_gu, exp_w_down, exp_b_down, ln_ffn_g, ln_ffn_b):
    nb_ctx, len_ctx, _ = x_prompt.shape
    nb_lat, len_lat, _ = x_sample.shape
    past = cache_diff_k.shape[2]
    n_ctx_rows = nb_ctx * len_ctx
    t = n_ctx_rows + nb_lat * len_lat

    cvec = jnp.zeros((SUBLANES, D_MODEL), F32).at[0].set(c_ctx).at[1:1 + nb_lat].set(c)
    mods_all = modulation(cvec, w_mod, b_mod)

    w_in_p = _pack_w_in(w_in)
    w_branch_b = w_branch.astype(BF16)
    w_out_b = w_out.astype(BF16)
    router_w_p = jnp.pad(router_w, ((0, 0), (0, 0), (0, LANES - N_EXPERTS))).astype(BF16)
    router_b_p = jnp.pad(router_b, ((0, 0), (0, LANES - N_EXPERTS))).reshape(DEPTH, 1, LANES)
    c32, s32 = _rope_tables(len_lat, 2 * (DIFF_D // 2), LANES)
    c64, s64 = _rope_tables(len_lat, GQA_HEAD_DIM, LANES)
    n_pair = SSD_HEADS // 2
    state_in = state_ssd.reshape(nb_lat, DEPTH, 2, n_pair, 2 * SSD_HEAD_DIM, SSD_STATE)
    diff_kc = cache_diff_k.reshape(nb_lat, DEPTH, past, BRANCH_W)
    diff_vc = cache_diff_v.reshape(nb_lat, DEPTH, past, BRANCH_W)
    mla_krc = jnp.pad(cache_mla_kr, ((0, 0), (0, 0), (0, 0), (0, LANES - MLA_ROPE)))
    gqa_kc = cache_gqa_k.reshape(nb_lat, DEPTH, past, GQA_KV_HEADS * GQA_HEAD_DIM)
    gqa_vc = cache_gqa_v.reshape(nb_lat, DEPTH, past, GQA_KV_HEADS * GQA_HEAD_DIM)
    vecs = dict(ln_mix_g=ln_mix_g.reshape(DEPTH, 1, D_MODEL), ln_mix_b=ln_mix_b.reshape(DEPTH, 1, D_MODEL),
                ln_ffn_g=ln_ffn_g.reshape(DEPTH, 1, D_MODEL), ln_ffn_b=ln_ffn_b.reshape(DEPTH, 1, D_MODEL),
                w_branch=w_branch_b, w_out=w_out_b, router_w=router_w_p, router_b=router_b_p)
    b_gu = exp_b_gu.reshape(DEPTH, N_EXPERTS, 1, 2 * EXPERT_FF)
    b_down = exp_b_down.reshape(DEPTH, N_EXPERTS, 1, D_MODEL)

    x = jnp.concatenate([x_prompt.reshape(n_ctx_rows, D_MODEL), x_sample.reshape(-1, D_MODEL)], axis=0)
    new_state, new_dk, new_dv, new_ckv, new_kr, new_gk, new_gv = [], [], [], [], [], [], []
    for l in range(DEPTH):
        lp = dict(vecs)
        lp.update(
            conv_w=jnp.pad(ssd_conv_w[l], ((0, SUBLANES - SSD_CONV), (0, 0))),
            conv_b=ssd_conv_b[l].reshape(1, SSD_XBC),
            dt_bias=_small_row(ssd_dt_bias[l]),
            a_row=_small_row(-jnp.exp(ssd_a_log[l])),
            d_row=jnp.repeat(ssd_d[l], SSD_HEAD_DIM).reshape(1, BRANCH_W),
            ssd_norm=ssd_norm[l].reshape(1, BRANCH_W),
            diff_lambda=diff_lambda[l],
            diff_norm=diff_norm[l].reshape(1, 2 * DIFF_D),
            mla_q_norm=mla_q_norm[l].reshape(1, MLA_Q_RANK),
            mla_kv_norm=mla_kv_norm[l].reshape(1, MLA_KV_RANK),
            w_uq=_pack_w_uq(mla_w_uq[l]),
            w_uk=mla_w_uk[l].astype(BF16),
            w_uv=mla_w_uv[l].astype(BF16),
            gqa_q_norm=gqa_q_norm[l].reshape(1, GQA_HEAD_DIM),
            gqa_k_norm=gqa_k_norm[l].reshape(1, GQA_HEAD_DIM),
            rope_c32=c32, rope_s32=s32, rope_c64=c64, rope_s64=s64,
        )
        mods = mods_all[l].reshape(SUBLANES, 1, 6 * D_MODEL)
        proj = in_projection(x, mods, w_in_p, l, n_ctx_rows, len_lat)

        o_ssd_c, st = ssd_branch(proj, 0, nb_ctx, len_ctx, lp, l)
        o_ssd_l = ssd_branch(proj, n_ctx_rows, nb_lat, len_lat, lp, l, state_in)
        o_diff_c, dk_c, dv_c = diff_branch(proj, 0, nb_ctx, len_ctx, lp, l)
        o_diff_l = diff_branch(proj, n_ctx_rows, nb_lat, len_lat, lp, l, (diff_kc, diff_vc))
        o_mla_c, ckv_c, kr_c = mla_branch(proj, 0, nb_ctx, len_ctx, lp, l)
        o_mla_l = mla_branch(proj, n_ctx_rows, nb_lat, len_lat, lp, l, (cache_mla_ckv, mla_krc))
        o_gqa_c, gk_c, gv_c = gqa_branch(proj, 0, nb_ctx, len_ctx, lp, l)
        o_gqa_l = gqa_branch(proj, n_ctx_rows, nb_lat, len_lat, lp, l, (gqa_kc, gqa_vc))
        outs_ctx = (o_ssd_c, o_diff_c, o_mla_c, o_gqa_c)
        outs_lat = (o_ssd_l, o_diff_l, o_mla_l, o_gqa_l)

        new_state.append(st.reshape(nb_ctx, 2, SSD_HEADS, SSD_HEAD_DIM, SSD_STATE))
        new_dk.append(dk_c.reshape(nb_ctx, len_ctx, DIFF_HEADS, 2 * DIFF_D))
        new_dv.append(dv_c.reshape(nb_ctx, len_ctx, DIFF_HEADS, 2 * DIFF_D))
        new_ckv.append(ckv_c.reshape(nb_ctx, len_ctx, MLA_KV_RANK))
        new_kr.append(kr_c.reshape(nb_ctx, len_ctx, MLA_ROPE))
        new_gk.append(gk_c.reshape(nb_ctx, len_ctx, GQA_KV_HEADS, GQA_HEAD_DIM))
        new_gv.append(gv_c.reshape(nb_ctx, len_ctx, GQA_KV_HEADS, GQA_HEAD_DIM))

        x1, u2, logits = merge_and_route(outs_ctx, outs_lat, proj, x, mods, lp, l, n_ctx_rows, len_lat)
        gate, slot, tok, plan = route(logits[:, :N_EXPERTS])
        u2 = jnp.pad(u2, ((0, max(0, SC_GATHER_MIN_ROWS - t)), (0, 0)))
        yb = moe_experts(u2[tok], plan, exp_w_gu, b_gu, exp_w_down, b_down, l)
        y_rows = yb[slot.T.reshape(-1)].reshape(TOP_K, t, D_MODEL // 2)
        x = ffn_norm(x1, y_rows, gate, mods, lp, l, n_ctx_rows, len_lat)

    y_prompt = x[:n_ctx_rows].reshape(nb_ctx, len_ctx, D_MODEL)
    y_sample = x[n_ctx_rows:].reshape(nb_lat, len_lat, D_MODEL)
    stack = lambda parts: jnp.stack(parts, axis=1)
    return (y_prompt, y_sample, stack(new_state), stack(new_dk), stack(new_dv), stack(new_ckv),
            stack(new_kr), stack(new_gk), stack(new_gv))
```

```python
import functools
import math

import jax
import jax.numpy as jnp
from jax import lax
from jax.experimental import pallas as pl
from jax.experimental.pallas import tpu as pltpu

F32 = jnp.float32
BF16 = jnp.bfloat16

D_MODEL = 2048
DEPTH = 4
GRID_W = 64
ROPE_THETA = 10000.0
BRANCH_W = 512
N_BRANCH = 4
SSD_HEADS = 8
SSD_HEAD_DIM = 64
SSD_STATE = 128
SSD_CONV = 5
SSD_CHUNK = 128
SSD_XBC = 1024
DIFF_HEADS = 4
DIFF_D = 64
MLA_HEADS = 4
MLA_NOPE = 128
MLA_ROPE = 64
MLA_Q_RANK = 512
MLA_KV_RANK = 256
GQA_HEAD_DIM = 128
GQA_HEADS = 4
GQA_KV_HEADS = 2
N_EXPERTS = 32
TOP_K = 4
EXPERT_FF = 1024
SWIGLU_LIMIT = 7.0
SWIGLU_ALPHA = 1.702
DEEPNORM_ALPHA = (2.0 * DEPTH) ** 0.25
EPS = 1e-6

LANES = 128
SUBLANES = 8

_O_DT = 1536
_O_DQ = 1552
_O_MCKV = 3600
_O_MKR = 3856
_O_GQ = 3920
_O_GK = 4432
_O_GATE = 4944
IN_WIDTH = 13136
C_XBC = 0
C_Z = 1024
C_DQ = 1536
C_DK = 2048
C_DV = 2560
C_MCQ = 3072
C_GQ = 3584
C_MCKV = 4096
C_GK = 4352
C_GV = 4608
C_SMALL = 4864
C_GATE = 5120
PROJ_W = C_GATE + N_BRANCH * D_MODEL
SMALL_DT = MLA_ROPE

VMEM_LIMIT = 60000 * 1024

TM_IN = 1024
TN_IN = 1024
TM_MERGE = 512
TN_MERGE = 512
TM_LN = 512
TM_MOE = 512
TN_MOD = 1024
TR_PACK = 128
TQ = 256
SC_GATHER_MIN_ROWS = 16384


def _cparams(sem):
    return pltpu.CompilerParams(dimension_semantics=sem, vmem_limit_bytes=VMEM_LIMIT)


def _dot(a, b):
    return jnp.dot(a, b, preferred_element_type=F32)


def _dot_nt(a, b):
    return lax.dot_general(a, b, (((1,), (1,)), ((), ())), preferred_element_type=F32)


def _ln(x):
    mu = jnp.mean(x, axis=-1, keepdims=True)
    xc = x - mu
    var = jnp.mean(xc * xc, axis=-1, keepdims=True)
    return xc * lax.rsqrt(var + EPS)


def _rms(x, g):
    return x * lax.rsqrt(jnp.mean(x * x, axis=-1, keepdims=True) + EPS) * g


def _sigmoid(x):
    return 0.5 * jnp.tanh(0.5 * x) + 0.5


def _silu(x):
    return x * _sigmoid(x)


def _rope(x, c, s, half):
    w = x.shape[-1]
    lane = lax.broadcasted_iota(jnp.int32, x.shape, 1)
    lo = (lane & (2 * half - 1)) < half
    swapped = jnp.where(lo, pltpu.roll(x, w - half, 1), pltpu.roll(x, half, 1))
    return x * c + swapped * s


def _softmax_pv(q, k, v, scale):
    s = _dot_nt(q, k) * scale
    m = jnp.max(s, axis=-1, keepdims=True)
    p = jnp.exp(s - m)
    l = jnp.sum(p, axis=-1, keepdims=True)
    return _dot(p.astype(BF16), v) / l


def _mod_row(i, tm, n_ctx_rows, lat_len):
    r = i * tm
    return jnp.where(r < n_ctx_rows, 0, 1 + (r - n_ctx_rows) // lat_len)


def _mod_kernel(c_ref, w_ref, b_ref, o_ref):
    c = c_ref[...]
    o_ref[...] = _dot(_silu(c).astype(BF16), w_ref[...].astype(BF16)) + b_ref[...]


def modulation(cvec, w_mod, b_mod):
    n = w_mod.shape[-1]
    return pl.pallas_call(
        _mod_kernel,
        grid=(DEPTH, n // TN_MOD),
        in_specs=[
            pl.BlockSpec((SUBLANES, D_MODEL), lambda l, j: (0, 0)),
            pl.BlockSpec((None, D_MODEL, TN_MOD), lambda l, j: (l, 0, j)),
            pl.BlockSpec((None, 1, TN_MOD), lambda l, j: (l, 0, j)),
        ],
        out_specs=pl.BlockSpec((None, SUBLANES, TN_MOD), lambda l, j: (l, 0, j)),
        out_shape=jax.ShapeDtypeStruct((DEPTH, SUBLANES, n), F32),
        compiler_params=_cparams(("parallel", "parallel")),
        name="modulation",
    )(cvec, w_mod, b_mod.reshape(DEPTH, 1, n))


def _inproj_kernel(x_ref, sh_ref, sc_ref, w_ref, o_ref, u_sc):
    @pl.when(pl.program_id(1) == 0)
    def _():
        u = _ln(x_ref[...]) * (1.0 + sc_ref[...]) + sh_ref[...]
        u_sc[...] = u.astype(BF16)

    o_ref[...] = _dot(u_sc[...], w_ref[...])


def in_projection(x, mods, w_in_p, layer, n_ctx_rows, lat_len):
    t = x.shape[0]
    tm = TM_IN
    row = functools.partial(_mod_row, tm=tm, n_ctx_rows=n_ctx_rows, lat_len=lat_len)
    return pl.pallas_call(
        _inproj_kernel,
        grid=(t // tm, PROJ_W // TN_IN),
        in_specs=[
            pl.BlockSpec((tm, D_MODEL), lambda i, j: (i, 0)),
            pl.BlockSpec((None, 1, D_MODEL), lambda i, j: (row(i), 0, 0)),
            pl.BlockSpec((None, 1, D_MODEL), lambda i, j: (row(i), 0, 1)),
            pl.BlockSpec((None, D_MODEL, TN_IN), lambda i, j: (layer, 0, j)),
        ],
        out_specs=pl.BlockSpec((tm, TN_IN), lambda i, j: (i, j)),
        out_shape=jax.ShapeDtypeStruct((t, PROJ_W), F32),
        scratch_shapes=[pltpu.VMEM((tm, D_MODEL), BF16)],
        compiler_params=_cparams(("parallel", "arbitrary")),
        name="in_projection",
    )(x, mods, mods, w_in_p)


def _lane_col(a, lane):
    idx = lax.broadcasted_iota(jnp.int32, a.shape, 1)
    return jnp.sum(jnp.where(idx == lane, a, 0.0), axis=1, keepdims=True)


def _ssd_kernel(*refs, seq, latent):
    if latent:
        (z_ref, xbc_ref, small_ref, cw_ref, cb_ref, dtb_ref, arow_ref, drow_ref, nw_ref, h0_ref,
         o_ref, xpad_sc, xs_sc, y_sc, dt_sc, h_sc) = refs
    else:
        (z_ref, xbc_ref, small_ref, cw_ref, cb_ref, dtb_ref, arow_ref, drow_ref, nw_ref,
         o_ref, hout_ref, xpad_sc, xs_sc, y_sc, dt_sc, h_sc) = refs
    q = SSD_CHUNK
    nc = seq // q
    pad = SUBLANES
    ext = q + 2 * pad

    xpad_sc[0:pad, :] = jnp.zeros((pad, SSD_XBC), F32)
    xpad_sc[pad + seq:2 * pad + seq, :] = jnp.zeros((pad, SSD_XBC), F32)

    def copy_in(c, carry):
        r0 = pl.multiple_of(c * q, q)
        xpad_sc[pl.ds(pad + r0, q), :] = xbc_ref[pl.ds(r0, q), :]
        return carry

    lax.fori_loop(0, nc, copy_in, 0)

    def conv(c, carry):
        r0 = pl.multiple_of(c * q, q)
        for ct in range(SSD_XBC // LANES):
            cols = slice(ct * LANES, (ct + 1) * LANES)
            win = xpad_sc[pl.ds(r0, ext), cols]
            acc = cb_ref[:, cols] + win[pad:pad + q] * cw_ref[2:3, cols]
            for k in (0, 1, 3, 4):
                shifted = pltpu.roll(win, (2 - k) % ext, 0)[pad:pad + q]
                acc = acc + shifted * cw_ref[k:k + 1, cols]
            xs_sc[pl.ds(r0, q), cols] = _silu(acc)
        sd = small_ref[pl.ds(r0, q), :] + dtb_ref[...]
        dt_sc[pl.ds(r0, q), :] = jnp.maximum(sd, 0.0) + jnp.log(1.0 + jnp.exp(-jnp.abs(sd)))
        return carry

    lax.fori_loop(0, nc, conv, 0)

    if latent:
        h_sc[...] = h0_ref[...]
    else:
        h_sc[...] = jnp.zeros(h_sc.shape, F32)

    ri = lax.broadcasted_iota(jnp.int32, (q, q), 0)
    ci = lax.broadcasted_iota(jnp.int32, (q, q), 1)
    lane = lax.broadcasted_iota(jnp.int32, (q, LANES), 1)
    lane_lo = lane < SSD_HEAD_DIM
    row_lo = lax.broadcasted_iota(jnp.int32, (2 * SSD_HEAD_DIM, q), 0) < SSD_HEAD_DIM
    row_lo_n = lax.broadcasted_iota(jnp.int32, (2 * SSD_HEAD_DIM, SSD_STATE), 0) < SSD_HEAD_DIM

    def chunk(c, d):
        keep = (ci <= ri) if d == 0 else (ci >= ri)
        tri = jnp.where(keep, 1.0, 0.0)
        r0 = pl.multiple_of(c * q, q)
        xc = xs_sc[pl.ds(r0, q), 0:BRANCH_W]
        dtc = dt_sc[pl.ds(r0, q), :]
        ac = dtc * arow_ref[...]
        acum = jnp.dot(tri, ac, preferred_element_type=F32, precision=lax.Precision.HIGHEST)
        acum_t = acum.T
        dt_t = dtc.T
        tot = acum[q - 1:q, :] if d == 0 else acum[0:1, :]
        eac = jnp.exp(acum)
        w_t = (jnp.exp(tot - acum) * dtc).T
        etot = jnp.exp(tot)
        x_t = xc.T
        for g in range(2):
            bg = xs_sc[pl.ds(r0, q), BRANCH_W + g * SSD_STATE:BRANCH_W + (g + 1) * SSD_STATE].astype(BF16)
            cg = xs_sc[pl.ds(r0, q), BRANCH_W + 2 * SSD_STATE + g * SSD_STATE:
                       BRANCH_W + 2 * SSD_STATE + (g + 1) * SSD_STATE].astype(BF16)
            cb = _dot_nt(cg, bg)
            for pr in range(2):
                pair = g * 2 + pr
                h0 = pair * 2
                l0 = SMALL_DT + d * SSD_HEADS + h0
                atts = []
                for hh in range(2):
                    ln = l0 + hh
                    seg = _lane_col(acum, ln) - acum_t[ln:ln + 1, :]
                    dec = jnp.exp(jnp.where(keep, seg, -1e30))
                    atts.append(cb * dec * dt_t[ln:ln + 1, :])
                att = jnp.concatenate(atts, axis=1).astype(BF16)
                cols = slice(h0 * SSD_HEAD_DIM, (h0 + 2) * SSD_HEAD_DIM)
                xp = xc[:, cols]
                x2 = jnp.concatenate([jnp.where(lane_lo, xp, 0.0), jnp.where(lane_lo, 0.0, xp)],
                                     axis=0).astype(BF16)
                y_diag = _dot(att, x2)
                st_in = h_sc[d, pair]
                y_off = _dot_nt(cg, st_in.astype(BF16))
                esel = jnp.where(lane_lo, _lane_col(eac, l0), _lane_col(eac, l0 + 1))
                y = y_diag + y_off * esel
                if d == 0:
                    y_sc[pl.ds(r0, q), cols] = y
                else:
                    y_sc[pl.ds(r0, q), cols] = y_sc[pl.ds(r0, q), cols] + y
                wsel = jnp.where(row_lo, w_t[l0:l0 + 1, :], w_t[l0 + 1:l0 + 2, :])
                st_new = _dot((x_t[cols, :] * wsel).astype(BF16), bg)
                dsel = jnp.where(row_lo_n, _lane_col(etot, l0), _lane_col(etot, l0 + 1))
                h_sc[d, pair] = dsel * st_in + st_new

    def fwd(c, carry):
        chunk(c, 0)
        return carry

    def bwd(i, carry):
        chunk(nc - 1 - i, 1)
        return carry

    lax.fori_loop(0, nc, fwd, 0)
    lax.fori_loop(0, nc, bwd, 0)

    def finish(c, carry):
        r0 = pl.multiple_of(c * q, q)
        y = y_sc[pl.ds(r0, q), :] + drow_ref[...] * xs_sc[pl.ds(r0, q), 0:BRANCH_W]
        gated = y * _silu(z_ref[pl.ds(r0, q), :])
        o_ref[pl.ds(r0, q), :] = _rms(gated, nw_ref[...]).astype(BF16)
        return carry

    lax.fori_loop(0, nc, finish, 0)
    if not latent:
        hout_ref[...] = h_sc[...]


def ssd_branch(proj, row0, batch, seq, lp, layer, state=None):
    latent = state is not None
    rb = row0 // seq
    n_pair = SSD_HEADS // 2
    st_shape = (2, n_pair, 2 * SSD_HEAD_DIM, SSD_STATE)
    const = lambda b: (0, 0)
    in_specs = [
        pl.BlockSpec((seq, BRANCH_W), lambda b: (rb + b, C_Z // BRANCH_W)),
        pl.BlockSpec((seq, SSD_XBC), lambda b: (rb + b, C_XBC // SSD_XBC)),
        pl.BlockSpec((seq, LANES), lambda b: (rb + b, C_SMALL // LANES)),
        pl.BlockSpec((SUBLANES, SSD_XBC), const),
        pl.BlockSpec((1, SSD_XBC), const),
        pl.BlockSpec((1, LANES), const),
        pl.BlockSpec((1, LANES), const),
        pl.BlockSpec((1, BRANCH_W), const),
        pl.BlockSpec((1, BRANCH_W), const),
    ]
    args = [proj, proj, proj, lp["conv_w"], lp["conv_b"], lp["dt_bias"], lp["a_row"], lp["d_row"],
            lp["ssd_norm"]]
    out_specs = [pl.BlockSpec((seq, BRANCH_W), lambda b: (b, 0))]
    out_shape = [jax.ShapeDtypeStruct((batch * seq, BRANCH_W), BF16)]
    if latent:
        in_specs.append(pl.BlockSpec((None, None) + st_shape, lambda b: (b, layer, 0, 0, 0, 0)))
        args.append(state)
    else:
        out_specs.append(pl.BlockSpec((None,) + st_shape, lambda b: (b, 0, 0, 0, 0)))
        out_shape.append(jax.ShapeDtypeStruct((batch,) + st_shape, F32))
    res = pl.pallas_call(
        functools.partial(_ssd_kernel, seq=seq, latent=latent),
        grid=(batch,),
        in_specs=in_specs,
        out_specs=out_specs,
        out_shape=out_shape,
        scratch_shapes=[
            pltpu.VMEM((seq + 2 * SUBLANES, SSD_XBC), F32),
            pltpu.VMEM((seq, SSD_XBC), F32),
            pltpu.VMEM((seq, BRANCH_W), F32),
            pltpu.VMEM((seq, LANES), F32),
            pltpu.VMEM(st_shape, F32),
        ],
        compiler_params=_cparams(("parallel",)),
        name="ssd_latent" if latent else "ssd_context",
    )(*args)
    return res if not latent else res[0]


def _diff_kernel(*refs, seq, ctx_len, latent, lam_init):
    if latent:
        (q_ref, k_ref, v_ref, kc_ref, vc_ref, c_ref, s_ref, lam_ref, nw_ref, o_ref, k_sc, v_sc) = refs
    else:
        (q_ref, k_ref, v_ref, lam_ref, nw_ref, o_ref, kout_ref, vout_ref, k_sc, v_sc) = refs
    lk = ctx_len + seq
    half = DIFF_D // 2
    wid = 2 * DIFF_D
    lv = lam_ref[...]
    lam = (jnp.exp(jnp.sum(lv[0:1] * lv[1:2], axis=1, keepdims=True))
           - jnp.exp(jnp.sum(lv[2:3] * lv[3:4], axis=1, keepdims=True)) + lam_init)
    scale = DIFF_D ** -0.5
    lane_lo = lax.broadcasted_iota(jnp.int32, (TQ, LANES), 1) < DIFF_D
    if not latent:
        kout_ref[...] = k_ref[...]
        vout_ref[...] = v_ref[...]

    for h in range(DIFF_HEADS):
        hs = slice(h * wid, (h + 1) * wid)
        if latent:
            k_sc[0:ctx_len, hs] = kc_ref[:, hs].astype(BF16)
            v_sc[0:ctx_len, hs] = vc_ref[:, hs].astype(BF16)
            kn = _rope(k_ref[:, hs], c_ref[...], s_ref[...], half)
        else:
            kn = k_ref[:, hs]
        k_sc[ctx_len:lk, hs] = kn.astype(BF16)
        v_sc[ctx_len:lk, hs] = v_ref[:, hs].astype(BF16)

        def qtile(t, carry, hs=hs):
            q0 = pl.multiple_of(t * TQ, TQ)
            qv = q_ref[pl.ds(q0, TQ), hs]
            if latent:
                qv = _rope(qv, c_ref[pl.ds(q0, TQ), :], s_ref[pl.ds(q0, TQ), :], half)
            q1 = jnp.where(lane_lo, qv, 0.0).astype(BF16)
            q2 = jnp.where(lane_lo, 0.0, qv).astype(BF16)
            k = k_sc[:, hs]
            v = v_sc[:, hs]
            o = _softmax_pv(q1, k, v, scale) - lam * _softmax_pv(q2, k, v, scale)
            o_ref[pl.ds(q0, TQ), hs] = (_rms(o, nw_ref[...]) * (1.0 - lam_init)).astype(BF16)
            return carry

        lax.fori_loop(0, seq // TQ, qtile, 0)


def diff_branch(proj, row0, batch, seq, lp, layer, ctx=None):
    latent = ctx is not None
    rb = row0 // seq
    ctx_len = ctx[0].shape[2] if latent else 0
    lam_init = 0.8 - 0.6 * math.exp(-0.3 * layer)
    wid = 2 * DIFF_D
    bw = BRANCH_W
    in_specs = [
        pl.BlockSpec((seq, bw), lambda b: (rb + b, C_DQ // bw)),
        pl.BlockSpec((seq, bw), lambda b: (rb + b, C_DK // bw)),
        pl.BlockSpec((seq, bw), lambda b: (rb + b, C_DV // bw)),
    ]
    args = [proj, proj, proj]
    if latent:
        in_specs += [
            pl.BlockSpec((None, None, ctx_len, bw), lambda b: (b, layer, 0, 0)),
            pl.BlockSpec((None, None, ctx_len, bw), lambda b: (b, layer, 0, 0)),
            pl.BlockSpec((seq, wid), lambda b: (0, 0)),
            pl.BlockSpec((seq, wid), lambda b: (0, 0)),
        ]
        args += [ctx[0], ctx[1], lp["rope_c32"], lp["rope_s32"]]
    in_specs += [
        pl.BlockSpec((4, DIFF_D), lambda b: (0, 0)),
        pl.BlockSpec((1, wid), lambda b: (0, 0)),
    ]
    args += [lp["diff_lambda"], lp["diff_norm"]]
    out_specs = [pl.BlockSpec((seq, bw), lambda b: (b, 0))]
    out_shape = [jax.ShapeDtypeStruct((batch * seq, bw), BF16)]
    if not latent:
        out_specs += [pl.BlockSpec((seq, bw), lambda b: (b, 0))] * 2
        out_shape += [jax.ShapeDtypeStruct((batch * seq, bw), F32)] * 2
    res = pl.pallas_call(
        functools.partial(_diff_kernel, seq=seq, ctx_len=ctx_len, latent=latent, lam_init=lam_init),
        grid=(batch,),
        in_specs=in_specs,
        out_specs=out_specs,
        out_shape=out_shape,
        scratch_shapes=[pltpu.VMEM((ctx_len + seq, bw), BF16), pltpu.VMEM((ctx_len + seq, bw), BF16)],
        compiler_params=_cparams(("parallel",)),
        name="diff_latent" if latent else "diff_context",
    )(*args)
    return res if not latent else res[0]


def _gqa_kernel(*refs, seq, ctx_len, latent):
    if latent:
        (q_ref, k_ref, v_ref, kc_ref, vc_ref, c_ref, s_ref, qw_ref, kw_ref, o_ref, k_sc, v_sc) = refs
    else:
        (q_ref, k_ref, v_ref, qw_ref, kw_ref, o_ref, kout_ref, vout_ref, k_sc, v_sc) = refs
    lk = ctx_len + seq
    half = GQA_HEAD_DIM // 2
    hd = GQA_HEAD_DIM
    grp = GQA_HEADS // GQA_KV_HEADS
    scale = GQA_HEAD_DIM ** -0.5
    if not latent:
        vout_ref[...] = v_ref[...]

    for kvh in range(GQA_KV_HEADS):
        ks = slice(kvh * hd, (kvh + 1) * hd)
        kn = _rms(k_ref[:, ks], kw_ref[...])
        if latent:
            k_sc[0:ctx_len, ks] = kc_ref[:, ks].astype(BF16)
            v_sc[0:ctx_len, ks] = vc_ref[:, ks].astype(BF16)
            kn = _rope(kn, c_ref[...], s_ref[...], half)
        else:
            kout_ref[:, ks] = kn
        k_sc[ctx_len:lk, ks] = kn.astype(BF16)
        v_sc[ctx_len:lk, ks] = v_ref[:, ks].astype(BF16)

        def qtile(t, carry, kvh=kvh, ks=ks):
            q0 = pl.multiple_of(t * TQ, TQ)
            qs = []
            for g in range(grp):
                c0 = (kvh * grp + g) * hd
                qv = _rms(q_ref[pl.ds(q0, TQ), c0:c0 + hd], qw_ref[...])
                if latent:
                    qv = _rope(qv, c_ref[pl.ds(q0, TQ), :], s_ref[pl.ds(q0, TQ), :], half)
                qs.append(qv)
            qq = jnp.concatenate(qs, axis=0).astype(BF16)
            o = _softmax_pv(qq, k_sc[:, ks], v_sc[:, ks], scale)
            for g in range(grp):
                c0 = (kvh * grp + g) * hd
                o_ref[pl.ds(q0, TQ), c0:c0 + hd] = o[g * TQ:(g + 1) * TQ].astype(BF16)
            return carry

        lax.fori_loop(0, seq // TQ, qtile, 0)


def gqa_branch(proj, row0, batch, seq, lp, layer, ctx=None):
    latent = ctx is not None
    rb = row0 // seq
    ctx_len = ctx[0].shape[2] if latent else 0
    hd = GQA_HEAD_DIM
    qw = GQA_HEADS * hd
    kw = GQA_KV_HEADS * hd
    in_specs = [
        pl.BlockSpec((seq, qw), lambda b: (rb + b, C_GQ // qw)),
        pl.BlockSpec((seq, kw), lambda b: (rb + b, C_GK // kw)),
        pl.BlockSpec((seq, kw), lambda b: (rb + b, C_GV // kw)),
    ]
    args = [proj, proj, proj]
    if latent:
        in_specs += [
            pl.BlockSpec((None, None, ctx_len, kw), lambda b: (b, layer, 0, 0)),
            pl.BlockSpec((None, None, ctx_len, kw), lambda b: (b, layer, 0, 0)),
            pl.BlockSpec((seq, hd), lambda b: (0, 0)),
            pl.BlockSpec((seq, hd), lambda b: (0, 0)),
        ]
        args += [ctx[0], ctx[1], lp["rope_c64"], lp["rope_s64"]]
    in_specs += [pl.BlockSpec((1, hd), lambda b: (0, 0)), pl.BlockSpec((1, hd), lambda b: (0, 0))]
    args += [lp["gqa_q_norm"], lp["gqa_k_norm"]]
    out_specs = [pl.BlockSpec((seq, qw), lambda b: (b, 0))]
    out_shape = [jax.ShapeDtypeStruct((batch * seq, qw), BF16)]
    if not latent:
        out_specs += [pl.BlockSpec((seq, kw), lambda b: (b, 0))] * 2
        out_shape += [jax.ShapeDtypeStruct((batch * seq, kw), F32)] * 2
    res = pl.pallas_call(
        functools.partial(_gqa_kernel, seq=seq, ctx_len=ctx_len, latent=latent),
        grid=(batch,),
        in_specs=in_specs,
        out_specs=out_specs,
        out_shape=out_shape,
        scratch_shapes=[pltpu.VMEM((ctx_len + seq, kw), BF16), pltpu.VMEM((ctx_len + seq, kw), BF16)],
        compiler_params=_cparams(("parallel",)),
        name="gqa_latent" if latent else "gqa_context",
    )(*args)
    return res if not latent else res[0]


def _mla_kernel(*refs, seq, ctx_len, latent):
    if latent:
        (cq_ref, ckv_ref, small_ref, ckvc_ref, krc_ref, c_ref, s_ref, qw_ref, kvw_ref, wuq_ref, wuk_ref,
         wuv_ref, o_ref, kn_sc, v_sc, kr_sc) = refs
    else:
        (cq_ref, ckv_ref, small_ref, qw_ref, kvw_ref, wuq_ref, wuk_ref, wuv_ref,
         o_ref, ckv_out_ref, kr_out_ref, kn_sc, v_sc, kr_sc) = refs
    lk = ctx_len + seq
    half = MLA_ROPE // 2
    ckv = _rms(ckv_ref[...], kvw_ref[...])
    if latent:
        cc = ckvc_ref[...].astype(BF16)
        kn_sc[0:ctx_len, :] = _dot(cc, wuk_ref[...]).astype(BF16)
        v_sc[0:ctx_len, :] = _dot(cc, wuv_ref[...]).astype(BF16)
        kr_sc[0:ctx_len, :] = krc_ref[...].astype(BF16)
    else:
        ckv_out_ref[...] = ckv
        kr_out_ref[...] = small_ref[:, 0:MLA_ROPE]
    cb = ckv.astype(BF16)
    kn_sc[ctx_len:lk, :] = _dot(cb, wuk_ref[...]).astype(BF16)
    v_sc[ctx_len:lk, :] = _dot(cb, wuv_ref[...]).astype(BF16)
    kr = small_ref[...]
    if latent:
        kr = _rope(kr, c_ref[...], s_ref[...], half)
    lane_lo = lax.broadcasted_iota(jnp.int32, (seq, LANES), 1) < MLA_ROPE
    kr_sc[ctx_len:lk, :] = jnp.where(lane_lo, kr, 0.0).astype(BF16)
    scale = (MLA_NOPE + MLA_ROPE) ** -0.5
    nope_w = MLA_HEADS * MLA_NOPE

    def qtile(t, carry):
        q0 = pl.multiple_of(t * TQ, TQ)
        cq = _dot(_rms(cq_ref[pl.ds(q0, TQ), :], qw_ref[...]).astype(BF16), wuq_ref[...])
        for h in range(MLA_HEADS):
            hs = slice(h * MLA_NOPE, (h + 1) * MLA_NOPE)
            qn = cq[:, hs].astype(BF16)
            qr = cq[:, nope_w + h * LANES:nope_w + (h + 1) * LANES]
            if latent:
                qr = _rope(qr, c_ref[pl.ds(q0, TQ), :], s_ref[pl.ds(q0, TQ), :], half)
            s = (_dot_nt(qn, kn_sc[:, hs]) + _dot_nt(qr.astype(BF16), kr_sc[...])) * scale
            m = jnp.max(s, axis=-1, keepdims=True)
            p = jnp.exp(s - m)
            l = jnp.sum(p, axis=-1, keepdims=True)
            o = _dot(p.astype(BF16), v_sc[:, hs]) / l
            o_ref[pl.ds(q0, TQ), hs] = o.astype(BF16)
        return carry

    lax.fori_loop(0, seq // TQ, qtile, 0)


def mla_branch(proj, row0, batch, seq, lp, layer, ctx=None):
    latent = ctx is not None
    rb = row0 // seq
    ctx_len = ctx[0].shape[2] if latent else 0
    const = lambda b: (0, 0)
    in_specs = [
        pl.BlockSpec((seq, MLA_Q_RANK), lambda b: (rb + b, C_MCQ // MLA_Q_RANK)),
        pl.BlockSpec((seq, MLA_KV_RANK), lambda b: (rb + b, C_MCKV // MLA_KV_RANK)),
        pl.BlockSpec((seq, LANES), lambda b: (rb + b, C_SMALL // LANES)),
    ]
    args = [proj, proj, proj]
    if latent:
        in_specs += [
            pl.BlockSpec((None, None, ctx_len, MLA_KV_RANK), lambda b: (b, layer, 0, 0)),
            pl.BlockSpec((None, None, ctx_len, LANES), lambda b: (b, layer, 0, 0)),
            pl.BlockSpec((seq, LANES), const),
            pl.BlockSpec((seq, LANES), const),
        ]
        args += [ctx[0], ctx[1], lp["rope_c32"], lp["rope_s32"]]
    in_specs += [
        pl.BlockSpec((1, MLA_Q_RANK), const),
        pl.BlockSpec((1, MLA_KV_RANK), const),
        pl.BlockSpec((MLA_Q_RANK, 2 * MLA_HEADS * LANES), const),
        pl.BlockSpec((MLA_KV_RANK, MLA_HEADS * MLA_NOPE), const),
        pl.BlockSpec((MLA_KV_RANK, BRANCH_W), const),
    ]
    args += [lp["mla_q_norm"], lp["mla_kv_norm"], lp["w_uq"], lp["w_uk"], lp["w_uv"]]
    out_specs = [pl.BlockSpec((seq, BRANCH_W), lambda b: (b, 0))]
    out_shape = [jax.ShapeDtypeStruct((batch * seq, BRANCH_W), BF16)]
    if not latent:
        out_specs += [pl.BlockSpec((seq, MLA_KV_RANK), lambda b: (b, 0)),
                      pl.BlockSpec((seq, MLA_ROPE), lambda b: (b, 0))]
        out_shape += [jax.ShapeDtypeStruct((batch * seq, MLA_KV_RANK), F32),
                      jax.ShapeDtypeStruct((batch * seq, MLA_ROPE), F32)]
    lk = ctx_len + seq
    res = pl.pallas_call(
        functools.partial(_mla_kernel, seq=seq, ctx_len=ctx_len, latent=latent),
        grid=(batch,),
        in_specs=in_specs,
        out_specs=out_specs,
        out_shape=out_shape,
        scratch_shapes=[pltpu.VMEM((lk, MLA_HEADS * MLA_NOPE), BF16), pltpu.VMEM((lk, BRANCH_W), BF16),
                        pltpu.VMEM((lk, LANES), BF16)],
        compiler_params=_cparams(("parallel",)),
        name="mla_latent" if latent else "mla_context",
    )(*args)
    return res if not latent else res[0]


def _pack_pairs(lo, hi):
    lo_bits = lax.bitcast_convert_type(lo.astype(BF16).astype(F32), jnp.uint32)
    hi_bits = lax.bitcast_convert_type(hi.astype(BF16).astype(F32), jnp.uint32)
    return (lo_bits >> jnp.uint32(16)) | (hi_bits & jnp.uint32(0xFFFF0000))


def _unpack_pairs(w):
    lo = lax.bitcast_convert_type(w << jnp.uint32(16), F32)
    hi = lax.bitcast_convert_type(w & jnp.uint32(0xFFFF0000), F32)
    return lo, hi


def _merge_kernel(*refs, n_ctx_tiles):
    oc_refs, ol_refs, g_refs = refs[0:4], refs[4:8], refs[8:12]
    (wb_ref, wo_ref, x_ref, ga_ref, shf_ref, scf_ref, lg_ref, lb_ref, rw_ref, rb_ref,
     x1_ref, u2_ref, lo_ref, acc_sc) = refs[12:]
    i = pl.program_id(0)
    j = pl.program_id(1)

    @pl.when(j == 0)
    def _():
        acc_sc[...] = jnp.zeros(acc_sc.shape, F32)

    is_ctx = i < n_ctx_tiles
    merged = None
    for n in range(N_BRANCH):
        o = jnp.where(is_ctx, oc_refs[n][...], ol_refs[n][...])
        term = _sigmoid(g_refs[n][...]) * _dot(o, wb_ref[n])
        merged = term if merged is None else merged + term
    acc_sc[...] += _dot(merged.astype(BF16), wo_ref[...])

    @pl.when(j == pl.num_programs(1) - 1)
    def _():
        h = DEEPNORM_ALPHA * x_ref[...] + ga_ref[...] * acc_sc[...]
        x1 = _ln(h) * lg_ref[...] + lb_ref[...]
        x1_ref[...] = x1
        u2 = (_ln(x1) * (1.0 + scf_ref[...]) + shf_ref[...]).astype(BF16)
        half = D_MODEL // 2
        u2f = u2.astype(F32)
        u2_ref[...] = _pack_pairs(u2f[:, :half], u2f[:, half:])
        lo_ref[...] = _dot(u2, rw_ref[...]) + rb_ref[...]


def merge_and_route(outs_ctx, outs_lat, proj, x, mods, lp, layer, n_ctx_rows, lat_len):
    t = x.shape[0]
    tm, tn = TM_MERGE, TN_MERGE
    n_ctx_tiles = n_ctx_rows // tm
    row = functools.partial(_mod_row, tm=tm, n_ctx_rows=n_ctx_rows, lat_len=lat_len)
    gate_spec = lambda n: pl.BlockSpec((tm, tn), lambda i, j: (i, (C_GATE + n * D_MODEL) // tn + j))
    vec = lambda k: pl.BlockSpec((None, 1, D_MODEL), lambda i, j: (row(i), 0, k))
    cvec = pl.BlockSpec((None, 1, D_MODEL), lambda i, j: (layer, 0, 0))
    in_specs = (
        [pl.BlockSpec((tm, BRANCH_W), lambda i, j: (jnp.minimum(i, n_ctx_tiles - 1), 0))] * N_BRANCH
        + [pl.BlockSpec((tm, BRANCH_W), lambda i, j: (jnp.maximum(i - n_ctx_tiles, 0), 0))] * N_BRANCH
        + [gate_spec(n) for n in range(N_BRANCH)]
        + [
            pl.BlockSpec((None, N_BRANCH, BRANCH_W, tn), lambda i, j: (layer, 0, 0, j)),
            pl.BlockSpec((None, tn, D_MODEL), lambda i, j: (layer, j, 0)),
            pl.BlockSpec((tm, D_MODEL), lambda i, j: (i, 0)),
            vec(2), vec(3), vec(4), cvec, cvec,
            pl.BlockSpec((None, D_MODEL, LANES), lambda i, j: (layer, 0, 0)),
            pl.BlockSpec((None, 1, LANES), lambda i, j: (layer, 0, 0)),
        ]
    )
    return pl.pallas_call(
        functools.partial(_merge_kernel, n_ctx_tiles=n_ctx_tiles),
        grid=(t // tm, D_MODEL // tn),
        in_specs=in_specs,
        out_specs=[
            pl.BlockSpec((tm, D_MODEL), lambda i, j: (i, 0)),
            pl.BlockSpec((tm, D_MODEL // 2), lambda i, j: (i, 0)),
            pl.BlockSpec((tm, LANES), lambda i, j: (i, 0)),
        ],
        out_shape=[
            jax.ShapeDtypeStruct((t, D_MODEL), F32),
            jax.ShapeDtypeStruct((t, D_MODEL // 2), jnp.uint32),
            jax.ShapeDtypeStruct((t, LANES), F32),
        ],
        scratch_shapes=[pltpu.VMEM((tm, D_MODEL), F32)],
        compiler_params=_cparams(("parallel", "arbitrary")),
        name="merge_route",
    )(*outs_ctx, *outs_lat, proj, proj, proj, proj, lp["w_branch"], lp["w_out"], x, mods, mods, mods,
      lp["ln_mix_g"], lp["ln_mix_b"], lp["router_w"], lp["router_b"])


def _moe_up_kernel(e_ref, src_ref, valid_ref, first_ref, x_ref, wg_ref, wl_ref, bg_ref, bl_ref, h_ref,
                   wg_sc, wl_sc):
    blk = pl.program_id(0)

    @pl.when(first_ref[blk] == 1)
    def _():
        wg_sc[...] = wg_ref[...].astype(BF16)
        wl_sc[...] = wl_ref[...].astype(BF16)

    @pl.when(valid_ref[blk] == 1)
    def _():
        lo, hi = _unpack_pairs(x_ref[...])
        x = jnp.concatenate([lo, hi], axis=1).astype(BF16)
        g = jnp.minimum(_dot(x, wg_sc[...]) + bg_ref[...], SWIGLU_LIMIT)
        lin = jnp.clip(_dot(x, wl_sc[...]) + bl_ref[...], -SWIGLU_LIMIT, SWIGLU_LIMIT)
        h_ref[...] = ((lin + 1.0) * (g * _sigmoid(SWIGLU_ALPHA * g))).astype(BF16)

    @pl.when(valid_ref[blk] == 0)
    def _():
        h_ref[...] = jnp.zeros(h_ref.shape, BF16)


def _moe_down_kernel(e_ref, src_ref, valid_ref, first_ref, h_ref, wd_ref, bd_ref, y_ref, wd_sc):
    blk = pl.program_id(0)

    @pl.when(first_ref[blk] == 1)
    def _():
        wd_sc[...] = wd_ref[...].astype(BF16)

    @pl.when(valid_ref[blk] == 1)
    def _():
        y = _dot(h_ref[...], wd_sc[...]) + bd_ref[...]
        half = D_MODEL // 2
        y_ref[...] = _pack_pairs(y[:, :half], y[:, half:])

    @pl.when(valid_ref[blk] == 0)
    def _():
        y_ref[...] = jnp.zeros(y_ref.shape, jnp.uint32)


def moe_experts(xs, plan, w_gu, b_gu, w_down, b_down, layer):
    p = xs.shape[0]
    nb = p // TM_MOE
    tm = TM_MOE
    half = D_MODEL // 2
    hidden = pl.pallas_call(
        _moe_up_kernel,
        grid_spec=pltpu.PrefetchScalarGridSpec(
            num_scalar_prefetch=4,
            grid=(nb,),
            in_specs=[
                pl.BlockSpec((tm, half), lambda b, e, s, v, fi: (s[b], 0)),
                pl.BlockSpec((None, None, D_MODEL, EXPERT_FF), lambda b, e, s, v, fi: (layer, e[b], 0, 0)),
                pl.BlockSpec((None, None, D_MODEL, EXPERT_FF), lambda b, e, s, v, fi: (layer, e[b], 0, 1)),
                pl.BlockSpec((None, None, 1, EXPERT_FF), lambda b, e, s, v, fi: (layer, e[b], 0, 0)),
                pl.BlockSpec((None, None, 1, EXPERT_FF), lambda b, e, s, v, fi: (layer, e[b], 0, 1)),
            ],
            out_specs=pl.BlockSpec((tm, EXPERT_FF), lambda b, e, s, v, fi: (b, 0)),
            scratch_shapes=[pltpu.VMEM((D_MODEL, EXPERT_FF), BF16), pltpu.VMEM((D_MODEL, EXPERT_FF), BF16)],
        ),
        out_shape=jax.ShapeDtypeStruct((p, EXPERT_FF), BF16),
        compiler_params=_cparams(("arbitrary",)),
        name="moe_up",
    )(*plan, xs, w_gu, w_gu, b_gu, b_gu)
    return pl.pallas_call(
        _moe_down_kernel,
        grid_spec=pltpu.PrefetchScalarGridSpec(
            num_scalar_prefetch=4,
            grid=(nb,),
            in_specs=[
                pl.BlockSpec((tm, EXPERT_FF), lambda b, e, s, v, fi: (s[b], 0)),
                pl.BlockSpec((None, None, EXPERT_FF, D_MODEL), lambda b, e, s, v, fi: (layer, e[b], 0, 0)),
                pl.BlockSpec((None, None, 1, D_MODEL), lambda b, e, s, v, fi: (layer, e[b], 0, 0)),
            ],
            out_specs=pl.BlockSpec((tm, half), lambda b, e, s, v, fi: (b, 0)),
            scratch_shapes=[pltpu.VMEM((EXPERT_FF, D_MODEL), BF16)],
        ),
        out_shape=jax.ShapeDtypeStruct((p, half), jnp.uint32),
        compiler_params=_cparams(("arbitrary",)),
        name="moe_down",
    )(*plan, hidden, w_down, b_down)


def route(logits):
    t = logits.shape[0]
    a = t * TOP_K
    top_v, top_i = lax.top_k(logits, TOP_K)
    gate = jax.nn.softmax(top_v, axis=-1)
    flat_e = top_i.reshape(-1)
    onehot = (flat_e[:, None] == jnp.arange(N_EXPERTS, dtype=jnp.int32)[None, :]).astype(jnp.int32)
    csum = jnp.cumsum(onehot, axis=0)
    counts = csum[-1]
    padded = (counts + TM_MOE - 1) // TM_MOE * TM_MOE
    ends = jnp.cumsum(padded)
    slot = jnp.sum(onehot * (csum - 1 + (ends - padded)[None, :]), axis=1)
    nb = a // TM_MOE + N_EXPERTS
    blk = jnp.arange(nb, dtype=jnp.int32)
    n_used = (ends[-1] // TM_MOE).astype(jnp.int32)
    src = jnp.minimum(blk, n_used - 1)
    blk_e = jnp.minimum(jnp.searchsorted(ends, src * TM_MOE, side="right"), N_EXPERTS - 1).astype(jnp.int32)
    valid = (blk < n_used).astype(jnp.int32)
    first = jnp.where((blk == 0) | (blk_e != jnp.roll(blk_e, 1)), 1, 0).astype(jnp.int32)
    key_bits = 16
    assert a < (1 << key_bits)
    skey = lax.sort(flat_e * (1 << key_bits) + jnp.arange(a, dtype=jnp.int32))
    sorted_t = (skey & ((1 << key_bits) - 1)) // TOP_K
    start = jnp.cumsum(counts) - counts
    blk_off = blk * TM_MOE - (ends - padded)[blk_e]
    base = start[blk_e] + blk_off
    room = jnp.where(valid == 1, counts[blk_e] - blk_off, 0)
    r = jnp.arange(TM_MOE, dtype=jnp.int32)[None, :]
    live = r < room[:, None]
    filler = (blk[:, None] * TM_MOE + r) % t
    tok = jnp.where(live, sorted_t[jnp.clip(base[:, None] + r, 0, a - 1)], filler).reshape(-1)
    tok = lax.optimization_barrier(tok)
    return gate, slot.reshape(t, TOP_K), tok, (blk_e, src, valid, first)


def _ffn_norm_kernel(x_ref, y_ref, gate_ref, gf_ref, lg_ref, lb_ref, o_ref):
    gate = gate_ref[...]
    ffn_lo = ffn_hi = None
    for k in range(TOP_K):
        lo, hi = _unpack_pairs(y_ref[k])
        gk = gate[:, k:k + 1]
        ffn_lo = lo * gk if ffn_lo is None else ffn_lo + lo * gk
        ffn_hi = hi * gk if ffn_hi is None else ffn_hi + hi * gk
    ffn = jnp.concatenate([ffn_lo, ffn_hi], axis=1)
    h = DEEPNORM_ALPHA * x_ref[...] + gf_ref[...] * ffn
    o_ref[...] = _ln(h) * lg_ref[...] + lb_ref[...]


def ffn_norm(x1, y_rows, gate, mods, lp, layer, n_ctx_rows, lat_len):
    t = x1.shape[0]
    tm = TM_LN
    row = functools.partial(_mod_row, tm=tm, n_ctx_rows=n_ctx_rows, lat_len=lat_len)
    cvec = pl.BlockSpec((None, 1, D_MODEL), lambda i: (layer, 0, 0))
    return pl.pallas_call(
        _ffn_norm_kernel,
        grid=(t // tm,),
        in_specs=[
            pl.BlockSpec((tm, D_MODEL), lambda i: (i, 0)),
            pl.BlockSpec((TOP_K, tm, D_MODEL // 2), lambda i: (0, i, 0)),
            pl.BlockSpec((tm, TOP_K), lambda i: (i, 0)),
            pl.BlockSpec((None, 1, D_MODEL), lambda i: (row(i), 0, 5)),
            cvec, cvec,
        ],
        out_specs=pl.BlockSpec((tm, D_MODEL), lambda i: (i, 0)),
        out_shape=jax.ShapeDtypeStruct((t, D_MODEL), F32),
        compiler_params=_cparams(("parallel",)),
        name="ffn_norm",
    )(x1, y_rows, gate, mods, lp["ln_ffn_g"], lp["ln_ffn_b"])


def _rope_tables(n_tokens, seg, width):
    rows = n_tokens // GRID_W
    row = jnp.repeat(jnp.arange(rows, dtype=F32), GRID_W)
    col = jnp.tile(jnp.arange(GRID_W, dtype=F32), rows)
    n_freq = seg // 4
    inv = ROPE_THETA ** (-jnp.arange(n_freq, dtype=F32) / n_freq)
    ang = jnp.concatenate([row[:, None] * inv, col[:, None] * inv], axis=-1)
    cos, sin = jnp.cos(ang), jnp.sin(ang)
    reps = width // seg
    return (jnp.tile(jnp.concatenate([cos, cos], axis=-1), (1, reps)),
            jnp.tile(jnp.concatenate([-sin, sin], axis=-1), (1, reps)))


def _pack_w_in_kernel(w_ref, o_ref):
    def put(dst, src, n):
        for t0 in range(0, n, LANES):
            o_ref[:, dst + t0:dst + t0 + LANES] = w_ref[src + t0:src + t0 + LANES, :].T.astype(BF16)

    put(C_XBC, BRANCH_W, SSD_XBC)
    put(C_Z, 0, BRANCH_W)
    put(C_DQ, _O_DQ, _O_MCKV - _O_DQ)
    put(C_GQ, _O_GQ, _O_GK - _O_GQ)
    put(C_MCKV, _O_MCKV, MLA_KV_RANK)
    put(C_GK, _O_GK, _O_GATE - _O_GK)
    n_dt = 2 * SSD_HEADS
    small = jnp.concatenate([w_ref[_O_MKR:_O_MKR + MLA_ROPE, :], w_ref[_O_DT:_O_DT + n_dt, :],
                             jnp.zeros((LANES - MLA_ROPE - n_dt, TR_PACK), F32)], axis=0)
    o_ref[:, C_SMALL:C_SMALL + LANES] = small.T.astype(BF16)
    o_ref[:, C_SMALL + LANES:C_GATE] = jnp.zeros((TR_PACK, C_GATE - C_SMALL - LANES), BF16)
    put(C_GATE, _O_GATE, N_BRANCH * D_MODEL)


def _pack_w_in(w_in):
    tr = TR_PACK
    return pl.pallas_call(
        _pack_w_in_kernel,
        grid=(DEPTH, D_MODEL // tr),
        in_specs=[pl.BlockSpec((None, IN_WIDTH, tr), lambda l, r: (l, 0, r))],
        out_specs=pl.BlockSpec((None, tr, PROJ_W), lambda l, r: (l, r, 0)),
        out_shape=jax.ShapeDtypeStruct((DEPTH, D_MODEL, PROJ_W), BF16),
        compiler_params=_cparams(("parallel", "parallel")),
        name="pack_w_in",
    )(jnp.transpose(w_in, (0, 2, 1)))


def _pack_w_uq(w_uq):
    w = w_uq.reshape(MLA_Q_RANK, MLA_HEADS, MLA_NOPE + MLA_ROPE)
    nope = w[:, :, :MLA_NOPE].reshape(MLA_Q_RANK, MLA_HEADS * MLA_NOPE)
    rope = jnp.pad(w[:, :, MLA_NOPE:], ((0, 0), (0, 0), (0, LANES - MLA_ROPE)))
    return jnp.concatenate([nope, rope.reshape(MLA_Q_RANK, MLA_HEADS * LANES)], axis=-1).astype(BF16)


def _small_row(v16):
    return jnp.zeros((1, LANES), F32).at[0, SMALL_DT:SMALL_DT + 2 * SSD_HEADS].set(v16.reshape(-1))


def kernel(x_prompt, x_sample, state_ssd, cache_diff_k, cache_diff_v, cache_mla_ckv, cache_mla_kr, cache_gqa_k, cache_gqa_v, c, c_ctx, w_mod, b_mod, w_in, ssd_conv_w, ssd_conv_b, ssd_dt_bias, ssd_a_log, ssd_d, ssd_norm, diff_lambda, diff_norm, mla_q_norm, mla_w_uq, mla_kv_norm, mla_w_uk, mla_w_uv, gqa_q_norm, gqa_k_norm, w_branch, w_out, ln_mix_g, ln_mix_b, router_w, router_b, exp_w_gu, exp_b_gu, exp_w_down, exp_b_down, ln_ffn_g, ln_ffn_b):
    nb_ctx, len_ctx, _ = x_prompt.shape
    nb_lat, len_lat, _ = x_sample.shape
    past = cache_diff_k.shape[2]
    n_ctx_rows = nb_ctx * len_ctx
    t = n_ctx_rows + nb_lat * len_lat

    cvec = jnp.zeros((SUBLANES, D_MODEL), F32).at[0].set(c_ctx).at[1:1 + nb_lat].set(c)
    mods_all = modulation(cvec, w_mod, b_mod)

    w_in_p = _pack_w_in(w_in)
    w_branch_b = w_branch.astype(BF16)
    w_out_b = w_out.astype(BF16)
    router_w_p = jnp.pad(router_w, ((0, 0), (0, 0), (0, LANES - N_EXPERTS))).astype(BF16)
    router_b_p = jnp.pad(router_b, ((0, 0), (0, LANES - N_EXPERTS))).reshape(DEPTH, 1, LANES)
    c32, s32 = _rope_tables(len_lat, 2 * (DIFF_D // 2), LANES)
    c64, s64 = _rope_tables(len_lat, GQA_HEAD_DIM, LANES)
    n_pair = SSD_HEADS // 2
    state_in = state_ssd.reshape(nb_lat, DEPTH, 2, n_pair, 2 * SSD_HEAD_DIM, SSD_STATE)
    diff_kc = cache_diff_k.reshape(nb_lat, DEPTH, past, BRANCH_W)
    diff_vc = cache_diff_v.reshape(nb_lat, DEPTH, past, BRANCH_W)
    mla_krc = jnp.pad(cache_mla_kr, ((0, 0), (0, 0), (0, 0), (0, LANES - MLA_ROPE)))
    gqa_kc = cache_gqa_k.reshape(nb_lat, DEPTH, past, GQA_KV_HEADS * GQA_HEAD_DIM)
    gqa_vc = cache_gqa_v.reshape(nb_lat, DEPTH, past, GQA_KV_HEADS * GQA_HEAD_DIM)
    vecs = dict(ln_mix_g=ln_mix_g.reshape(DEPTH, 1, D_MODEL), ln_mix_b=ln_mix_b.reshape(DEPTH, 1, D_MODEL),
                ln_ffn_g=ln_ffn_g.reshape(DEPTH, 1, D_MODEL), ln_ffn_b=ln_ffn_b.reshape(DEPTH, 1, D_MODEL),
                w_branch=w_branch_b, w_out=w_out_b, router_w=router_w_p, router_b=router_b_p)
    b_gu = exp_b_gu.reshape(DEPTH, N_EXPERTS, 1, 2 * EXPERT_FF)
    b_down = exp_b_down.reshape(DEPTH, N_EXPERTS, 1, D_MODEL)

    x = jnp.concatenate([x_prompt.reshape(n_ctx_rows, D_MODEL), x_sample.reshape(-1, D_MODEL)], axis=0)
    new_state, new_dk, new_dv, new_ckv, new_kr, new_gk, new_gv = [], [], [], [], [], [], []
    for l in range(DEPTH):
        lp = dict(vecs)
        lp.update(
            conv_w=jnp.pad(ssd_conv_w[l], ((0, SUBLANES - SSD_CONV), (0, 0))),
            conv_b=ssd_conv_b[l].reshape(1, SSD_XBC),
            dt_bias=_small_row(ssd_dt_bias[l]),
            a_row=_small_row(-jnp.exp(ssd_a_log[l])),
            d_row=jnp.repeat(ssd_d[l], SSD_HEAD_DIM).reshape(1, BRANCH_W),
            ssd_norm=ssd_norm[l].reshape(1, BRANCH_W),
            diff_lambda=diff_lambda[l],
            diff_norm=diff_norm[l].reshape(1, 2 * DIFF_D),
            mla_q_norm=mla_q_norm[l].reshape(1, MLA_Q_RANK),
            mla_kv_norm=mla_kv_norm[l].reshape(1, MLA_KV_RANK),
            w_uq=_pack_w_uq(mla_w_uq[l]),
            w_uk=mla_w_uk[l].astype(BF16),
            w_uv=mla_w_uv[l].astype(BF16),
            gqa_q_norm=gqa_q_norm[l].reshape(1, GQA_HEAD_DIM),
            gqa_k_norm=gqa_k_norm[l].reshape(1, GQA_HEAD_DIM),
            rope_c32=c32, rope_s32=s32, rope_c64=c64, rope_s64=s64,
        )
        mods = mods_all[l].reshape(SUBLANES, 1, 6 * D_MODEL)
        proj = in_projection(x, mods, w_in_p, l, n_ctx_rows, len_lat)

        o_ssd_c, st = ssd_branch(proj, 0, nb_ctx, len_ctx, lp, l)
        o_ssd_l = ssd_branch(proj, n_ctx_rows, nb_lat, len_lat, lp, l, state_in)
        o_diff_c, dk_c, dv_c = diff_branch(proj, 0, nb_ctx, len_ctx, lp, l)
        o_diff_l = diff_branch(proj, n_ctx_rows, nb_lat, len_lat, lp, l, (diff_kc, diff_vc))
        o_mla_c, ckv_c, kr_c = mla_branch(proj, 0, nb_ctx, len_ctx, lp, l)
        o_mla_l = mla_branch(proj, n_ctx_rows, nb_lat, len_lat, lp, l, (cache_mla_ckv, mla_krc))
        o_gqa_c, gk_c, gv_c = gqa_branch(proj, 0, nb_ctx, len_ctx, lp, l)
        o_gqa_l = gqa_branch(proj, n_ctx_rows, nb_lat, len_lat, lp, l, (gqa_kc, gqa_vc))
        outs_ctx = (o_ssd_c, o_diff_c, o_mla_c, o_gqa_c)
        outs_lat = (o_ssd_l, o_diff_l, o_mla_l, o_gqa_l)

        new_state.append(st.reshape(nb_ctx, 2, SSD_HEADS, SSD_HEAD_DIM, SSD_STATE))
        new_dk.append(dk_c.reshape(nb_ctx, len_ctx, DIFF_HEADS, 2 * DIFF_D))
        new_dv.append(dv_c.reshape(nb_ctx, len_ctx, DIFF_HEADS, 2 * DIFF_D))
        new_ckv.append(ckv_c.reshape(nb_ctx, len_ctx, MLA_KV_RANK))
        new_kr.append(kr_c.reshape(nb_ctx, len_ctx, MLA_ROPE))
        new_gk.append(gk_c.reshape(nb_ctx, len_ctx, GQA_KV_HEADS, GQA_HEAD_DIM))
        new_gv.append(gv_c.reshape(nb_ctx, len_ctx, GQA_KV_HEADS, GQA_HEAD_DIM))

        x1, u2, logits = merge_and_route(outs_ctx, outs_lat, proj, x, mods, lp, l, n_ctx_rows, len_lat)
        gate, slot, tok, plan = route(logits[:, :N_EXPERTS])
        u2 = jnp.pad(u2, ((0, max(0, SC_GATHER_MIN_ROWS - t)), (0, 0)))
        yb = moe_experts(u2[tok], plan, exp_w_gu, b_gu, exp_w_down, b_down, l)
        y_rows = yb[slot.T.reshape(-1)].reshape(TOP_K, t, D_MODEL // 2)
        x = ffn_norm(x1, y_rows, gate, mods, lp, l, n_ctx_rows, len_lat)

    y_prompt = x[:n_ctx_rows].reshape(nb_ctx, len_ctx, D_MODEL)
    y_sample = x[n_ctx_rows:].reshape(nb_lat, len_lat, D_MODEL)
    stack = lambda parts: jnp.stack(parts, axis=1)
    return (y_prompt, y_sample, stack(new_state), stack(new_dk), stack(new_dv), stack(new_ckv),
            stack(new_kr), stack(new_gk), stack(new_gv))
```

```python
import functools
import math

import jax
import jax.numpy as jnp
from jax import lax
from jax.experimental import pallas as pl
from jax.experimental.pallas import tpu as pltpu

F32 = jnp.float32
BF16 = jnp.bfloat16

D_MODEL = 2048
DEPTH = 4
GRID_W = 64
ROPE_THETA = 10000.0
BRANCH_W = 512
N_BRANCH = 4
SSD_HEADS = 8
SSD_HEAD_DIM = 64
SSD_STATE = 128
SSD_CONV = 5
SSD_CHUNK = 128
SSD_XBC = 1024
DIFF_HEADS = 4
DIFF_D = 64
MLA_HEADS = 4
MLA_NOPE = 128
MLA_ROPE = 64
MLA_Q_RANK = 512
MLA_KV_RANK = 256
GQA_HEAD_DIM = 128
GQA_HEADS = 4
GQA_KV_HEADS = 2
N_EXPERTS = 32
TOP_K = 4
EXPERT_FF = 1024
SWIGLU_LIMIT = 7.0
SWIGLU_ALPHA = 1.702
DEEPNORM_ALPHA = (2.0 * DEPTH) ** 0.25
EPS = 1e-6

LANES = 128
SUBLANES = 8

_O_DT = 1536
_O_DQ = 1552
_O_MCKV = 3600
_O_MKR = 3856
_O_GQ = 3920
_O_GK = 4432
_O_GATE = 4944
IN_WIDTH = 13136
C_XBC = 0
C_Z = 1024
C_DQ = 1536
C_DK = 2048
C_DV = 2560
C_MCQ = 3072
C_GQ = 3584
C_MCKV = 4096
C_GK = 4352
C_GV = 4608
C_SMALL = 4864
C_GATE = 5120
PROJ_W = C_GATE + N_BRANCH * D_MODEL
SMALL_DT = MLA_ROPE

VMEM_LIMIT = 60000 * 1024

TM_IN = 1024
TN_IN = 1024
TM_MERGE = 512
TN_MERGE = 512
TM_LN = 512
TM_MOE = 512
N_WPIECE = 4
TN_MOD = 1024
TR_PACK = 128
TQ = 256
SC_GATHER_MIN_ROWS = 16384


def _cparams(sem):
    return pltpu.CompilerParams(dimension_semantics=sem, vmem_limit_bytes=VMEM_LIMIT)


def _dot(a, b):
    return jnp.dot(a, b, preferred_element_type=F32)


def _dot_nt(a, b):
    return lax.dot_general(a, b, (((1,), (1,)), ((), ())), preferred_element_type=F32)


def _ln(x):
    mu = jnp.mean(x, axis=-1, keepdims=True)
    xc = x - mu
    var = jnp.mean(xc * xc, axis=-1, keepdims=True)
    return xc * lax.rsqrt(var + EPS)


def _rms(x, g):
    return x * lax.rsqrt(jnp.mean(x * x, axis=-1, keepdims=True) + EPS) * g


def _sigmoid(x):
    return 0.5 * jnp.tanh(0.5 * x) + 0.5


def _silu(x):
    return x * _sigmoid(x)


def _rope(x, c, s, half):
    w = x.shape[-1]
    lane = lax.broadcasted_iota(jnp.int32, x.shape, 1)
    lo = (lane & (2 * half - 1)) < half
    swapped = jnp.where(lo, pltpu.roll(x, w - half, 1), pltpu.roll(x, half, 1))
    return x * c + swapped * s


def _softmax_pv(q, k, v, scale):
    s = _dot_nt(q, k) * scale
    m = jnp.max(s, axis=-1, keepdims=True)
    p = jnp.exp(s - m)
    l = jnp.sum(p, axis=-1, keepdims=True)
    return _dot(p.astype(BF16), v) / l


def _mod_row(i, tm, n_ctx_rows, lat_len):
    r = i * tm
    return jnp.where(r < n_ctx_rows, 0, 1 + (r - n_ctx_rows) // lat_len)


def _mod_kernel(c_ref, w_ref, b_ref, o_ref):
    c = c_ref[...]
    o_ref[...] = _dot(_silu(c).astype(BF16), w_ref[...].astype(BF16)) + b_ref[...]


def modulation(cvec, w_mod, b_mod):
    n = w_mod.shape[-1]
    return pl.pallas_call(
        _mod_kernel,
        grid=(DEPTH, n // TN_MOD),
        in_specs=[
            pl.BlockSpec((SUBLANES, D_MODEL), lambda l, j: (0, 0)),
            pl.BlockSpec((None, D_MODEL, TN_MOD), lambda l, j: (l, 0, j)),
            pl.BlockSpec((None, 1, TN_MOD), lambda l, j: (l, 0, j)),
        ],
        out_specs=pl.BlockSpec((None, SUBLANES, TN_MOD), lambda l, j: (l, 0, j)),
        out_shape=jax.ShapeDtypeStruct((DEPTH, SUBLANES, n), F32),
        compiler_params=_cparams(("parallel", "parallel")),
        name="modulation",
    )(cvec, w_mod, b_mod.reshape(DEPTH, 1, n))


def _inproj_kernel(x_ref, sh_ref, sc_ref, w_ref, o_ref, u_sc):
    @pl.when(pl.program_id(1) == 0)
    def _():
        u = _ln(x_ref[...]) * (1.0 + sc_ref[...]) + sh_ref[...]
        u_sc[...] = u.astype(BF16)

    o_ref[...] = _dot(u_sc[...], w_ref[...])


def in_projection(x, mods, w_in_p, layer, n_ctx_rows, lat_len):
    t = x.shape[0]
    tm = TM_IN
    row = functools.partial(_mod_row, tm=tm, n_ctx_rows=n_ctx_rows, lat_len=lat_len)
    return pl.pallas_call(
        _inproj_kernel,
        grid=(t // tm, PROJ_W // TN_IN),
        in_specs=[
            pl.BlockSpec((tm, D_MODEL), lambda i, j: (i, 0)),
            pl.BlockSpec((None, 1, D_MODEL), lambda i, j: (row(i), 0, 0)),
            pl.BlockSpec((None, 1, D_MODEL), lambda i, j: (row(i), 0, 1)),
            pl.BlockSpec((None, D_MODEL, TN_IN), lambda i, j: (layer, 0, j)),
        ],
        out_specs=pl.BlockSpec((tm, TN_IN), lambda i, j: (i, j)),
        out_shape=jax.ShapeDtypeStruct((t, PROJ_W), F32),
        scratch_shapes=[pltpu.VMEM((tm, D_MODEL), BF16)],
        compiler_params=_cparams(("parallel", "arbitrary")),
        name="in_projection",
    )(x, mods, mods, w_in_p)


def _lane_col(a, lane):
    idx = lax.broadcasted_iota(jnp.int32, a.shape, 1)
    return jnp.sum(jnp.where(idx == lane, a, 0.0), axis=1, keepdims=True)


def _ssd_kernel(*refs, seq, latent):
    if latent:
        (z_ref, xbc_ref, small_ref, cw_ref, cb_ref, dtb_ref, arow_ref, drow_ref, nw_ref, h0_ref,
         o_ref, xpad_sc, xs_sc, y_sc, dt_sc, h_sc) = refs
    else:
        (z_ref, xbc_ref, small_ref, cw_ref, cb_ref, dtb_ref, arow_ref, drow_ref, nw_ref,
         o_ref, hout_ref, xpad_sc, xs_sc, y_sc, dt_sc, h_sc) = refs
    q = SSD_CHUNK
    nc = seq // q
    pad = SUBLANES
    ext = q + 2 * pad

    xpad_sc[0:pad, :] = jnp.zeros((pad, SSD_XBC), F32)
    xpad_sc[pad + seq:2 * pad + seq, :] = jnp.zeros((pad, SSD_XBC), F32)

    def copy_in(c, carry):
        r0 = pl.multiple_of(c * q, q)
        xpad_sc[pl.ds(pad + r0, q), :] = xbc_ref[pl.ds(r0, q), :]
        return carry

    lax.fori_loop(0, nc, copy_in, 0)

    def conv(c, carry):
        r0 = pl.multiple_of(c * q, q)
        for ct in range(SSD_XBC // LANES):
            cols = slice(ct * LANES, (ct + 1) * LANES)
            win = xpad_sc[pl.ds(r0, ext), cols]
            acc = cb_ref[:, cols] + win[pad:pad + q] * cw_ref[2:3, cols]
            for k in (0, 1, 3, 4):
                shifted = pltpu.roll(win, (2 - k) % ext, 0)[pad:pad + q]
                acc = acc + shifted * cw_ref[k:k + 1, cols]
            xs_sc[pl.ds(r0, q), cols] = _silu(acc)
        sd = small_ref[pl.ds(r0, q), :] + dtb_ref[...]
        dt_sc[pl.ds(r0, q), :] = jnp.maximum(sd, 0.0) + jnp.log(1.0 + jnp.exp(-jnp.abs(sd)))
        return carry

    lax.fori_loop(0, nc, conv, 0)

    if latent:
        h_sc[...] = h0_ref[...]
    else:
        h_sc[...] = jnp.zeros(h_sc.shape, F32)

    ri = lax.broadcasted_iota(jnp.int32, (q, q), 0)
    ci = lax.broadcasted_iota(jnp.int32, (q, q), 1)
    lane = lax.broadcasted_iota(jnp.int32, (q, LANES), 1)
    lane_lo = lane < SSD_HEAD_DIM
    row_lo = lax.broadcasted_iota(jnp.int32, (2 * SSD_HEAD_DIM, q), 0) < SSD_HEAD_DIM
    row_lo_n = lax.broadcasted_iota(jnp.int32, (2 * SSD_HEAD_DIM, SSD_STATE), 0) < SSD_HEAD_DIM

    def chunk(c, d):
        keep = (ci <= ri) if d == 0 else (ci >= ri)
        tri = jnp.where(keep, 1.0, 0.0)
        r0 = pl.multiple_of(c * q, q)
        xc = xs_sc[pl.ds(r0, q), 0:BRANCH_W]
        dtc = dt_sc[pl.ds(r0, q), :]
        ac = dtc * arow_ref[...]
        acum = jnp.dot(tri, ac, preferred_element_type=F32, precision=lax.Precision.HIGHEST)
        acum_t = acum.T
        dt_t = dtc.T
        tot = acum[q - 1:q, :] if d == 0 else acum[0:1, :]
        eac = jnp.exp(acum)
        w_t = (jnp.exp(tot - acum) * dtc).T
        etot = jnp.exp(tot)
        x_t = xc.T
        for g in range(2):
            bg = xs_sc[pl.ds(r0, q), BRANCH_W + g * SSD_STATE:BRANCH_W + (g + 1) * SSD_STATE].astype(BF16)
            cg = xs_sc[pl.ds(r0, q), BRANCH_W + 2 * SSD_STATE + g * SSD_STATE:
                       BRANCH_W + 2 * SSD_STATE + (g + 1) * SSD_STATE].astype(BF16)
            cb = _dot_nt(cg, bg)
            for pr in range(2):
                pair = g * 2 + pr
                h0 = pair * 2
                l0 = SMALL_DT + d * SSD_HEADS + h0
                atts = []
                for hh in range(2):
                    ln = l0 + hh
                    seg = _lane_col(acum, ln) - acum_t[ln:ln + 1, :]
                    dec = jnp.exp(jnp.where(keep, seg, -1e30))
                    atts.append(cb * dec * dt_t[ln:ln + 1, :])
                att = jnp.concatenate(atts, axis=1).astype(BF16)
                cols = slice(h0 * SSD_HEAD_DIM, (h0 + 2) * SSD_HEAD_DIM)
                xp = xc[:, cols]
                x2 = jnp.concatenate([jnp.where(lane_lo, xp, 0.0), jnp.where(lane_lo, 0.0, xp)],
                                     axis=0).astype(BF16)
                y_diag = _dot(att, x2)
                st_in = h_sc[d, pair]
                y_off = _dot_nt(cg, st_in.astype(BF16))
                esel = jnp.where(lane_lo, _lane_col(eac, l0), _lane_col(eac, l0 + 1))
                y = y_diag + y_off * esel
                if d == 0:
                    y_sc[pl.ds(r0, q), cols] = y
                else:
                    y_sc[pl.ds(r0, q), cols] = y_sc[pl.ds(r0, q), cols] + y
                wsel = jnp.where(row_lo, w_t[l0:l0 + 1, :], w_t[l0 + 1:l0 + 2, :])
                st_new = _dot((x_t[cols, :] * wsel).astype(BF16), bg)
                dsel = jnp.where(row_lo_n, _lane_col(etot, l0), _lane_col(etot, l0 + 1))
                h_sc[d, pair] = dsel * st_in + st_new

    def fwd(c, carry):
        chunk(c, 0)
        return carry

    def bwd(i, carry):
        chunk(nc - 1 - i, 1)
        return carry

    lax.fori_loop(0, nc, fwd, 0)
    lax.fori_loop(0, nc, bwd, 0)

    def finish(c, carry):
        r0 = pl.multiple_of(c * q, q)
        y = y_sc[pl.ds(r0, q), :] + drow_ref[...] * xs_sc[pl.ds(r0, q), 0:BRANCH_W]
        gated = y * _silu(z_ref[pl.ds(r0, q), :])
        o_ref[pl.ds(r0, q), :] = _rms(gated, nw_ref[...]).astype(BF16)
        return carry

    lax.fori_loop(0, nc, finish, 0)
    if not latent:
        hout_ref[...] = h_sc[...]


def ssd_branch(proj, row0, batch, seq, lp, layer, state=None):
    latent = state is not None
    rb = row0 // seq
    n_pair = SSD_HEADS // 2
    st_shape = (2, n_pair, 2 * SSD_HEAD_DIM, SSD_STATE)
    const = lambda b: (0, 0)
    in_specs = [
        pl.BlockSpec((seq, BRANCH_W), lambda b: (rb + b, C_Z // BRANCH_W)),
        pl.BlockSpec((seq, SSD_XBC), lambda b: (rb + b, C_XBC // SSD_XBC)),
        pl.BlockSpec((seq, LANES), lambda b: (rb + b, C_SMALL // LANES)),
        pl.BlockSpec((SUBLANES, SSD_XBC), const),
        pl.BlockSpec((1, SSD_XBC), const),
        pl.BlockSpec((1, LANES), const),
        pl.BlockSpec((1, LANES), const),
        pl.BlockSpec((1, BRANCH_W), const),
        pl.BlockSpec((1, BRANCH_W), const),
    ]
    args = [proj, proj, proj, lp["conv_w"], lp["conv_b"], lp["dt_bias"], lp["a_row"], lp["d_row"],
            lp["ssd_norm"]]
    out_specs = [pl.BlockSpec((seq, BRANCH_W), lambda b: (b, 0))]
    out_shape = [jax.ShapeDtypeStruct((batch * seq, BRANCH_W), BF16)]
    if latent:
        in_specs.append(pl.BlockSpec((None, None) + st_shape, lambda b: (b, layer, 0, 0, 0, 0)))
        args.append(state)
    else:
        out_specs.append(pl.BlockSpec((None,) + st_shape, lambda b: (b, 0, 0, 0, 0)))
        out_shape.append(jax.ShapeDtypeStruct((batch,) + st_shape, F32))
    res = pl.pallas_call(
        functools.partial(_ssd_kernel, seq=seq, latent=latent),
        grid=(batch,),
        in_specs=in_specs,
        out_specs=out_specs,
        out_shape=out_shape,
        scratch_shapes=[
            pltpu.VMEM((seq + 2 * SUBLANES, SSD_XBC), F32),
            pltpu.VMEM((seq, SSD_XBC), F32),
            pltpu.VMEM((seq, BRANCH_W), F32),
            pltpu.VMEM((seq, LANES), F32),
            pltpu.VMEM(st_shape, F32),
        ],
        compiler_params=_cparams(("parallel",)),
        name="ssd_latent" if latent else "ssd_context",
    )(*args)
    return res if not latent else res[0]


def _diff_kernel(*refs, seq, ctx_len, latent, lam_init):
    if latent:
        (q_ref, k_ref, v_ref, kc_ref, vc_ref, c_ref, s_ref, lam_ref, nw_ref, o_ref, k_sc, v_sc) = refs
    else:
        (q_ref, k_ref, v_ref, lam_ref, nw_ref, o_ref, kout_ref, vout_ref, k_sc, v_sc) = refs
    lk = ctx_len + seq
    half = DIFF_D // 2
    wid = 2 * DIFF_D
    lv = lam_ref[...]
    lam = (jnp.exp(jnp.sum(lv[0:1] * lv[1:2], axis=1, keepdims=True))
           - jnp.exp(jnp.sum(lv[2:3] * lv[3:4], axis=1, keepdims=True)) + lam_init)
    scale = DIFF_D ** -0.5
    lane_lo = lax.broadcasted_iota(jnp.int32, (TQ, LANES), 1) < DIFF_D
    if not latent:
        kout_ref[...] = k_ref[...]
        vout_ref[...] = v_ref[...]

    for h in range(DIFF_HEADS):
        hs = slice(h * wid, (h + 1) * wid)
        if latent:
            k_sc[0:ctx_len, hs] = kc_ref[:, hs].astype(BF16)
            v_sc[0:ctx_len, hs] = vc_ref[:, hs].astype(BF16)
            kn = _rope(k_ref[:, hs], c_ref[...], s_ref[...], half)
        else:
            kn = k_ref[:, hs]
        k_sc[ctx_len:lk, hs] = kn.astype(BF16)
        v_sc[ctx_len:lk, hs] = v_ref[:, hs].astype(BF16)

        def qtile(t, carry, hs=hs):
            q0 = pl.multiple_of(t * TQ, TQ)
            qv = q_ref[pl.ds(q0, TQ), hs]
            if latent:
                qv = _rope(qv, c_ref[pl.ds(q0, TQ), :], s_ref[pl.ds(q0, TQ), :], half)
            q1 = jnp.where(lane_lo, qv, 0.0).astype(BF16)
            q2 = jnp.where(lane_lo, 0.0, qv).astype(BF16)
            k = k_sc[:, hs]
            v = v_sc[:, hs]
            o = _softmax_pv(q1, k, v, scale) - lam * _softmax_pv(q2, k, v, scale)
            o_ref[pl.ds(q0, TQ), hs] = (_rms(o, nw_ref[...]) * (1.0 - lam_init)).astype(BF16)
            return carry

        lax.fori_loop(0, seq // TQ, qtile, 0)


def diff_branch(proj, row0, batch, seq, lp, layer, ctx=None):
    latent = ctx is not None
    rb = row0 // seq
    ctx_len = ctx[0].shape[2] if latent else 0
    lam_init = 0.8 - 0.6 * math.exp(-0.3 * layer)
    wid = 2 * DIFF_D
    bw = BRANCH_W
    in_specs = [
        pl.BlockSpec((seq, bw), lambda b: (rb + b, C_DQ // bw)),
        pl.BlockSpec((seq, bw), lambda b: (rb + b, C_DK // bw)),
        pl.BlockSpec((seq, bw), lambda b: (rb + b, C_DV // bw)),
    ]
    args = [proj, proj, proj]
    if latent:
        in_specs += [
            pl.BlockSpec((None, None, ctx_len, bw), lambda b: (b, layer, 0, 0)),
            pl.BlockSpec((None, None, ctx_len, bw), lambda b: (b, layer, 0, 0)),
            pl.BlockSpec((seq, wid), lambda b: (0, 0)),
            pl.BlockSpec((seq, wid), lambda b: (0, 0)),
        ]
        args += [ctx[0], ctx[1], lp["rope_c32"], lp["rope_s32"]]
    in_specs += [
        pl.BlockSpec((4, DIFF_D), lambda b: (0, 0)),
        pl.BlockSpec((1, wid), lambda b: (0, 0)),
    ]
    args += [lp["diff_lambda"], lp["diff_norm"]]
    out_specs = [pl.BlockSpec((seq, bw), lambda b: (b, 0))]
    out_shape = [jax.ShapeDtypeStruct((batch * seq, bw), BF16)]
    if not latent:
        out_specs += [pl.BlockSpec((seq, bw), lambda b: (b, 0))] * 2
        out_shape += [jax.ShapeDtypeStruct((batch * seq, bw), F32)] * 2
    res = pl.pallas_call(
        functools.partial(_diff_kernel, seq=seq, ctx_len=ctx_len, latent=latent, lam_init=lam_init),
        grid=(batch,),
        in_specs=in_specs,
        out_specs=out_specs,
        out_shape=out_shape,
        scratch_shapes=[pltpu.VMEM((ctx_len + seq, bw), BF16), pltpu.VMEM((ctx_len + seq, bw), BF16)],
        compiler_params=_cparams(("parallel",)),
        name="diff_latent" if latent else "diff_context",
    )(*args)
    return res if not latent else res[0]


def _gqa_kernel(*refs, seq, ctx_len, latent):
    if latent:
        (q_ref, k_ref, v_ref, kc_ref, vc_ref, c_ref, s_ref, qw_ref, kw_ref, o_ref, k_sc, v_sc) = refs
    else:
        (q_ref, k_ref, v_ref, qw_ref, kw_ref, o_ref, kout_ref, vout_ref, k_sc, v_sc) = refs
    lk = ctx_len + seq
    half = GQA_HEAD_DIM // 2
    hd = GQA_HEAD_DIM
    grp = GQA_HEADS // GQA_KV_HEADS
    scale = GQA_HEAD_DIM ** -0.5
    if not latent:
        vout_ref[...] = v_ref[...]

    for kvh in range(GQA_KV_HEADS):
        ks = slice(kvh * hd, (kvh + 1) * hd)
        kn = _rms(k_ref[:, ks], kw_ref[...])
        if latent:
            k_sc[0:ctx_len, ks] = kc_ref[:, ks].astype(BF16)
            v_sc[0:ctx_len, ks] = vc_ref[:, ks].astype(BF16)
            kn = _rope(kn, c_ref[...], s_ref[...], half)
        else:
            kout_ref[:, ks] = kn
        k_sc[ctx_len:lk, ks] = kn.astype(BF16)
        v_sc[ctx_len:lk, ks] = v_ref[:, ks].astype(BF16)

        def qtile(t, carry, kvh=kvh, ks=ks):
            q0 = pl.multiple_of(t * TQ, TQ)
            qs = []
            for g in range(grp):
                c0 = (kvh * grp + g) * hd
                qv = _rms(q_ref[pl.ds(q0, TQ), c0:c0 + hd], qw_ref[...])
                if latent:
                    qv = _rope(qv, c_ref[pl.ds(q0, TQ), :], s_ref[pl.ds(q0, TQ), :], half)
                qs.append(qv)
            qq = jnp.concatenate(qs, axis=0).astype(BF16)
            o = _softmax_pv(qq, k_sc[:, ks], v_sc[:, ks], scale)
            for g in range(grp):
                c0 = (kvh * grp + g) * hd
                o_ref[pl.ds(q0, TQ), c0:c0 + hd] = o[g * TQ:(g + 1) * TQ].astype(BF16)
            return carry

        lax.fori_loop(0, seq // TQ, qtile, 0)


def gqa_branch(proj, row0, batch, seq, lp, layer, ctx=None):
    latent = ctx is not None
    rb = row0 // seq
    ctx_len = ctx[0].shape[2] if latent else 0
    hd = GQA_HEAD_DIM
    qw = GQA_HEADS * hd
    kw = GQA_KV_HEADS * hd
    in_specs = [
        pl.BlockSpec((seq, qw), lambda b: (rb + b, C_GQ // qw)),
        pl.BlockSpec((seq, kw), lambda b: (rb + b, C_GK // kw)),
        pl.BlockSpec((seq, kw), lambda b: (rb + b, C_GV // kw)),
    ]
    args = [proj, proj, proj]
    if latent:
        in_specs += [
            pl.BlockSpec((None, None, ctx_len, kw), lambda b: (b, layer, 0, 0)),
            pl.BlockSpec((None, None, ctx_len, kw), lambda b: (b, layer, 0, 0)),
            pl.BlockSpec((seq, hd), lambda b: (0, 0)),
            pl.BlockSpec((seq, hd), lambda b: (0, 0)),
        ]
        args += [ctx[0], ctx[1], lp["rope_c64"], lp["rope_s64"]]
    in_specs += [pl.BlockSpec((1, hd), lambda b: (0, 0)), pl.BlockSpec((1, hd), lambda b: (0, 0))]
    args += [lp["gqa_q_norm"], lp["gqa_k_norm"]]
    out_specs = [pl.BlockSpec((seq, qw), lambda b: (b, 0))]
    out_shape = [jax.ShapeDtypeStruct((batch * seq, qw), BF16)]
    if not latent:
        out_specs += [pl.BlockSpec((seq, kw), lambda b: (b, 0))] * 2
        out_shape += [jax.ShapeDtypeStruct((batch * seq, kw), F32)] * 2
    res = pl.pallas_call(
        functools.partial(_gqa_kernel, seq=seq, ctx_len=ctx_len, latent=latent),
        grid=(batch,),
        in_specs=in_specs,
        out_specs=out_specs,
        out_shape=out_shape,
        scratch_shapes=[pltpu.VMEM((ctx_len + seq, kw), BF16), pltpu.VMEM((ctx_len + seq, kw), BF16)],
        compiler_params=_cparams(("parallel",)),
        name="gqa_latent" if latent else "gqa_context",
    )(*args)
    return res if not latent else res[0]


def _mla_kernel(*refs, seq, ctx_len, latent):
    if latent:
        (cq_ref, ckv_ref, small_ref, ckvc_ref, krc_ref, c_ref, s_ref, qw_ref, kvw_ref, wuq_ref, wuk_ref,
         wuv_ref, o_ref, kn_sc, v_sc, kr_sc) = refs
    else:
        (cq_ref, ckv_ref, small_ref, qw_ref, kvw_ref, wuq_ref, wuk_ref, wuv_ref,
         o_ref, ckv_out_ref, kr_out_ref, kn_sc, v_sc, kr_sc) = refs
    lk = ctx_len + seq
    half = MLA_ROPE // 2
    ckv = _rms(ckv_ref[...], kvw_ref[...])
    if latent:
        cc = ckvc_ref[...].astype(BF16)
        kn_sc[0:ctx_len, :] = _dot(cc, wuk_ref[...]).astype(BF16)
        v_sc[0:ctx_len, :] = _dot(cc, wuv_ref[...]).astype(BF16)
        kr_sc[0:ctx_len, :] = krc_ref[...].astype(BF16)
    else:
        ckv_out_ref[...] = ckv
        kr_out_ref[...] = small_ref[:, 0:MLA_ROPE]
    cb = ckv.astype(BF16)
    kn_sc[ctx_len:lk, :] = _dot(cb, wuk_ref[...]).astype(BF16)
    v_sc[ctx_len:lk, :] = _dot(cb, wuv_ref[...]).astype(BF16)
    kr = small_ref[...]
    if latent:
        kr = _rope(kr, c_ref[...], s_ref[...], half)
    lane_lo = lax.broadcasted_iota(jnp.int32, (seq, LANES), 1) < MLA_ROPE
    kr_sc[ctx_len:lk, :] = jnp.where(lane_lo, kr, 0.0).astype(BF16)
    scale = (MLA_NOPE + MLA_ROPE) ** -0.5
    nope_w = MLA_HEADS * MLA_NOPE

    def qtile(t, carry):
        q0 = pl.multiple_of(t * TQ, TQ)
        cq = _dot(_rms(cq_ref[pl.ds(q0, TQ), :], qw_ref[...]).astype(BF16), wuq_ref[...])
        for h in range(MLA_HEADS):
            hs = slice(h * MLA_NOPE, (h + 1) * MLA_NOPE)
            qn = cq[:, hs].astype(BF16)
            qr = cq[:, nope_w + h * LANES:nope_w + (h + 1) * LANES]
            if latent:
                qr = _rope(qr, c_ref[pl.ds(q0, TQ), :], s_ref[pl.ds(q0, TQ), :], half)
            s = (_dot_nt(qn, kn_sc[:, hs]) + _dot_nt(qr.astype(BF16), kr_sc[...])) * scale
            m = jnp.max(s, axis=-1, keepdims=True)
            p = jnp.exp(s - m)
            l = jnp.sum(p, axis=-1, keepdims=True)
            o = _dot(p.astype(BF16), v_sc[:, hs]) / l
            o_ref[pl.ds(q0, TQ), hs] = o.astype(BF16)
        return carry

    lax.fori_loop(0, seq // TQ, qtile, 0)


def mla_branch(proj, row0, batch, seq, lp, layer, ctx=None):
    latent = ctx is not None
    rb = row0 // seq
    ctx_len = ctx[0].shape[2] if latent else 0
    const = lambda b: (0, 0)
    in_specs = [
        pl.BlockSpec((seq, MLA_Q_RANK), lambda b: (rb + b, C_MCQ // MLA_Q_RANK)),
        pl.BlockSpec((seq, MLA_KV_RANK), lambda b: (rb + b, C_MCKV // MLA_KV_RANK)),
        pl.BlockSpec((seq, LANES), lambda b: (rb + b, C_SMALL // LANES)),
    ]
    args = [proj, proj, proj]
    if latent:
        in_specs += [
            pl.BlockSpec((None, None, ctx_len, MLA_KV_RANK), lambda b: (b, layer, 0, 0)),
            pl.BlockSpec((None, None, ctx_len, LANES), lambda b: (b, layer, 0, 0)),
            pl.BlockSpec((seq, LANES), const),
            pl.BlockSpec((seq, LANES), const),
        ]
        args += [ctx[0], ctx[1], lp["rope_c32"], lp["rope_s32"]]
    in_specs += [
        pl.BlockSpec((1, MLA_Q_RANK), const),
        pl.BlockSpec((1, MLA_KV_RANK), const),
        pl.BlockSpec((MLA_Q_RANK, 2 * MLA_HEADS * LANES), const),
        pl.BlockSpec((MLA_KV_RANK, MLA_HEADS * MLA_NOPE), const),
        pl.BlockSpec((MLA_KV_RANK, BRANCH_W), const),
    ]
    args += [lp["mla_q_norm"], lp["mla_kv_norm"], lp["w_uq"], lp["w_uk"], lp["w_uv"]]
    out_specs = [pl.BlockSpec((seq, BRANCH_W), lambda b: (b, 0))]
    out_shape = [jax.ShapeDtypeStruct((batch * seq, BRANCH_W), BF16)]
    if not latent:
        out_specs += [pl.BlockSpec((seq, MLA_KV_RANK), lambda b: (b, 0)),
                      pl.BlockSpec((seq, MLA_ROPE), lambda b: (b, 0))]
        out_shape += [jax.ShapeDtypeStruct((batch * seq, MLA_KV_RANK), F32),
                      jax.ShapeDtypeStruct((batch * seq, MLA_ROPE), F32)]
    lk = ctx_len + seq
    res = pl.pallas_call(
        functools.partial(_mla_kernel, seq=seq, ctx_len=ctx_len, latent=latent),
        grid=(batch,),
        in_specs=in_specs,
        out_specs=out_specs,
        out_shape=out_shape,
        scratch_shapes=[pltpu.VMEM((lk, MLA_HEADS * MLA_NOPE), BF16), pltpu.VMEM((lk, BRANCH_W), BF16),
                        pltpu.VMEM((lk, LANES), BF16)],
        compiler_params=_cparams(("parallel",)),
        name="mla_latent" if latent else "mla_context",
    )(*args)
    return res if not latent else res[0]


def _pack_pairs(lo, hi):
    lo_bits = lax.bitcast_convert_type(lo.astype(BF16).astype(F32), jnp.uint32)
    hi_bits = lax.bitcast_convert_type(hi.astype(BF16).astype(F32), jnp.uint32)
    return (lo_bits >> jnp.uint32(16)) | (hi_bits & jnp.uint32(0xFFFF0000))


def _unpack_pairs(w):
    lo = lax.bitcast_convert_type(w << jnp.uint32(16), F32)
    hi = lax.bitcast_convert_type(w & jnp.uint32(0xFFFF0000), F32)
    return lo, hi


def _merge_kernel(*refs, n_ctx_tiles):
    oc_refs, ol_refs, g_refs = refs[0:4], refs[4:8], refs[8:12]
    (wb_ref, wo_ref, x_ref, ga_ref, shf_ref, scf_ref, lg_ref, lb_ref, rw_ref, rb_ref,
     x1_ref, u2_ref, lo_ref, acc_sc) = refs[12:]
    i = pl.program_id(0)
    j = pl.program_id(1)

    @pl.when(j == 0)
    def _():
        acc_sc[...] = jnp.zeros(acc_sc.shape, F32)

    is_ctx = i < n_ctx_tiles
    merged = None
    for n in range(N_BRANCH):
        o = jnp.where(is_ctx, oc_refs[n][...], ol_refs[n][...])
        term = _sigmoid(g_refs[n][...]) * _dot(o, wb_ref[n])
        merged = term if merged is None else merged + term
    acc_sc[...] += _dot(merged.astype(BF16), wo_ref[...])

    @pl.when(j == pl.num_programs(1) - 1)
    def _():
        h = DEEPNORM_ALPHA * x_ref[...] + ga_ref[...] * acc_sc[...]
        x1 = _ln(h) * lg_ref[...] + lb_ref[...]
        x1_ref[...] = x1
        u2 = (_ln(x1) * (1.0 + scf_ref[...]) + shf_ref[...]).astype(BF16)
        half = D_MODEL // 2
        u2f = u2.astype(F32)
        u2_ref[...] = _pack_pairs(u2f[:, :half], u2f[:, half:])
        lo_ref[...] = _dot(u2, rw_ref[...]) + rb_ref[...]


def merge_and_route(outs_ctx, outs_lat, proj, x, mods, lp, layer, n_ctx_rows, lat_len):
    t = x.shape[0]
    tm, tn = TM_MERGE, TN_MERGE
    n_ctx_tiles = n_ctx_rows // tm
    row = functools.partial(_mod_row, tm=tm, n_ctx_rows=n_ctx_rows, lat_len=lat_len)
    gate_spec = lambda n: pl.BlockSpec((tm, tn), lambda i, j: (i, (C_GATE + n * D_MODEL) // tn + j))
    vec = lambda k: pl.BlockSpec((None, 1, D_MODEL), lambda i, j: (row(i), 0, k))
    cvec = pl.BlockSpec((None, 1, D_MODEL), lambda i, j: (layer, 0, 0))
    in_specs = (
        [pl.BlockSpec((tm, BRANCH_W), lambda i, j: (jnp.minimum(i, n_ctx_tiles - 1), 0))] * N_BRANCH
        + [pl.BlockSpec((tm, BRANCH_W), lambda i, j: (jnp.maximum(i - n_ctx_tiles, 0), 0))] * N_BRANCH
        + [gate_spec(n) for n in range(N_BRANCH)]
        + [
            pl.BlockSpec((None, N_BRANCH, BRANCH_W, tn), lambda i, j: (layer, 0, 0, j)),
            pl.BlockSpec((None, tn, D_MODEL), lambda i, j: (layer, j, 0)),
            pl.BlockSpec((tm, D_MODEL), lambda i, j: (i, 0)),
            vec(2), vec(3), vec(4), cvec, cvec,
            pl.BlockSpec((None, D_MODEL, LANES), lambda i, j: (layer, 0, 0)),
            pl.BlockSpec((None, 1, LANES), lambda i, j: (layer, 0, 0)),
        ]
    )
    return pl.pallas_call(
        functools.partial(_merge_kernel, n_ctx_tiles=n_ctx_tiles),
        grid=(t // tm, D_MODEL // tn),
        in_specs=in_specs,
        out_specs=[
            pl.BlockSpec((tm, D_MODEL), lambda i, j: (i, 0)),
            pl.BlockSpec((tm, D_MODEL // 2), lambda i, j: (i, 0)),
            pl.BlockSpec((tm, LANES), lambda i, j: (i, 0)),
        ],
        out_shape=[
            jax.ShapeDtypeStruct((t, D_MODEL), F32),
            jax.ShapeDtypeStruct((t, D_MODEL // 2), jnp.uint32),
            jax.ShapeDtypeStruct((t, LANES), F32),
        ],
        scratch_shapes=[pltpu.VMEM((tm, D_MODEL), F32)],
        compiler_params=_cparams(("parallel", "arbitrary")),
        name="merge_route",
    )(*outs_ctx, *outs_lat, proj, proj, proj, proj, lp["w_branch"], lp["w_out"], x, mods, mods, mods,
      lp["ln_mix_g"], lp["ln_mix_b"], lp["router_w"], lp["router_b"])


def _moe_up_kernel(*refs):
    n = N_WPIECE
    valid_ref, first_ref = refs[2], refs[3]
    x_ref = refs[4 + n]
    wg_refs = refs[5 + n:5 + 2 * n]
    wl_refs = refs[5 + 2 * n:5 + 3 * n]
    bg_ref, bl_ref, h_ref, wg_sc, wl_sc = refs[5 + 3 * n:]
    blk = pl.program_id(0)
    pw = EXPERT_FF // n

    @pl.when(first_ref[blk] == 1)
    def _():
        for c in range(n):
            wg_sc[:, c * pw:(c + 1) * pw] = wg_refs[c][...].astype(BF16)
            wl_sc[:, c * pw:(c + 1) * pw] = wl_refs[c][...].astype(BF16)

    @pl.when(valid_ref[blk] == 1)
    def _():
        lo, hi = _unpack_pairs(x_ref[...])
        x = jnp.concatenate([lo, hi], axis=1).astype(BF16)
        g = jnp.minimum(_dot(x, wg_sc[...]) + bg_ref[...], SWIGLU_LIMIT)
        lin = jnp.clip(_dot(x, wl_sc[...]) + bl_ref[...], -SWIGLU_LIMIT, SWIGLU_LIMIT)
        h_ref[...] = ((lin + 1.0) * (g * _sigmoid(SWIGLU_ALPHA * g))).astype(BF16)

    @pl.when(valid_ref[blk] == 0)
    def _():
        h_ref[...] = jnp.zeros(h_ref.shape, BF16)


def _moe_down_kernel(*refs):
    n = N_WPIECE
    valid_ref, first_ref = refs[2], refs[3]
    h_ref = refs[4 + n]
    wd_refs = refs[5 + n:5 + 2 * n]
    bd_ref, y_ref, wd_sc = refs[5 + 2 * n:]
    blk = pl.program_id(0)
    ph = EXPERT_FF // n

    @pl.when(first_ref[blk] == 1)
    def _():
        for c in range(n):
            wd_sc[c * ph:(c + 1) * ph, :] = wd_refs[c][...].astype(BF16)

    @pl.when(valid_ref[blk] == 1)
    def _():
        y = _dot(h_ref[...], wd_sc[...]) + bd_ref[...]
        half = D_MODEL // 2
        y_ref[...] = _pack_pairs(y[:, :half], y[:, half:])

    @pl.when(valid_ref[blk] == 0)
    def _():
        y_ref[...] = jnp.zeros(y_ref.shape, jnp.uint32)


def moe_experts(xs, plan, w_gu, b_gu, w_down, b_down, layer):
    p = xs.shape[0]
    nb = p // TM_MOE
    tm = TM_MOE
    half = D_MODEL // 2
    n = N_WPIECE
    pw = EXPERT_FF // n
    nlin = EXPERT_FF // pw
    pre = 4 + n

    def spec(shape, fn):
        return pl.BlockSpec(shape, lambda b, *pf: fn(b, pf))

    hidden = pl.pallas_call(
        _moe_up_kernel,
        grid_spec=pltpu.PrefetchScalarGridSpec(
            num_scalar_prefetch=pre,
            grid=(nb,),
            in_specs=(
                [spec((tm, half), lambda b, pf: (pf[1][b], 0))]
                + [spec((None, None, D_MODEL, pw), lambda b, pf, c=c: (layer, pf[4 + c][b], 0, c)) for c in range(n)]
                + [spec((None, None, D_MODEL, pw), lambda b, pf, c=c: (layer, pf[4 + c][b], 0, nlin + c))
                   for c in range(n)]
                + [spec((None, None, 1, EXPERT_FF), lambda b, pf: (layer, pf[0][b], 0, 0)),
                   spec((None, None, 1, EXPERT_FF), lambda b, pf: (layer, pf[0][b], 0, 1))]
            ),
            out_specs=spec((tm, EXPERT_FF), lambda b, pf: (b, 0)),
            scratch_shapes=[pltpu.VMEM((D_MODEL, EXPERT_FF), BF16), pltpu.VMEM((D_MODEL, EXPERT_FF), BF16)],
        ),
        out_shape=jax.ShapeDtypeStruct((p, EXPERT_FF), BF16),
        compiler_params=_cparams(("arbitrary",)),
        name="moe_up",
    )(*plan, xs, *([w_gu] * (2 * n)), b_gu, b_gu)
    return pl.pallas_call(
        _moe_down_kernel,
        grid_spec=pltpu.PrefetchScalarGridSpec(
            num_scalar_prefetch=pre,
            grid=(nb,),
            in_specs=(
                [spec((tm, EXPERT_FF), lambda b, pf: (pf[1][b], 0))]
                + [spec((None, None, pw, D_MODEL), lambda b, pf, c=c: (layer, pf[4 + c][b], c, 0)) for c in range(n)]
                + [spec((None, None, 1, D_MODEL), lambda b, pf: (layer, pf[0][b], 0, 0))]
            ),
            out_specs=spec((tm, half), lambda b, pf: (b, 0)),
            scratch_shapes=[pltpu.VMEM((EXPERT_FF, D_MODEL), BF16)],
        ),
        out_shape=jax.ShapeDtypeStruct((p, half), jnp.uint32),
        compiler_params=_cparams(("arbitrary",)),
        name="moe_down",
    )(*plan, hidden, *([w_down] * n), b_down)


def route(logits):
    t = logits.shape[0]
    a = t * TOP_K
    top_v, top_i = lax.top_k(logits, TOP_K)
    gate = jax.nn.softmax(top_v, axis=-1)
    flat_e = top_i.reshape(-1)
    onehot = (flat_e[:, None] == jnp.arange(N_EXPERTS, dtype=jnp.int32)[None, :]).astype(jnp.int32)
    csum = jnp.cumsum(onehot, axis=0)
    counts = csum[-1]
    padded = (counts + TM_MOE - 1) // TM_MOE * TM_MOE
    ends = jnp.cumsum(padded)
    slot = jnp.sum(onehot * (csum - 1 + (ends - padded)[None, :]), axis=1)
    nb = a // TM_MOE + N_EXPERTS
    blk = jnp.arange(nb, dtype=jnp.int32)
    n_used = (ends[-1] // TM_MOE).astype(jnp.int32)
    src = jnp.minimum(blk, n_used - 1)
    blk_e = jnp.minimum(jnp.searchsorted(ends, src * TM_MOE, side="right"), N_EXPERTS - 1).astype(jnp.int32)
    valid = (blk < n_used).astype(jnp.int32)
    first = jnp.where((blk == 0) | (blk_e != jnp.roll(blk_e, 1)), 1, 0).astype(jnp.int32)
    key_bits = 16
    assert a < (1 << key_bits)
    skey = lax.sort(flat_e * (1 << key_bits) + jnp.arange(a, dtype=jnp.int32))
    sorted_t = (skey & ((1 << key_bits) - 1)) // TOP_K
    start = jnp.cumsum(counts) - counts
    blk_off = blk * TM_MOE - (ends - padded)[blk_e]
    base = start[blk_e] + blk_off
    room = jnp.where(valid == 1, counts[blk_e] - blk_off, 0)
    r = jnp.arange(TM_MOE, dtype=jnp.int32)[None, :]
    live = r < room[:, None]
    filler = (blk[:, None] * TM_MOE + r) % t
    tok = jnp.where(live, sorted_t[jnp.clip(base[:, None] + r, 0, a - 1)], filler).reshape(-1)
    tok = lax.optimization_barrier(tok)
    run_start = lax.cummax(jnp.where(first == 1, blk, 0))
    next_start = jnp.concatenate([lax.cummin(jnp.where(first == 1, blk, nb), reverse=True)[1:],
                                  jnp.full((1,), nb, jnp.int32)])
    e_next = jnp.where(next_start < nb, blk_e[jnp.minimum(next_start, nb - 1)], blk_e)
    pieces = tuple(jnp.where(blk - run_start > c, e_next, blk_e).astype(jnp.int32) for c in range(N_WPIECE))
    return gate, slot.reshape(t, TOP_K), tok, (blk_e, src, valid, first) + pieces


def _ffn_norm_kernel(x_ref, y_ref, gate_ref, gf_ref, lg_ref, lb_ref, o_ref):
    gate = gate_ref[...]
    ffn_lo = ffn_hi = None
    for k in range(TOP_K):
        lo, hi = _unpack_pairs(y_ref[k])
        gk = gate[:, k:k + 1]
        ffn_lo = lo * gk if ffn_lo is None else ffn_lo + lo * gk
        ffn_hi = hi * gk if ffn_hi is None else ffn_hi + hi * gk
    ffn = jnp.concatenate([ffn_lo, ffn_hi], axis=1)
    h = DEEPNORM_ALPHA * x_ref[...] + gf_ref[...] * ffn
    o_ref[...] = _ln(h) * lg_ref[...] + lb_ref[...]


def ffn_norm(x1, y_rows, gate, mods, lp, layer, n_ctx_rows, lat_len):
    t = x1.shape[0]
    tm = TM_LN
    row = functools.partial(_mod_row, tm=tm, n_ctx_rows=n_ctx_rows, lat_len=lat_len)
    cvec = pl.BlockSpec((None, 1, D_MODEL), lambda i: (layer, 0, 0))
    return pl.pallas_call(
        _ffn_norm_kernel,
        grid=(t // tm,),
        in_specs=[
            pl.BlockSpec((tm, D_MODEL), lambda i: (i, 0)),
            pl.BlockSpec((TOP_K, tm, D_MODEL // 2), lambda i: (0, i, 0)),
            pl.BlockSpec((tm, TOP_K), lambda i: (i, 0)),
            pl.BlockSpec((None, 1, D_MODEL), lambda i: (row(i), 0, 5)),
            cvec, cvec,
        ],
        out_specs=pl.BlockSpec((tm, D_MODEL), lambda i: (i, 0)),
        out_shape=jax.ShapeDtypeStruct((t, D_MODEL), F32),
        compiler_params=_cparams(("parallel",)),
        name="ffn_norm",
    )(x1, y_rows, gate, mods, lp["ln_ffn_g"], lp["ln_ffn_b"])


def _rope_tables(n_tokens, seg, width):
    rows = n_tokens // GRID_W
    row = jnp.repeat(jnp.arange(rows, dtype=F32), GRID_W)
    col = jnp.tile(jnp.arange(GRID_W, dtype=F32), rows)
    n_freq = seg // 4
    inv = ROPE_THETA ** (-jnp.arange(n_freq, dtype=F32) / n_freq)
    ang = jnp.concatenate([row[:, None] * inv, col[:, None] * inv], axis=-1)
    cos, sin = jnp.cos(ang), jnp.sin(ang)
    reps = width // seg
    return (jnp.tile(jnp.concatenate([cos, cos], axis=-1), (1, reps)),
            jnp.tile(jnp.concatenate([-sin, sin], axis=-1), (1, reps)))


def _pack_w_in_kernel(w_ref, o_ref):
    def put(dst, src, n):
        for t0 in range(0, n, LANES):
            o_ref[:, dst + t0:dst + t0 + LANES] = w_ref[src + t0:src + t0 + LANES, :].T.astype(BF16)

    put(C_XBC, BRANCH_W, SSD_XBC)
    put(C_Z, 0, BRANCH_W)
    put(C_DQ, _O_DQ, _O_MCKV - _O_DQ)
    put(C_GQ, _O_GQ, _O_GK - _O_GQ)
    put(C_MCKV, _O_MCKV, MLA_KV_RANK)
    put(C_GK, _O_GK, _O_GATE - _O_GK)
    n_dt = 2 * SSD_HEADS
    small = jnp.concatenate([w_ref[_O_MKR:_O_MKR + MLA_ROPE, :], w_ref[_O_DT:_O_DT + n_dt, :],
                             jnp.zeros((LANES - MLA_ROPE - n_dt, TR_PACK), F32)], axis=0)
    o_ref[:, C_SMALL:C_SMALL + LANES] = small.T.astype(BF16)
    o_ref[:, C_SMALL + LANES:C_GATE] = jnp.zeros((TR_PACK, C_GATE - C_SMALL - LANES), BF16)
    put(C_GATE, _O_GATE, N_BRANCH * D_MODEL)


def _pack_w_in(w_in):
    tr = TR_PACK
    return pl.pallas_call(
        _pack_w_in_kernel,
        grid=(DEPTH, D_MODEL // tr),
        in_specs=[pl.BlockSpec((None, IN_WIDTH, tr), lambda l, r: (l, 0, r))],
        out_specs=pl.BlockSpec((None, tr, PROJ_W), lambda l, r: (l, r, 0)),
        out_shape=jax.ShapeDtypeStruct((DEPTH, D_MODEL, PROJ_W), BF16),
        compiler_params=_cparams(("parallel", "parallel")),
        name="pack_w_in",
    )(jnp.transpose(w_in, (0, 2, 1)))


def _pack_w_uq(w_uq):
    w = w_uq.reshape(MLA_Q_RANK, MLA_HEADS, MLA_NOPE + MLA_ROPE)
    nope = w[:, :, :MLA_NOPE].reshape(MLA_Q_RANK, MLA_HEADS * MLA_NOPE)
    rope = jnp.pad(w[:, :, MLA_NOPE:], ((0, 0), (0, 0), (0, LANES - MLA_ROPE)))
    return jnp.concatenate([nope, rope.reshape(MLA_Q_RANK, MLA_HEADS * LANES)], axis=-1).astype(BF16)


def _small_row(v16):
    return jnp.zeros((1, LANES), F32).at[0, SMALL_DT:SMALL_DT + 2 * SSD_HEADS].set(v16.reshape(-1))


def kernel(x_prompt, x_sample, state_ssd, cache_diff_k, cache_diff_v, cache_mla_ckv, cache_mla_kr, cache_gqa_k, cache_gqa_v, c, c_ctx, w_mod, b_mod, w_in, ssd_conv_w, ssd_conv_b, ssd_dt_bias, ssd_a_log, ssd_d, ssd_norm, diff_lambda, diff_norm, mla_q_norm, mla_w_uq, mla_kv_norm, mla_w_uk, mla_w_uv, gqa_q_norm, gqa_k_norm, w_branch, w_out, ln_mix_g, ln_mix_b, router_w, router_b, exp_w_gu, exp_b_gu, exp_w_down, exp_b_down, ln_ffn_g, ln_ffn_b):
    nb_ctx, len_ctx, _ = x_prompt.shape
    nb_lat, len_lat, _ = x_sample.shape
    past = cache_diff_k.shape[2]
    n_ctx_rows = nb_ctx * len_ctx
    t = n_ctx_rows + nb_lat * len_lat

    cvec = jnp.zeros((SUBLANES, D_MODEL), F32).at[0].set(c_ctx).at[1:1 + nb_lat].set(c)
    mods_all = modulation(cvec, w_mod, b_mod)

    w_in_p = _pack_w_in(w_in)
    w_branch_b = w_branch.astype(BF16)
    w_out_b = w_out.astype(BF16)
    router_w_p = jnp.pad(router_w, ((0, 0), (0, 0), (0, LANES - N_EXPERTS))).astype(BF16)
    router_b_p = jnp.pad(router_b, ((0, 0), (0, LANES - N_EXPERTS))).reshape(DEPTH, 1, LANES)
    c32, s32 = _rope_tables(len_lat, 2 * (DIFF_D // 2), LANES)
    c64, s64 = _rope_tables(len_lat, GQA_HEAD_DIM, LANES)
    n_pair = SSD_HEADS // 2
    state_in = state_ssd.reshape(nb_lat, DEPTH, 2, n_pair, 2 * SSD_HEAD_DIM, SSD_STATE)
    diff_kc = cache_diff_k.reshape(nb_lat, DEPTH, past, BRANCH_W)
    diff_vc = cache_diff_v.reshape(nb_lat, DEPTH, past, BRANCH_W)
    mla_krc = jnp.pad(cache_mla_kr, ((0, 0), (0, 0), (0, 0), (0, LANES - MLA_ROPE)))
    gqa_kc = cache_gqa_k.reshape(nb_lat, DEPTH, past, GQA_KV_HEADS * GQA_HEAD_DIM)
    gqa_vc = cache_gqa_v.reshape(nb_lat, DEPTH, past, GQA_KV_HEADS * GQA_HEAD_DIM)
    vecs = dict(ln_mix_g=ln_mix_g.reshape(DEPTH, 1, D_MODEL), ln_mix_b=ln_mix_b.reshape(DEPTH, 1, D_MODEL),
                ln_ffn_g=ln_ffn_g.reshape(DEPTH, 1, D_MODEL), ln_ffn_b=ln_ffn_b.reshape(DEPTH, 1, D_MODEL),
                w_branch=w_branch_b, w_out=w_out_b, router_w=router_w_p, router_b=router_b_p)
    b_gu = exp_b_gu.reshape(DEPTH, N_EXPERTS, 1, 2 * EXPERT_FF)
    b_down = exp_b_down.reshape(DEPTH, N_EXPERTS, 1, D_MODEL)

    x = jnp.concatenate([x_prompt.reshape(n_ctx_rows, D_MODEL), x_sample.reshape(-1, D_MODEL)], axis=0)
    new_state, new_dk, new_dv, new_ckv, new_kr, new_gk, new_gv = [], [], [], [], [], [], []
    for l in range(DEPTH):
        lp = dict(vecs)
        lp.update(
            conv_w=jnp.pad(ssd_conv_w[l], ((0, SUBLANES - SSD_CONV), (0, 0))),
            conv_b=ssd_conv_b[l].reshape(1, SSD_XBC),
            dt_bias=_small_row(ssd_dt_bias[l]),
            a_row=_small_row(-jnp.exp(ssd_a_log[l])),
            d_row=jnp.repeat(ssd_d[l], SSD_HEAD_DIM).reshape(1, BRANCH_W),
            ssd_norm=ssd_norm[l].reshape(1, BRANCH_W),
            diff_lambda=diff_lambda[l],
            diff_norm=diff_norm[l].reshape(1, 2 * DIFF_D),
            mla_q_norm=mla_q_norm[l].reshape(1, MLA_Q_RANK),
            mla_kv_norm=mla_kv_norm[l].reshape(1, MLA_KV_RANK),
            w_uq=_pack_w_uq(mla_w_uq[l]),
            w_uk=mla_w_uk[l].astype(BF16),
            w_uv=mla_w_uv[l].astype(BF16),
            gqa_q_norm=gqa_q_norm[l].reshape(1, GQA_HEAD_DIM),
            gqa_k_norm=gqa_k_norm[l].reshape(1, GQA_HEAD_DIM),
            rope_c32=c32, rope_s32=s32, rope_c64=c64, rope_s64=s64,
        )
        mods = mods_all[l].reshape(SUBLANES, 1, 6 * D_MODEL)
        proj = in_projection(x, mods, w_in_p, l, n_ctx_rows, len_lat)

        o_ssd_c, st = ssd_branch(proj, 0, nb_ctx, len_ctx, lp, l)
        o_ssd_l = ssd_branch(proj, n_ctx_rows, nb_lat, len_lat, lp, l, state_in)
        o_diff_c, dk_c, dv_c = diff_branch(proj, 0, nb_ctx, len_ctx, lp, l)
        o_diff_l = diff_branch(proj, n_ctx_rows, nb_lat, len_lat, lp, l, (diff_kc, diff_vc))
        o_mla_c, ckv_c, kr_c = mla_branch(proj, 0, nb_ctx, len_ctx, lp, l)
        o_mla_l = mla_branch(proj, n_ctx_rows, nb_lat, len_lat, lp, l, (cache_mla_ckv, mla_krc))
        o_gqa_c, gk_c, gv_c = gqa_branch(proj, 0, nb_ctx, len_ctx, lp, l)
        o_gqa_l = gqa_branch(proj, n_ctx_rows, nb_lat, len_lat, lp, l, (gqa_kc, gqa_vc))
        outs_ctx = (o_ssd_c, o_diff_c, o_mla_c, o_gqa_c)
        outs_lat = (o_ssd_l, o_diff_l, o_mla_l, o_gqa_l)

        new_state.append(st.reshape(nb_ctx, 2, SSD_HEADS, SSD_HEAD_DIM, SSD_STATE))
        new_dk.append(dk_c.reshape(nb_ctx, len_ctx, DIFF_HEADS, 2 * DIFF_D))
        new_dv.append(dv_c.reshape(nb_ctx, len_ctx, DIFF_HEADS, 2 * DIFF_D))
        new_ckv.append(ckv_c.reshape(nb_ctx, len_ctx, MLA_KV_RANK))
        new_kr.append(kr_c.reshape(nb_ctx, len_ctx, MLA_ROPE))
        new_gk.append(gk_c.reshape(nb_ctx, len_ctx, GQA_KV_HEADS, GQA_HEAD_DIM))
        new_gv.append(gv_c.reshape(nb_ctx, len_ctx, GQA_KV_HEADS, GQA_HEAD_DIM))

        x1, u2, logits = merge_and_route(outs_ctx, outs_lat, proj, x, mods, lp, l, n_ctx_rows, len_lat)
        gate, slot, tok, plan = route(logits[:, :N_EXPERTS])
        u2 = jnp.pad(u2, ((0, max(0, SC_GATHER_MIN_ROWS - t)), (0, 0)))
        yb = moe_experts(u2[tok], plan, exp_w_gu, b_gu, exp_w_down, b_down, l)
        y_rows = yb[slot.T.reshape(-1)].reshape(TOP_K, t, D_MODEL // 2)
        x = ffn_norm(x1, y_rows, gate, mods, lp, l, n_ctx_rows, len_lat)

    y_prompt = x[:n_ctx_rows].reshape(nb_ctx, len_ctx, D_MODEL)
    y_sample = x[n_ctx_rows:].reshape(nb_lat, len_lat, D_MODEL)
    stack = lambda parts: jnp.stack(parts, axis=1)
    return (y_prompt, y_sample, stack(new_state), stack(new_dk), stack(new_dv), stack(new_ckv),
            stack(new_kr), stack(new_gk), stack(new_gv))
```

```python
import functools
import math

import jax
import jax.numpy as jnp
from jax import lax
from jax.experimental import pallas as pl
from jax.experimental.pallas import tpu as pltpu

F32 = jnp.float32
BF16 = jnp.bfloat16

D_MODEL = 2048
DEPTH = 4
GRID_W = 64
ROPE_THETA = 10000.0
BRANCH_W = 512
N_BRANCH = 4
SSD_HEADS = 8
SSD_HEAD_DIM = 64
SSD_STATE = 128
SSD_CONV = 5
SSD_CHUNK = 128
SSD_XBC = 1024
DIFF_HEADS = 4
DIFF_D = 64
MLA_HEADS = 4
MLA_NOPE = 128
MLA_ROPE = 64
MLA_Q_RANK = 512
MLA_KV_RANK = 256
GQA_HEAD_DIM = 128
GQA_HEADS = 4
GQA_KV_HEADS = 2
N_EXPERTS = 32
TOP_K = 4
EXPERT_FF = 1024
SWIGLU_LIMIT = 7.0
SWIGLU_ALPHA = 1.702
DEEPNORM_ALPHA = (2.0 * DEPTH) ** 0.25
EPS = 1e-6

LANES = 128
SUBLANES = 8

_O_DT = 1536
_O_DQ = 1552
_O_MCKV = 3600
_O_MKR = 3856
_O_GQ = 3920
_O_GK = 4432
_O_GATE = 4944
IN_WIDTH = 13136
C_XBC = 0
C_Z = 1024
C_DQ = 1536
C_DK = 2048
C_DV = 2560
C_MCQ = 3072
C_GQ = 3584
C_MCKV = 4096
C_GK = 4352
C_GV = 4608
C_SMALL = 4864
C_GATE = 5120
PROJ_W = C_GATE + N_BRANCH * D_MODEL
SMALL_DT = MLA_ROPE

VMEM_LIMIT = 60000 * 1024

TM_IN = 1024
TN_IN = 1024
TM_MERGE = 512
TN_MERGE = 512
TM_LN = 512
TM_MOE = 512
N_WPIECE = 4
TN_MOD = 1024
TR_PACK = 128
TQ = 256
SC_GATHER_MIN_ROWS = 16384


def _cparams(sem):
    return pltpu.CompilerParams(dimension_semantics=sem, vmem_limit_bytes=VMEM_LIMIT)


def _dot(a, b):
    return jnp.dot(a, b, preferred_element_type=F32)


def _dot_nt(a, b):
    return lax.dot_general(a, b, (((1,), (1,)), ((), ())), preferred_element_type=F32)


def _ln(x):
    mu = jnp.mean(x, axis=-1, keepdims=True)
    xc = x - mu
    var = jnp.mean(xc * xc, axis=-1, keepdims=True)
    return xc * lax.rsqrt(var + EPS)


def _rms(x, g):
    return x * lax.rsqrt(jnp.mean(x * x, axis=-1, keepdims=True) + EPS) * g


def _sigmoid(x):
    return 0.5 * jnp.tanh(0.5 * x) + 0.5


def _silu(x):
    return x * _sigmoid(x)


def _rope(x, c, s, half):
    w = x.shape[-1]
    lane = lax.broadcasted_iota(jnp.int32, x.shape, 1)
    lo = (lane & (2 * half - 1)) < half
    swapped = jnp.where(lo, pltpu.roll(x, w - half, 1), pltpu.roll(x, half, 1))
    return x * c + swapped * s


def _softmax_pv(q, k, v, scale):
    s = _dot_nt(q, k) * scale
    m = jnp.max(s, axis=-1, keepdims=True)
    p = jnp.exp(s - m)
    l = jnp.sum(p, axis=-1, keepdims=True)
    return _dot(p.astype(BF16), v) / l


def _mod_row(i, tm, n_ctx_rows, lat_len):
    r = i * tm
    return jnp.where(r < n_ctx_rows, 0, 1 + (r - n_ctx_rows) // lat_len)


def _mod_kernel(c_ref, w_ref, b_ref, o_ref):
    c = c_ref[...]
    o_ref[...] = _dot(_silu(c).astype(BF16), w_ref[...].astype(BF16)) + b_ref[...]


def modulation(cvec, w_mod, b_mod):
    n = w_mod.shape[-1]
    return pl.pallas_call(
        _mod_kernel,
        grid=(DEPTH, n // TN_MOD),
        in_specs=[
            pl.BlockSpec((SUBLANES, D_MODEL), lambda l, j: (0, 0)),
            pl.BlockSpec((None, D_MODEL, TN_MOD), lambda l, j: (l, 0, j)),
            pl.BlockSpec((None, 1, TN_MOD), lambda l, j: (l, 0, j)),
        ],
        out_specs=pl.BlockSpec((None, SUBLANES, TN_MOD), lambda l, j: (l, 0, j)),
        out_shape=jax.ShapeDtypeStruct((DEPTH, SUBLANES, n), F32),
        compiler_params=_cparams(("parallel", "parallel")),
        name="modulation",
    )(cvec, w_mod, b_mod.reshape(DEPTH, 1, n))


def _inproj_kernel(x_ref, sh_ref, sc_ref, w_ref, o_ref, g_ref, u_sc):
    j = pl.program_id(1)

    @pl.when(j == 0)
    def _():
        u = _ln(x_ref[...]) * (1.0 + sc_ref[...]) + sh_ref[...]
        u_sc[...] = u.astype(BF16)

    acc = _dot(u_sc[...], w_ref[...])

    @pl.when(j < C_GATE // TN_IN)
    def _():
        o_ref[...] = acc

    @pl.when(j >= C_GATE // TN_IN)
    def _():
        g_ref[...] = acc.astype(BF16)


def in_projection(x, mods, w_in_p, layer, n_ctx_rows, lat_len):
    t = x.shape[0]
    tm = TM_IN
    n_main = C_GATE // TN_IN
    row = functools.partial(_mod_row, tm=tm, n_ctx_rows=n_ctx_rows, lat_len=lat_len)
    return pl.pallas_call(
        _inproj_kernel,
        grid=(t // tm, PROJ_W // TN_IN),
        in_specs=[
            pl.BlockSpec((tm, D_MODEL), lambda i, j: (i, 0)),
            pl.BlockSpec((None, 1, D_MODEL), lambda i, j: (row(i), 0, 0)),
            pl.BlockSpec((None, 1, D_MODEL), lambda i, j: (row(i), 0, 1)),
            pl.BlockSpec((None, D_MODEL, TN_IN), lambda i, j: (layer, 0, j)),
        ],
        out_specs=[
            pl.BlockSpec((tm, TN_IN), lambda i, j: (i, jnp.minimum(j, n_main - 1))),
            pl.BlockSpec((tm, TN_IN), lambda i, j: (i, jnp.maximum(j - n_main, 0))),
        ],
        out_shape=[
            jax.ShapeDtypeStruct((t, C_GATE), F32),
            jax.ShapeDtypeStruct((t, N_BRANCH * D_MODEL), BF16),
        ],
        scratch_shapes=[pltpu.VMEM((tm, D_MODEL), BF16)],
        compiler_params=_cparams(("parallel", "arbitrary")),
        name="in_projection",
    )(x, mods, mods, w_in_p)


def _lane_col(a, lane):
    idx = lax.broadcasted_iota(jnp.int32, a.shape, 1)
    return jnp.sum(jnp.where(idx == lane, a, 0.0), axis=1, keepdims=True)


def _ssd_kernel(*refs, seq, latent):
    if latent:
        (z_ref, xbc_ref, small_ref, cw_ref, cb_ref, dtb_ref, arow_ref, drow_ref, nw_ref, h0_ref,
         o_ref, xpad_sc, xs_sc, y_sc, dt_sc, h_sc) = refs
    else:
        (z_ref, xbc_ref, small_ref, cw_ref, cb_ref, dtb_ref, arow_ref, drow_ref, nw_ref,
         o_ref, hout_ref, xpad_sc, xs_sc, y_sc, dt_sc, h_sc) = refs
    q = SSD_CHUNK
    nc = seq // q
    pad = SUBLANES
    ext = q + 2 * pad

    xpad_sc[0:pad, :] = jnp.zeros((pad, SSD_XBC), F32)
    xpad_sc[pad + seq:2 * pad + seq, :] = jnp.zeros((pad, SSD_XBC), F32)

    def copy_in(c, carry):
        r0 = pl.multiple_of(c * q, q)
        xpad_sc[pl.ds(pad + r0, q), :] = xbc_ref[pl.ds(r0, q), :]
        return carry

    lax.fori_loop(0, nc, copy_in, 0)

    def conv(c, carry):
        r0 = pl.multiple_of(c * q, q)
        for ct in range(SSD_XBC // LANES):
            cols = slice(ct * LANES, (ct + 1) * LANES)
            win = xpad_sc[pl.ds(r0, ext), cols]
            acc = cb_ref[:, cols] + win[pad:pad + q] * cw_ref[2:3, cols]
            for k in (0, 1, 3, 4):
                shifted = pltpu.roll(win, (2 - k) % ext, 0)[pad:pad + q]
                acc = acc + shifted * cw_ref[k:k + 1, cols]
            xs_sc[pl.ds(r0, q), cols] = _silu(acc)
        sd = small_ref[pl.ds(r0, q), :] + dtb_ref[...]
        dt_sc[pl.ds(r0, q), :] = jnp.maximum(sd, 0.0) + jnp.log(1.0 + jnp.exp(-jnp.abs(sd)))
        return carry

    lax.fori_loop(0, nc, conv, 0)

    if latent:
        h_sc[...] = h0_ref[...]
    else:
        h_sc[...] = jnp.zeros(h_sc.shape, F32)

    ri = lax.broadcasted_iota(jnp.int32, (q, q), 0)
    ci = lax.broadcasted_iota(jnp.int32, (q, q), 1)
    lane = lax.broadcasted_iota(jnp.int32, (q, LANES), 1)
    lane_lo = lane < SSD_HEAD_DIM
    row_lo = lax.broadcasted_iota(jnp.int32, (2 * SSD_HEAD_DIM, q), 0) < SSD_HEAD_DIM
    row_lo_n = lax.broadcasted_iota(jnp.int32, (2 * SSD_HEAD_DIM, SSD_STATE), 0) < SSD_HEAD_DIM

    def chunk(c, d):
        keep = (ci <= ri) if d == 0 else (ci >= ri)
        tri = jnp.where(keep, 1.0, 0.0)
        r0 = pl.multiple_of(c * q, q)
        xc = xs_sc[pl.ds(r0, q), 0:BRANCH_W]
        dtc = dt_sc[pl.ds(r0, q), :]
        ac = dtc * arow_ref[...]
        acum = jnp.dot(tri, ac, preferred_element_type=F32, precision=lax.Precision.HIGHEST)
        acum_t = acum.T
        dt_t = dtc.T
        tot = acum[q - 1:q, :] if d == 0 else acum[0:1, :]
        eac = jnp.exp(acum)
        w_t = (jnp.exp(tot - acum) * dtc).T
        etot = jnp.exp(tot)
        x_t = xc.T
        for g in range(2):
            bg = xs_sc[pl.ds(r0, q), BRANCH_W + g * SSD_STATE:BRANCH_W + (g + 1) * SSD_STATE].astype(BF16)
            cg = xs_sc[pl.ds(r0, q), BRANCH_W + 2 * SSD_STATE + g * SSD_STATE:
                       BRANCH_W + 2 * SSD_STATE + (g + 1) * SSD_STATE].astype(BF16)
            cb = _dot_nt(cg, bg)
            for pr in range(2):
                pair = g * 2 + pr
                h0 = pair * 2
                l0 = SMALL_DT + d * SSD_HEADS + h0
                atts = []
                for hh in range(2):
                    ln = l0 + hh
                    seg = _lane_col(acum, ln) - acum_t[ln:ln + 1, :]
                    dec = jnp.exp(jnp.where(keep, seg, -1e30))
                    atts.append(cb * dec * dt_t[ln:ln + 1, :])
                att = jnp.concatenate(atts, axis=1).astype(BF16)
                cols = slice(h0 * SSD_HEAD_DIM, (h0 + 2) * SSD_HEAD_DIM)
                xp = xc[:, cols]
                x2 = jnp.concatenate([jnp.where(lane_lo, xp, 0.0), jnp.where(lane_lo, 0.0, xp)],
                                     axis=0).astype(BF16)
                y_diag = _dot(att, x2)
                st_in = h_sc[d, pair]
                y_off = _dot_nt(cg, st_in.astype(BF16))
                esel = jnp.where(lane_lo, _lane_col(eac, l0), _lane_col(eac, l0 + 1))
                y = y_diag + y_off * esel
                if d == 0:
                    y_sc[pl.ds(r0, q), cols] = y
                else:
                    y_sc[pl.ds(r0, q), cols] = y_sc[pl.ds(r0, q), cols] + y
                wsel = jnp.where(row_lo, w_t[l0:l0 + 1, :], w_t[l0 + 1:l0 + 2, :])
                st_new = _dot((x_t[cols, :] * wsel).astype(BF16), bg)
                dsel = jnp.where(row_lo_n, _lane_col(etot, l0), _lane_col(etot, l0 + 1))
                h_sc[d, pair] = dsel * st_in + st_new

    def fwd(c, carry):
        chunk(c, 0)
        return carry

    def bwd(i, carry):
        chunk(nc - 1 - i, 1)
        return carry

    lax.fori_loop(0, nc, fwd, 0)
    lax.fori_loop(0, nc, bwd, 0)

    def finish(c, carry):
        r0 = pl.multiple_of(c * q, q)
        y = y_sc[pl.ds(r0, q), :] + drow_ref[...] * xs_sc[pl.ds(r0, q), 0:BRANCH_W]
        gated = y * _silu(z_ref[pl.ds(r0, q), :])
        o_ref[pl.ds(r0, q), :] = _rms(gated, nw_ref[...]).astype(BF16)
        return carry

    lax.fori_loop(0, nc, finish, 0)
    if not latent:
        hout_ref[...] = h_sc[...]


def ssd_branch(proj, row0, batch, seq, lp, layer, state=None):
    latent = state is not None
    rb = row0 // seq
    n_pair = SSD_HEADS // 2
    st_shape = (2, n_pair, 2 * SSD_HEAD_DIM, SSD_STATE)
    const = lambda b: (0, 0)
    in_specs = [
        pl.BlockSpec((seq, BRANCH_W), lambda b: (rb + b, C_Z // BRANCH_W)),
        pl.BlockSpec((seq, SSD_XBC), lambda b: (rb + b, C_XBC // SSD_XBC)),
        pl.BlockSpec((seq, LANES), lambda b: (rb + b, C_SMALL // LANES)),
        pl.BlockSpec((SUBLANES, SSD_XBC), const),
        pl.BlockSpec((1, SSD_XBC), const),
        pl.BlockSpec((1, LANES), const),
        pl.BlockSpec((1, LANES), const),
        pl.BlockSpec((1, BRANCH_W), const),
        pl.BlockSpec((1, BRANCH_W), const),
    ]
    args = [proj, proj, proj, lp["conv_w"], lp["conv_b"], lp["dt_bias"], lp["a_row"], lp["d_row"],
            lp["ssd_norm"]]
    out_specs = [pl.BlockSpec((seq, BRANCH_W), lambda b: (b, 0))]
    out_shape = [jax.ShapeDtypeStruct((batch * seq, BRANCH_W), BF16)]
    if latent:
        in_specs.append(pl.BlockSpec((None, None) + st_shape, lambda b: (b, layer, 0, 0, 0, 0)))
        args.append(state)
    else:
        out_specs.append(pl.BlockSpec((None,) + st_shape, lambda b: (b, 0, 0, 0, 0)))
        out_shape.append(jax.ShapeDtypeStruct((batch,) + st_shape, F32))
    res = pl.pallas_call(
        functools.partial(_ssd_kernel, seq=seq, latent=latent),
        grid=(batch,),
        in_specs=in_specs,
        out_specs=out_specs,
        out_shape=out_shape,
        scratch_shapes=[
            pltpu.VMEM((seq + 2 * SUBLANES, SSD_XBC), F32),
            pltpu.VMEM((seq, SSD_XBC), F32),
            pltpu.VMEM((seq, BRANCH_W), F32),
            pltpu.VMEM((seq, LANES), F32),
            pltpu.VMEM(st_shape, F32),
        ],
        compiler_params=_cparams(("parallel",)),
        name="ssd_latent" if latent else "ssd_context",
    )(*args)
    return res if not latent else res[0]


def _diff_kernel(*refs, seq, ctx_len, latent, lam_init):
    if latent:
        (q_ref, k_ref, v_ref, kc_ref, vc_ref, c_ref, s_ref, lam_ref, nw_ref, o_ref, k_sc, v_sc) = refs
    else:
        (q_ref, k_ref, v_ref, lam_ref, nw_ref, o_ref, kout_ref, vout_ref, k_sc, v_sc) = refs
    lk = ctx_len + seq
    half = DIFF_D // 2
    wid = 2 * DIFF_D
    lv = lam_ref[...]
    lam = (jnp.exp(jnp.sum(lv[0:1] * lv[1:2], axis=1, keepdims=True))
           - jnp.exp(jnp.sum(lv[2:3] * lv[3:4], axis=1, keepdims=True)) + lam_init)
    scale = DIFF_D ** -0.5
    lane_lo = lax.broadcasted_iota(jnp.int32, (TQ, LANES), 1) < DIFF_D
    if not latent:
        kout_ref[...] = k_ref[...]
        vout_ref[...] = v_ref[...]

    for h in range(DIFF_HEADS):
        hs = slice(h * wid, (h + 1) * wid)
        if latent:
            k_sc[0:ctx_len, hs] = kc_ref[:, hs].astype(BF16)
            v_sc[0:ctx_len, hs] = vc_ref[:, hs].astype(BF16)
            kn = _rope(k_ref[:, hs], c_ref[...], s_ref[...], half)
        else:
            kn = k_ref[:, hs]
        k_sc[ctx_len:lk, hs] = kn.astype(BF16)
        v_sc[ctx_len:lk, hs] = v_ref[:, hs].astype(BF16)

        def qtile(t, carry, hs=hs):
            q0 = pl.multiple_of(t * TQ, TQ)
            qv = q_ref[pl.ds(q0, TQ), hs]
            if latent:
                qv = _rope(qv, c_ref[pl.ds(q0, TQ), :], s_ref[pl.ds(q0, TQ), :], half)
            q1 = jnp.where(lane_lo, qv, 0.0).astype(BF16)
            q2 = jnp.where(lane_lo, 0.0, qv).astype(BF16)
            k = k_sc[:, hs]
            v = v_sc[:, hs]
            o = _softmax_pv(q1, k, v, scale) - lam * _softmax_pv(q2, k, v, scale)
            o_ref[pl.ds(q0, TQ), hs] = (_rms(o, nw_ref[...]) * (1.0 - lam_init)).astype(BF16)
            return carry

        lax.fori_loop(0, seq // TQ, qtile, 0)


def diff_branch(proj, row0, batch, seq, lp, layer, ctx=None):
    latent = ctx is not None
    rb = row0 // seq
    ctx_len = ctx[0].shape[2] if latent else 0
    lam_init = 0.8 - 0.6 * math.exp(-0.3 * layer)
    wid = 2 * DIFF_D
    bw = BRANCH_W
    in_specs = [
        pl.BlockSpec((seq, bw), lambda b: (rb + b, C_DQ // bw)),
        pl.BlockSpec((seq, bw), lambda b: (rb + b, C_DK // bw)),
        pl.BlockSpec((seq, bw), lambda b: (rb + b, C_DV // bw)),
    ]
    args = [proj, proj, proj]
    if latent:
        in_specs += [
            pl.BlockSpec((None, None, ctx_len, bw), lambda b: (b, layer, 0, 0)),
            pl.BlockSpec((None, None, ctx_len, bw), lambda b: (b, layer, 0, 0)),
            pl.BlockSpec((seq, wid), lambda b: (0, 0)),
            pl.BlockSpec((seq, wid), lambda b: (0, 0)),
        ]
        args += [ctx[0], ctx[1], lp["rope_c32"], lp["rope_s32"]]
    in_specs += [
        pl.BlockSpec((4, DIFF_D), lambda b: (0, 0)),
        pl.BlockSpec((1, wid), lambda b: (0, 0)),
    ]
    args += [lp["diff_lambda"], lp["diff_norm"]]
    out_specs = [pl.BlockSpec((seq, bw), lambda b: (b, 0))]
    out_shape = [jax.ShapeDtypeStruct((batch * seq, bw), BF16)]
    if not latent:
        out_specs += [pl.BlockSpec((seq, bw), lambda b: (b, 0))] * 2
        out_shape += [jax.ShapeDtypeStruct((batch * seq, bw), F32)] * 2
    res = pl.pallas_call(
        functools.partial(_diff_kernel, seq=seq, ctx_len=ctx_len, latent=latent, lam_init=lam_init),
        grid=(batch,),
        in_specs=in_specs,
        out_specs=out_specs,
        out_shape=out_shape,
        scratch_shapes=[pltpu.VMEM((ctx_len + seq, bw), BF16), pltpu.VMEM((ctx_len + seq, bw), BF16)],
        compiler_params=_cparams(("parallel",)),
        name="diff_latent" if latent else "diff_context",
    )(*args)
    return res if not latent else res[0]


def _gqa_kernel(*refs, seq, ctx_len, latent):
    if latent:
        (q_ref, k_ref, v_ref, kc_ref, vc_ref, c_ref, s_ref, qw_ref, kw_ref, o_ref, k_sc, v_sc) = refs
    else:
        (q_ref, k_ref, v_ref, qw_ref, kw_ref, o_ref, kout_ref, vout_ref, k_sc, v_sc) = refs
    lk = ctx_len + seq
    half = GQA_HEAD_DIM // 2
    hd = GQA_HEAD_DIM
    grp = GQA_HEADS // GQA_KV_HEADS
    scale = GQA_HEAD_DIM ** -0.5
    if not latent:
        vout_ref[...] = v_ref[...]

    for kvh in range(GQA_KV_HEADS):
        ks = slice(kvh * hd, (kvh + 1) * hd)
        kn = _rms(k_ref[:, ks], kw_ref[...])
        if latent:
            k_sc[0:ctx_len, ks] = kc_ref[:, ks].astype(BF16)
            v_sc[0:ctx_len, ks] = vc_ref[:, ks].astype(BF16)
            kn = _rope(kn, c_ref[...], s_ref[...], half)
        else:
            kout_ref[:, ks] = kn
        k_sc[ctx_len:lk, ks] = kn.astype(BF16)
        v_sc[ctx_len:lk, ks] = v_ref[:, ks].astype(BF16)

        def qtile(t, carry, kvh=kvh, ks=ks):
            q0 = pl.multiple_of(t * TQ, TQ)
            qs = []
            for g in range(grp):
                c0 = (kvh * grp + g) * hd
                qv = _rms(q_ref[pl.ds(q0, TQ), c0:c0 + hd], qw_ref[...])
                if latent:
                    qv = _rope(qv, c_ref[pl.ds(q0, TQ), :], s_ref[pl.ds(q0, TQ), :], half)
                qs.append(qv)
            qq = jnp.concatenate(qs, axis=0).astype(BF16)
            o = _softmax_pv(qq, k_sc[:, ks], v_sc[:, ks], scale)
            for g in range(grp):
                c0 = (kvh * grp + g) * hd
                o_ref[pl.ds(q0, TQ), c0:c0 + hd] = o[g * TQ:(g + 1) * TQ].astype(BF16)
            return carry

        lax.fori_loop(0, seq // TQ, qtile, 0)


def gqa_branch(proj, row0, batch, seq, lp, layer, ctx=None):
    latent = ctx is not None
    rb = row0 // seq
    ctx_len = ctx[0].shape[2] if latent else 0
    hd = GQA_HEAD_DIM
    qw = GQA_HEADS * hd
    kw = GQA_KV_HEADS * hd
    in_specs = [
        pl.BlockSpec((seq, qw), lambda b: (rb + b, C_GQ // qw)),
        pl.BlockSpec((seq, kw), lambda b: (rb + b, C_GK // kw)),
        pl.BlockSpec((seq, kw), lambda b: (rb + b, C_GV // kw)),
    ]
    args = [proj, proj, proj]
    if latent:
        in_specs += [
            pl.BlockSpec((None, None, ctx_len, kw), lambda b: (b, layer, 0, 0)),
            pl.BlockSpec((None, None, ctx_len, kw), lambda b: (b, layer, 0, 0)),
            pl.BlockSpec((seq, hd), lambda b: (0, 0)),
            pl.BlockSpec((seq, hd), lambda b: (0, 0)),
        ]
        args += [ctx[0], ctx[1], lp["rope_c64"], lp["rope_s64"]]
    in_specs += [pl.BlockSpec((1, hd), lambda b: (0, 0)), pl.BlockSpec((1, hd), lambda b: (0, 0))]
    args += [lp["gqa_q_norm"], lp["gqa_k_norm"]]
    out_specs = [pl.BlockSpec((seq, qw), lambda b: (b, 0))]
    out_shape = [jax.ShapeDtypeStruct((batch * seq, qw), BF16)]
    if not latent:
        out_specs += [pl.BlockSpec((seq, kw), lambda b: (b, 0))] * 2
        out_shape += [jax.ShapeDtypeStruct((batch * seq, kw), F32)] * 2
    res = pl.pallas_call(
        functools.partial(_gqa_kernel, seq=seq, ctx_len=ctx_len, latent=latent),
        grid=(batch,),
        in_specs=in_specs,
        out_specs=out_specs,
        out_shape=out_shape,
        scratch_shapes=[pltpu.VMEM((ctx_len + seq, kw), BF16), pltpu.VMEM((ctx_len + seq, kw), BF16)],
        compiler_params=_cparams(("parallel",)),
        name="gqa_latent" if latent else "gqa_context",
    )(*args)
    return res if not latent else res[0]


def _mla_kernel(*refs, seq, ctx_len, latent):
    if latent:
        (cq_ref, ckv_ref, small_ref, ckvc_ref, krc_ref, c_ref, s_ref, qw_ref, kvw_ref, wuq_ref, wuk_ref,
         wuv_ref, o_ref, kn_sc, v_sc, kr_sc) = refs
    else:
        (cq_ref, ckv_ref, small_ref, qw_ref, kvw_ref, wuq_ref, wuk_ref, wuv_ref,
         o_ref, ckv_out_ref, kr_out_ref, kn_sc, v_sc, kr_sc) = refs
    lk = ctx_len + seq
    half = MLA_ROPE // 2
    ckv = _rms(ckv_ref[...], kvw_ref[...])
    if latent:
        cc = ckvc_ref[...].astype(BF16)
        kn_sc[0:ctx_len, :] = _dot(cc, wuk_ref[...]).astype(BF16)
        v_sc[0:ctx_len, :] = _dot(cc, wuv_ref[...]).astype(BF16)
        kr_sc[0:ctx_len, :] = krc_ref[...].astype(BF16)
    else:
        ckv_out_ref[...] = ckv
        kr_out_ref[...] = small_ref[:, 0:MLA_ROPE]
    cb = ckv.astype(BF16)
    kn_sc[ctx_len:lk, :] = _dot(cb, wuk_ref[...]).astype(BF16)
    v_sc[ctx_len:lk, :] = _dot(cb, wuv_ref[...]).astype(BF16)
    kr = small_ref[...]
    if latent:
        kr = _rope(kr, c_ref[...], s_ref[...], half)
    lane_lo = lax.broadcasted_iota(jnp.int32, (seq, LANES), 1) < MLA_ROPE
    kr_sc[ctx_len:lk, :] = jnp.where(lane_lo, kr, 0.0).astype(BF16)
    scale = (MLA_NOPE + MLA_ROPE) ** -0.5
    nope_w = MLA_HEADS * MLA_NOPE

    def qtile(t, carry):
        q0 = pl.multiple_of(t * TQ, TQ)
        cq = _dot(_rms(cq_ref[pl.ds(q0, TQ), :], qw_ref[...]).astype(BF16), wuq_ref[...])
        for h in range(MLA_HEADS):
            hs = slice(h * MLA_NOPE, (h + 1) * MLA_NOPE)
            qn = cq[:, hs].astype(BF16)
            qr = cq[:, nope_w + h * LANES:nope_w + (h + 1) * LANES]
            if latent:
                qr = _rope(qr, c_ref[pl.ds(q0, TQ), :], s_ref[pl.ds(q0, TQ), :], half)
            s = (_dot_nt(qn, kn_sc[:, hs]) + _dot_nt(qr.astype(BF16), kr_sc[...])) * scale
            m = jnp.max(s, axis=-1, keepdims=True)
            p = jnp.exp(s - m)
            l = jnp.sum(p, axis=-1, keepdims=True)
            o = _dot(p.astype(BF16), v_sc[:, hs]) / l
            o_ref[pl.ds(q0, TQ), hs] = o.astype(BF16)
        return carry

    lax.fori_loop(0, seq // TQ, qtile, 0)


def mla_branch(proj, row0, batch, seq, lp, layer, ctx=None):
    latent = ctx is not None
    rb = row0 // seq
    ctx_len = ctx[0].shape[2] if latent else 0
    const = lambda b: (0, 0)
    in_specs = [
        pl.BlockSpec((seq, MLA_Q_RANK), lambda b: (rb + b, C_MCQ // MLA_Q_RANK)),
        pl.BlockSpec((seq, MLA_KV_RANK), lambda b: (rb + b, C_MCKV // MLA_KV_RANK)),
        pl.BlockSpec((seq, LANES), lambda b: (rb + b, C_SMALL // LANES)),
    ]
    args = [proj, proj, proj]
    if latent:
        in_specs += [
            pl.BlockSpec((None, None, ctx_len, MLA_KV_RANK), lambda b: (b, layer, 0, 0)),
            pl.BlockSpec((None, None, ctx_len, LANES), lambda b: (b, layer, 0, 0)),
            pl.BlockSpec((seq, LANES), const),
            pl.BlockSpec((seq, LANES), const),
        ]
        args += [ctx[0], ctx[1], lp["rope_c32"], lp["rope_s32"]]
    in_specs += [
        pl.BlockSpec((1, MLA_Q_RANK), const),
        pl.BlockSpec((1, MLA_KV_RANK), const),
        pl.BlockSpec((MLA_Q_RANK, 2 * MLA_HEADS * LANES), const),
        pl.BlockSpec((MLA_KV_RANK, MLA_HEADS * MLA_NOPE), const),
        pl.BlockSpec((MLA_KV_RANK, BRANCH_W), const),
    ]
    args += [lp["mla_q_norm"], lp["mla_kv_norm"], lp["w_uq"], lp["w_uk"], lp["w_uv"]]
    out_specs = [pl.BlockSpec((seq, BRANCH_W), lambda b: (b, 0))]
    out_shape = [jax.ShapeDtypeStruct((batch * seq, BRANCH_W), BF16)]
    if not latent:
        out_specs += [pl.BlockSpec((seq, MLA_KV_RANK), lambda b: (b, 0)),
                      pl.BlockSpec((seq, MLA_ROPE), lambda b: (b, 0))]
        out_shape += [jax.ShapeDtypeStruct((batch * seq, MLA_KV_RANK), F32),
                      jax.ShapeDtypeStruct((batch * seq, MLA_ROPE), F32)]
    lk = ctx_len + seq
    res = pl.pallas_call(
        functools.partial(_mla_kernel, seq=seq, ctx_len=ctx_len, latent=latent),
        grid=(batch,),
        in_specs=in_specs,
        out_specs=out_specs,
        out_shape=out_shape,
        scratch_shapes=[pltpu.VMEM((lk, MLA_HEADS * MLA_NOPE), BF16), pltpu.VMEM((lk, BRANCH_W), BF16),
                        pltpu.VMEM((lk, LANES), BF16)],
        compiler_params=_cparams(("parallel",)),
        name="mla_latent" if latent else "mla_context",
    )(*args)
    return res if not latent else res[0]


def _pack_pairs(lo, hi):
    lo_bits = lax.bitcast_convert_type(lo.astype(BF16).astype(F32), jnp.uint32)
    hi_bits = lax.bitcast_convert_type(hi.astype(BF16).astype(F32), jnp.uint32)
    return (lo_bits >> jnp.uint32(16)) | (hi_bits & jnp.uint32(0xFFFF0000))


def _unpack_pairs(w):
    lo = lax.bitcast_convert_type(w << jnp.uint32(16), F32)
    hi = lax.bitcast_convert_type(w & jnp.uint32(0xFFFF0000), F32)
    return lo, hi


def _merge_kernel(*refs, n_ctx_tiles):
    oc_refs, ol_refs, g_refs = refs[0:4], refs[4:8], refs[8:12]
    (wb_ref, wo_ref, x_ref, ga_ref, shf_ref, scf_ref, lg_ref, lb_ref, rw_ref, rb_ref,
     x1_ref, u2_ref, lo_ref, acc_sc) = refs[12:]
    i = pl.program_id(0)
    j = pl.program_id(1)

    @pl.when(j == 0)
    def _():
        acc_sc[...] = jnp.zeros(acc_sc.shape, F32)

    is_ctx = i < n_ctx_tiles
    merged = None
    for n in range(N_BRANCH):
        o = jnp.where(is_ctx, oc_refs[n][...], ol_refs[n][...])
        term = _sigmoid(g_refs[n][...].astype(F32)) * _dot(o, wb_ref[n])
        merged = term if merged is None else merged + term
    acc_sc[...] += _dot(merged.astype(BF16), wo_ref[...])

    @pl.when(j == pl.num_programs(1) - 1)
    def _():
        h = DEEPNORM_ALPHA * x_ref[...] + ga_ref[...] * acc_sc[...]
        x1 = _ln(h) * lg_ref[...] + lb_ref[...]
        x1_ref[...] = x1
        u2 = (_ln(x1) * (1.0 + scf_ref[...]) + shf_ref[...]).astype(BF16)
        half = D_MODEL // 2
        u2f = u2.astype(F32)
        u2_ref[...] = _pack_pairs(u2f[:, :half], u2f[:, half:])
        lo_ref[...] = _dot(u2, rw_ref[...]) + rb_ref[...]


def merge_and_route(outs_ctx, outs_lat, gates, x, mods, lp, layer, n_ctx_rows, lat_len):
    t = x.shape[0]
    tm, tn = TM_MERGE, TN_MERGE
    n_ctx_tiles = n_ctx_rows // tm
    row = functools.partial(_mod_row, tm=tm, n_ctx_rows=n_ctx_rows, lat_len=lat_len)
    gate_spec = lambda n: pl.BlockSpec((tm, tn), lambda i, j: (i, n * D_MODEL // tn + j))
    vec = lambda k: pl.BlockSpec((None, 1, D_MODEL), lambda i, j: (row(i), 0, k))
    cvec = pl.BlockSpec((None, 1, D_MODEL), lambda i, j: (layer, 0, 0))
    in_specs = (
        [pl.BlockSpec((tm, BRANCH_W), lambda i, j: (jnp.minimum(i, n_ctx_tiles - 1), 0))] * N_BRANCH
        + [pl.BlockSpec((tm, BRANCH_W), lambda i, j: (jnp.maximum(i - n_ctx_tiles, 0), 0))] * N_BRANCH
        + [gate_spec(n) for n in range(N_BRANCH)]
        + [
            pl.BlockSpec((None, N_BRANCH, BRANCH_W, tn), lambda i, j: (layer, 0, 0, j)),
            pl.BlockSpec((None, tn, D_MODEL), lambda i, j: (layer, j, 0)),
            pl.BlockSpec((tm, D_MODEL), lambda i, j: (i, 0)),
            vec(2), vec(3), vec(4), cvec, cvec,
            pl.BlockSpec((None, D_MODEL, LANES), lambda i, j: (layer, 0, 0)),
            pl.BlockSpec((None, 1, LANES), lambda i, j: (layer, 0, 0)),
        ]
    )
    return pl.pallas_call(
        functools.partial(_merge_kernel, n_ctx_tiles=n_ctx_tiles),
        grid=(t // tm, D_MODEL // tn),
        in_specs=in_specs,
        out_specs=[
            pl.BlockSpec((tm, D_MODEL), lambda i, j: (i, 0)),
            pl.BlockSpec((tm, D_MODEL // 2), lambda i, j: (i, 0)),
            pl.BlockSpec((tm, LANES), lambda i, j: (i, 0)),
        ],
        out_shape=[
            jax.ShapeDtypeStruct((t, D_MODEL), F32),
            jax.ShapeDtypeStruct((t, D_MODEL // 2), jnp.uint32),
            jax.ShapeDtypeStruct((t, LANES), F32),
        ],
        scratch_shapes=[pltpu.VMEM((tm, D_MODEL), F32)],
        compiler_params=_cparams(("parallel", "arbitrary")),
        name="merge_route",
    )(*outs_ctx, *outs_lat, gates, gates, gates, gates, lp["w_branch"], lp["w_out"], x, mods, mods, mods,
      lp["ln_mix_g"], lp["ln_mix_b"], lp["router_w"], lp["router_b"])


def _moe_up_kernel(*refs):
    n = N_WPIECE
    valid_ref, first_ref = refs[2], refs[3]
    x_ref = refs[4 + n]
    wg_refs = refs[5 + n:5 + 2 * n]
    wl_refs = refs[5 + 2 * n:5 + 3 * n]
    bg_ref, bl_ref, h_ref, wg_sc, wl_sc = refs[5 + 3 * n:]
    blk = pl.program_id(0)
    pw = EXPERT_FF // n

    @pl.when(first_ref[blk] == 1)
    def _():
        for c in range(n):
            wg_sc[:, c * pw:(c + 1) * pw] = wg_refs[c][...].astype(BF16)
            wl_sc[:, c * pw:(c + 1) * pw] = wl_refs[c][...].astype(BF16)

    @pl.when(valid_ref[blk] == 1)
    def _():
        lo, hi = _unpack_pairs(x_ref[...])
        x = jnp.concatenate([lo, hi], axis=1).astype(BF16)
        g = jnp.minimum(_dot(x, wg_sc[...]) + bg_ref[...], SWIGLU_LIMIT)
        lin = jnp.clip(_dot(x, wl_sc[...]) + bl_ref[...], -SWIGLU_LIMIT, SWIGLU_LIMIT)
        h_ref[...] = ((lin + 1.0) * (g * _sigmoid(SWIGLU_ALPHA * g))).astype(BF16)

    @pl.when(valid_ref[blk] == 0)
    def _():
        h_ref[...] = jnp.zeros(h_ref.shape, BF16)


def _moe_down_kernel(*refs):
    n = N_WPIECE
    valid_ref, first_ref = refs[2], refs[3]
    h_ref = refs[4 + n]
    wd_refs = refs[5 + n:5 + 2 * n]
    bd_ref, y_ref, wd_sc = refs[5 + 2 * n:]
    blk = pl.program_id(0)
    ph = EXPERT_FF // n

    @pl.when(first_ref[blk] == 1)
    def _():
        for c in range(n):
            wd_sc[c * ph:(c + 1) * ph, :] = wd_refs[c][...].astype(BF16)

    @pl.when(valid_ref[blk] == 1)
    def _():
        y = _dot(h_ref[...], wd_sc[...]) + bd_ref[...]
        half = D_MODEL // 2
        y_ref[...] = _pack_pairs(y[:, :half], y[:, half:])

    @pl.when(valid_ref[blk] == 0)
    def _():
        y_ref[...] = jnp.zeros(y_ref.shape, jnp.uint32)


def moe_experts(xs, plan, w_gu, b_gu, w_down, b_down, layer):
    p = xs.shape[0]
    nb = p // TM_MOE
    tm = TM_MOE
    half = D_MODEL // 2
    n = N_WPIECE
    pw = EXPERT_FF // n
    nlin = EXPERT_FF // pw
    pre = 4 + n

    def spec(shape, fn):
        return pl.BlockSpec(shape, lambda b, *pf: fn(b, pf))

    hidden = pl.pallas_call(
        _moe_up_kernel,
        grid_spec=pltpu.PrefetchScalarGridSpec(
            num_scalar_prefetch=pre,
            grid=(nb,),
            in_specs=(
                [spec((tm, half), lambda b, pf: (pf[1][b], 0))]
                + [spec((None, None, D_MODEL, pw), lambda b, pf, c=c: (layer, pf[4 + c][b], 0, c)) for c in range(n)]
                + [spec((None, None, D_MODEL, pw), lambda b, pf, c=c: (layer, pf[4 + c][b], 0, nlin + c))
                   for c in range(n)]
                + [spec((None, None, 1, EXPERT_FF), lambda b, pf: (layer, pf[0][b], 0, 0)),
                   spec((None, None, 1, EXPERT_FF), lambda b, pf: (layer, pf[0][b], 0, 1))]
            ),
            out_specs=spec((tm, EXPERT_FF), lambda b, pf: (b, 0)),
            scratch_shapes=[pltpu.VMEM((D_MODEL, EXPERT_FF), BF16), pltpu.VMEM((D_MODEL, EXPERT_FF), BF16)],
        ),
        out_shape=jax.ShapeDtypeStruct((p, EXPERT_FF), BF16),
        compiler_params=_cparams(("arbitrary",)),
        name="moe_up",
    )(*plan, xs, *([w_gu] * (2 * n)), b_gu, b_gu)
    return pl.pallas_call(
        _moe_down_kernel,
        grid_spec=pltpu.PrefetchScalarGridSpec(
            num_scalar_prefetch=pre,
            grid=(nb,),
            in_specs=(
                [spec((tm, EXPERT_FF), lambda b, pf: (pf[1][b], 0))]
                + [spec((None, None, pw, D_MODEL), lambda b, pf, c=c: (layer, pf[4 + c][b], c, 0)) for c in range(n)]
                + [spec((None, None, 1, D_MODEL), lambda b, pf: (layer, pf[0][b], 0, 0))]
            ),
            out_specs=spec((tm, half), lambda b, pf: (b, 0)),
            scratch_shapes=[pltpu.VMEM((EXPERT_FF, D_MODEL), BF16)],
        ),
        out_shape=jax.ShapeDtypeStruct((p, half), jnp.uint32),
        compiler_params=_cparams(("arbitrary",)),
        name="moe_down",
    )(*plan, hidden, *([w_down] * n), b_down)


def route(logits):
    t = logits.shape[0]
    a = t * TOP_K
    top_v, top_i = lax.top_k(logits, TOP_K)
    gate = jax.nn.softmax(top_v, axis=-1)
    flat_e = top_i.reshape(-1)
    onehot = (flat_e[:, None] == jnp.arange(N_EXPERTS, dtype=jnp.int32)[None, :]).astype(jnp.int32)
    csum = jnp.cumsum(onehot, axis=0)
    counts = csum[-1]
    padded = (counts + TM_MOE - 1) // TM_MOE * TM_MOE
    ends = jnp.cumsum(padded)
    slot = jnp.sum(onehot * (csum - 1 + (ends - padded)[None, :]), axis=1)
    nb = a // TM_MOE + N_EXPERTS
    blk = jnp.arange(nb, dtype=jnp.int32)
    n_used = (ends[-1] // TM_MOE).astype(jnp.int32)
    src = jnp.minimum(blk, n_used - 1)
    blk_e = jnp.minimum(jnp.searchsorted(ends, src * TM_MOE, side="right"), N_EXPERTS - 1).astype(jnp.int32)
    valid = (blk < n_used).astype(jnp.int32)
    first = jnp.where((blk == 0) | (blk_e != jnp.roll(blk_e, 1)), 1, 0).astype(jnp.int32)
    key_bits = 16
    assert a < (1 << key_bits)
    skey = lax.sort(flat_e * (1 << key_bits) + jnp.arange(a, dtype=jnp.int32))
    sorted_t = (skey & ((1 << key_bits) - 1)) // TOP_K
    start = jnp.cumsum(counts) - counts
    blk_off = blk * TM_MOE - (ends - padded)[blk_e]
    base = start[blk_e] + blk_off
    room = jnp.where(valid == 1, counts[blk_e] - blk_off, 0)
    r = jnp.arange(TM_MOE, dtype=jnp.int32)[None, :]
    live = r < room[:, None]
    filler = (blk[:, None] * TM_MOE + r) % t
    tok = jnp.where(live, sorted_t[jnp.clip(base[:, None] + r, 0, a - 1)], filler).reshape(-1)
    tok = lax.optimization_barrier(tok)
    run_start = lax.cummax(jnp.where(first == 1, blk, 0))
    next_start = jnp.concatenate([lax.cummin(jnp.where(first == 1, blk, nb), reverse=True)[1:],
                                  jnp.full((1,), nb, jnp.int32)])
    e_next = jnp.where(next_start < nb, blk_e[jnp.minimum(next_start, nb - 1)], blk_e)
    pieces = tuple(jnp.where(blk - run_start > c, e_next, blk_e).astype(jnp.int32) for c in range(N_WPIECE))
    return gate, slot.reshape(t, TOP_K), tok, (blk_e, src, valid, first) + pieces


def _ffn_norm_kernel(x_ref, y_ref, gate_ref, gf_ref, lg_ref, lb_ref, o_ref):
    gate = gate_ref[...]
    ffn_lo = ffn_hi = None
    for k in range(TOP_K):
        lo, hi = _unpack_pairs(y_ref[k])
        gk = gate[:, k:k + 1]
        ffn_lo = lo * gk if ffn_lo is None else ffn_lo + lo * gk
        ffn_hi = hi * gk if ffn_hi is None else ffn_hi + hi * gk
    ffn = jnp.concatenate([ffn_lo, ffn_hi], axis=1)
    h = DEEPNORM_ALPHA * x_ref[...] + gf_ref[...] * ffn
    o_ref[...] = _ln(h) * lg_ref[...] + lb_ref[...]


def ffn_norm(x1, y_rows, gate, mods, lp, layer, n_ctx_rows, lat_len):
    t = x1.shape[0]
    tm = TM_LN
    row = functools.partial(_mod_row, tm=tm, n_ctx_rows=n_ctx_rows, lat_len=lat_len)
    cvec = pl.BlockSpec((None, 1, D_MODEL), lambda i: (layer, 0, 0))
    return pl.pallas_call(
        _ffn_norm_kernel,
        grid=(t // tm,),
        in_specs=[
            pl.BlockSpec((tm, D_MODEL), lambda i: (i, 0)),
            pl.BlockSpec((TOP_K, tm, D_MODEL // 2), lambda i: (0, i, 0)),
            pl.BlockSpec((tm, TOP_K), lambda i: (i, 0)),
            pl.BlockSpec((None, 1, D_MODEL), lambda i: (row(i), 0, 5)),
            cvec, cvec,
        ],
        out_specs=pl.BlockSpec((tm, D_MODEL), lambda i: (i, 0)),
        out_shape=jax.ShapeDtypeStruct((t, D_MODEL), F32),
        compiler_params=_cparams(("parallel",)),
        name="ffn_norm",
    )(x1, y_rows, gate, mods, lp["ln_ffn_g"], lp["ln_ffn_b"])


def _rope_tables(n_tokens, seg, width):
    rows = n_tokens // GRID_W
    row = jnp.repeat(jnp.arange(rows, dtype=F32), GRID_W)
    col = jnp.tile(jnp.arange(GRID_W, dtype=F32), rows)
    n_freq = seg // 4
    inv = ROPE_THETA ** (-jnp.arange(n_freq, dtype=F32) / n_freq)
    ang = jnp.concatenate([row[:, None] * inv, col[:, None] * inv], axis=-1)
    cos, sin = jnp.cos(ang), jnp.sin(ang)
    reps = width // seg
    return (jnp.tile(jnp.concatenate([cos, cos], axis=-1), (1, reps)),
            jnp.tile(jnp.concatenate([-sin, sin], axis=-1), (1, reps)))


def _pack_w_in_kernel(w_ref, o_ref):
    def put(dst, src, n):
        for t0 in range(0, n, LANES):
            o_ref[:, dst + t0:dst + t0 + LANES] = w_ref[src + t0:src + t0 + LANES, :].T.astype(BF16)

    put(C_XBC, BRANCH_W, SSD_XBC)
    put(C_Z, 0, BRANCH_W)
    put(C_DQ, _O_DQ, _O_MCKV - _O_DQ)
    put(C_GQ, _O_GQ, _O_GK - _O_GQ)
    put(C_MCKV, _O_MCKV, MLA_KV_RANK)
    put(C_GK, _O_GK, _O_GATE - _O_GK)
    n_dt = 2 * SSD_HEADS
    small = jnp.concatenate([w_ref[_O_MKR:_O_MKR + MLA_ROPE, :], w_ref[_O_DT:_O_DT + n_dt, :],
                             jnp.zeros((LANES - MLA_ROPE - n_dt, TR_PACK), F32)], axis=0)
    o_ref[:, C_SMALL:C_SMALL + LANES] = small.T.astype(BF16)
    o_ref[:, C_SMALL + LANES:C_GATE] = jnp.zeros((TR_PACK, C_GATE - C_SMALL - LANES), BF16)
    put(C_GATE, _O_GATE, N_BRANCH * D_MODEL)


def _pack_w_in(w_in):
    tr = TR_PACK
    return pl.pallas_call(
        _pack_w_in_kernel,
        grid=(DEPTH, D_MODEL // tr),
        in_specs=[pl.BlockSpec((None, IN_WIDTH, tr), lambda l, r: (l, 0, r))],
        out_specs=pl.BlockSpec((None, tr, PROJ_W), lambda l, r: (l, r, 0)),
        out_shape=jax.ShapeDtypeStruct((DEPTH, D_MODEL, PROJ_W), BF16),
        compiler_params=_cparams(("parallel", "parallel")),
        name="pack_w_in",
    )(jnp.transpose(w_in, (0, 2, 1)))


def _pack_w_uq(w_uq):
    w = w_uq.reshape(MLA_Q_RANK, MLA_HEADS, MLA_NOPE + MLA_ROPE)
    nope = w[:, :, :MLA_NOPE].reshape(MLA_Q_RANK, MLA_HEADS * MLA_NOPE)
    rope = jnp.pad(w[:, :, MLA_NOPE:], ((0, 0), (0, 0), (0, LANES - MLA_ROPE)))
    return jnp.concatenate([nope, rope.reshape(MLA_Q_RANK, MLA_HEADS * LANES)], axis=-1).astype(BF16)


def _small_row(v16):
    return jnp.zeros((1, LANES), F32).at[0, SMALL_DT:SMALL_DT + 2 * SSD_HEADS].set(v16.reshape(-1))


def kernel(x_prompt, x_sample, state_ssd, cache_diff_k, cache_diff_v, cache_mla_ckv, cache_mla_kr, cache_gqa_k, cache_gqa_v, c, c_ctx, w_mod, b_mod, w_in, ssd_conv_w, ssd_conv_b, ssd_dt_bias, ssd_a_log, ssd_d, ssd_norm, diff_lambda, diff_norm, mla_q_norm, mla_w_uq, mla_kv_norm, mla_w_uk, mla_w_uv, gqa_q_norm, gqa_k_norm, w_branch, w_out, ln_mix_g, ln_mix_b, router_w, router_b, exp_w_gu, exp_b_gu, exp_w_down, exp_b_down, ln_ffn_g, ln_ffn_b):
    nb_ctx, len_ctx, _ = x_prompt.shape
    nb_lat, len_lat, _ = x_sample.shape
    past = cache_diff_k.shape[2]
    n_ctx_rows = nb_ctx * len_ctx
    t = n_ctx_rows + nb_lat * len_lat

    cvec = jnp.zeros((SUBLANES, D_MODEL), F32).at[0].set(c_ctx).at[1:1 + nb_lat].set(c)
    mods_all = modulation(cvec, w_mod, b_mod)

    w_in_p = _pack_w_in(w_in)
    w_branch_b = w_branch.astype(BF16)
    w_out_b = w_out.astype(BF16)
    router_w_p = jnp.pad(router_w, ((0, 0), (0, 0), (0, LANES - N_EXPERTS))).astype(BF16)
    router_b_p = jnp.pad(router_b, ((0, 0), (0, LANES - N_EXPERTS))).reshape(DEPTH, 1, LANES)
    c32, s32 = _rope_tables(len_lat, 2 * (DIFF_D // 2), LANES)
    c64, s64 = _rope_tables(len_lat, GQA_HEAD_DIM, LANES)
    n_pair = SSD_HEADS // 2
    state_in = state_ssd.reshape(nb_lat, DEPTH, 2, n_pair, 2 * SSD_HEAD_DIM, SSD_STATE)
    diff_kc = cache_diff_k.reshape(nb_lat, DEPTH, past, BRANCH_W)
    diff_vc = cache_diff_v.reshape(nb_lat, DEPTH, past, BRANCH_W)
    mla_krc = jnp.pad(cache_mla_kr, ((0, 0), (0, 0), (0, 0), (0, LANES - MLA_ROPE)))
    gqa_kc = cache_gqa_k.reshape(nb_lat, DEPTH, past, GQA_KV_HEADS * GQA_HEAD_DIM)
    gqa_vc = cache_gqa_v.reshape(nb_lat, DEPTH, past, GQA_KV_HEADS * GQA_HEAD_DIM)
    vecs = dict(ln_mix_g=ln_mix_g.reshape(DEPTH, 1, D_MODEL), ln_mix_b=ln_mix_b.reshape(DEPTH, 1, D_MODEL),
                ln_ffn_g=ln_ffn_g.reshape(DEPTH, 1, D_MODEL), ln_ffn_b=ln_ffn_b.reshape(DEPTH, 1, D_MODEL),
                w_branch=w_branch_b, w_out=w_out_b, router_w=router_w_p, router_b=router_b_p)
    b_gu = exp_b_gu.reshape(DEPTH, N_EXPERTS, 1, 2 * EXPERT_FF)
    b_down = exp_b_down.reshape(DEPTH, N_EXPERTS, 1, D_MODEL)

    x = jnp.concatenate([x_prompt.reshape(n_ctx_rows, D_MODEL), x_sample.reshape(-1, D_MODEL)], axis=0)
    new_state, new_dk, new_dv, new_ckv, new_kr, new_gk, new_gv = [], [], [], [], [], [], []
    for l in range(DEPTH):
        lp = dict(vecs)
        lp.update(
            conv_w=jnp.pad(ssd_conv_w[l], ((0, SUBLANES - SSD_CONV), (0, 0))),
            conv_b=ssd_conv_b[l].reshape(1, SSD_XBC),
            dt_bias=_small_row(ssd_dt_bias[l]),
            a_row=_small_row(-jnp.exp(ssd_a_log[l])),
            d_row=jnp.repeat(ssd_d[l], SSD_HEAD_DIM).reshape(1, BRANCH_W),
            ssd_norm=ssd_norm[l].reshape(1, BRANCH_W),
            diff_lambda=diff_lambda[l],
            diff_norm=diff_norm[l].reshape(1, 2 * DIFF_D),
            mla_q_norm=mla_q_norm[l].reshape(1, MLA_Q_RANK),
            mla_kv_norm=mla_kv_norm[l].reshape(1, MLA_KV_RANK),
            w_uq=_pack_w_uq(mla_w_uq[l]),
            w_uk=mla_w_uk[l].astype(BF16),
            w_uv=mla_w_uv[l].astype(BF16),
            gqa_q_norm=gqa_q_norm[l].reshape(1, GQA_HEAD_DIM),
            gqa_k_norm=gqa_k_norm[l].reshape(1, GQA_HEAD_DIM),
            rope_c32=c32, rope_s32=s32, rope_c64=c64, rope_s64=s64,
        )
        mods = mods_all[l].reshape(SUBLANES, 1, 6 * D_MODEL)
        proj, gates = in_projection(x, mods, w_in_p, l, n_ctx_rows, len_lat)

        o_ssd_c, st = ssd_branch(proj, 0, nb_ctx, len_ctx, lp, l)
        o_ssd_l = ssd_branch(proj, n_ctx_rows, nb_lat, len_lat, lp, l, state_in)
        o_diff_c, dk_c, dv_c = diff_branch(proj, 0, nb_ctx, len_ctx, lp, l)
        o_diff_l = diff_branch(proj, n_ctx_rows, nb_lat, len_lat, lp, l, (diff_kc, diff_vc))
        o_mla_c, ckv_c, kr_c = mla_branch(proj, 0, nb_ctx, len_ctx, lp, l)
        o_mla_l = mla_branch(proj, n_ctx_rows, nb_lat, len_lat, lp, l, (cache_mla_ckv, mla_krc))
        o_gqa_c, gk_c, gv_c = gqa_branch(proj, 0, nb_ctx, len_ctx, lp, l)
        o_gqa_l = gqa_branch(proj, n_ctx_rows, nb_lat, len_lat, lp, l, (gqa_kc, gqa_vc))
        outs_ctx = (o_ssd_c, o_diff_c, o_mla_c, o_gqa_c)
        outs_lat = (o_ssd_l, o_diff_l, o_mla_l, o_gqa_l)

        new_state.append(st.reshape(nb_ctx, 2, SSD_HEADS, SSD_HEAD_DIM, SSD_STATE))
        new_dk.append(dk_c.reshape(nb_ctx, len_ctx, DIFF_HEADS, 2 * DIFF_D))
        new_dv.append(dv_c.reshape(nb_ctx, len_ctx, DIFF_HEADS, 2 * DIFF_D))
        new_ckv.append(ckv_c.reshape(nb_ctx, len_ctx, MLA_KV_RANK))
        new_kr.append(kr_c.reshape(nb_ctx, len_ctx, MLA_ROPE))
        new_gk.append(gk_c.reshape(nb_ctx, len_ctx, GQA_KV_HEADS, GQA_HEAD_DIM))
        new_gv.append(gv_c.reshape(nb_ctx, len_ctx, GQA_KV_HEADS, GQA_HEAD_DIM))

        x1, u2, logits = merge_and_route(outs_ctx, outs_lat, gates, x, mods, lp, l, n_ctx_rows, len_lat)
        gate, slot, tok, plan = route(logits[:, :N_EXPERTS])
        u2 = jnp.pad(u2, ((0, max(0, SC_GATHER_MIN_ROWS - t)), (0, 0)))
        yb = moe_experts(u2[tok], plan, exp_w_gu, b_gu, exp_w_down, b_down, l)
        y_rows = yb[slot.T.reshape(-1)].reshape(TOP_K, t, D_MODEL // 2)
        x = ffn_norm(x1, y_rows, gate, mods, lp, l, n_ctx_rows, len_lat)

    y_prompt = x[:n_ctx_rows].reshape(nb_ctx, len_ctx, D_MODEL)
    y_sample = x[n_ctx_rows:].reshape(nb_lat, len_lat, D_MODEL)
    stack = lambda parts: jnp.stack(parts, axis=1)
    return (y_prompt, y_sample, stack(new_state), stack(new_dk), stack(new_dv), stack(new_ckv),
            stack(new_kr), stack(new_gk), stack(new_gv))
```
